```python
import math
import jax, jax.numpy as jnp
from jax import lax
import numpy as np

D_MODEL = 1024
BATCH = 16
SEQ = 2048
DEPTH = 1

CHUNK = 64
Q_BLOCK = 128
HEAD_DIM = 64
DIFF_HEADS = 4
DIFF_V_DIM = 2 * HEAD_DIM
DIFF_WIDTH = DIFF_HEADS * DIFF_V_DIM
FOX_HEADS = 8
FOX_WIDTH = FOX_HEADS * HEAD_DIM
N_BRANCHES = 2
D_FF = 4 * D_MODEL
REL_BUCKETS = 32
REL_MAX_DIST = 128
EPS = 1e-6
FORGET_BIAS_INIT = 2.0

DIFF_QK_COLS = DIFF_HEADS * 2 * HEAD_DIM
COL_SIZES = [DIFF_QK_COLS, DIFF_QK_COLS, DIFF_WIDTH,
             FOX_WIDTH, FOX_WIDTH, FOX_WIDTH, FOX_HEADS,
             N_BRANCHES * D_MODEL]
IN_COLS = sum(COL_SIZES)
SPLITS = [sum(COL_SIZES[:i + 1]) for i in range(len(COL_SIZES) - 1)]

kernel_name = "hybrid_diff_fox_gated_encoder"


def rmsnorm(x, g):
    xf = x.astype(jnp.float32)
    y = xf * lax.rsqrt(jnp.mean(xf * xf, axis=-1, keepdims=True) + EPS)
    return (y * g.astype(jnp.float32)).astype(x.dtype)


def lambda_init_fn(layer_idx):
    return 0.8 - 0.6 * math.exp(-0.3 * layer_idx)


def rel_bucket(rel):
    nb = REL_BUCKETS // 2
    ret = jnp.where(rel > 0, nb, 0)
    n = jnp.abs(rel)
    max_exact = nb // 2
    nf = jnp.maximum(n, 1).astype(jnp.float32)
    large = max_exact + (jnp.log(nf / max_exact) / math.log(REL_MAX_DIST / max_exact)
                         * (nb - max_exact)).astype(jnp.int32)
    large = jnp.minimum(large, nb - 1)
    return ret + jnp.where(n < max_exact, n, large)


def attention_branches(q_d, k_d, v_d, q_f, k_f, v_f, fcum, rel_table, lam):
    S = q_d.shape[1]
    scale = HEAD_DIM ** -0.5
    pos = jnp.arange(S, dtype=jnp.int32)
    outs_d, outs_f = [], []
    for blk in range(S // Q_BLOCK):
        q0 = blk * Q_BLOCK
        kend = q0 + Q_BLOCK
        qp = pos[q0:kend]
        kp = pos[:kend]
        rel = kp[None, :] - qp[:, None]
        bias = jnp.transpose(rel_table[rel_bucket(rel)], (2, 0, 1))
        chunk_mask = (kp[None, :] // CHUNK) <= (qp[:, None] // CHUNK)
        frame_mask = kp[None, :] <= qp[:, None]

        s_d = jnp.einsum('bqhmd,bkhmd->bhmqk', q_d[:, q0:kend], k_d[:, :kend]).astype(jnp.float32)
        s_d = s_d * scale + bias[None, :, None].astype(jnp.float32)
        s_d = jnp.where(chunk_mask, s_d, -jnp.inf)
        p_d = jax.nn.softmax(s_d, axis=-1)
        w_d = p_d[:, :, 0] - lam * p_d[:, :, 1]
        outs_d.append(jnp.einsum('bhqk,bkhe->bqhe', w_d.astype(v_d.dtype), v_d[:, :kend]))

        s_f = jnp.einsum('bqhd,bkhd->bhqk', q_f[:, q0:kend], k_f[:, :kend]).astype(jnp.float32)
        s_f = s_f * scale + (fcum[:, :, q0:kend, None] - fcum[:, :, None, :kend])
        s_f = jnp.where(frame_mask, s_f, -jnp.inf)
        p_f = jax.nn.softmax(s_f, axis=-1)
        outs_f.append(jnp.einsum('bhqk,bkhd->bqhd', p_f.astype(v_f.dtype), v_f[:, :kend]))
    return jnp.concatenate(outs_d, axis=1), jnp.concatenate(outs_f, axis=1)


def hybrid_layer(x, layer_idx, g_mix, w_in, b_f, lam_q1, lam_k1, lam_q2, lam_k2, g_subln,
                 w_pa, w_pb, w_o, g_mlp, w_1, w_2, rel_table):
    B, S, _ = x.shape
    h = rmsnorm(x, g_mix)
    proj = h @ w_in
    dq, dk, dv, fq, fk, fv, fl, gl = jnp.split(proj, SPLITS, axis=-1)
    q_d = dq.reshape(B, S, DIFF_HEADS, 2, HEAD_DIM)
    k_d = dk.reshape(B, S, DIFF_HEADS, 2, HEAD_DIM)
    v_d = dv.reshape(B, S, DIFF_HEADS, DIFF_V_DIM)
    q_f = fq.reshape(B, S, FOX_HEADS, HEAD_DIM)
    k_f = fk.reshape(B, S, FOX_HEADS, HEAD_DIM)
    v_f = fv.reshape(B, S, FOX_HEADS, HEAD_DIM)

    logf = jax.nn.log_sigmoid((fl + b_f).astype(jnp.float32))
    fcum = jnp.transpose(jnp.cumsum(logf, axis=1), (0, 2, 1))

    lam_init = lambda_init_fn(layer_idx)
    lam = (jnp.exp(jnp.sum(lam_q1.astype(jnp.float32) * lam_k1.astype(jnp.float32)))
           - jnp.exp(jnp.sum(lam_q2.astype(jnp.float32) * lam_k2.astype(jnp.float32)))
           + lam_init)

    o_d, o_f = attention_branches(q_d, k_d, v_d, q_f, k_f, v_f, fcum, rel_table, lam)
    o_d = (rmsnorm(o_d, g_subln) * (1.0 - lam_init)).reshape(B, S, DIFF_WIDTH)
    o_f = o_f.reshape(B, S, FOX_WIDTH)

    gates = jax.nn.sigmoid(gl.astype(jnp.float32)).astype(x.dtype).reshape(B, S, N_BRANCHES, D_MODEL)
    merged = gates[:, :, 0] * (o_d @ w_pa) + gates[:, :, 1] * (o_f @ w_pb)
    x = x + merged @ w_o

    h2 = rmsnorm(x, g_mlp)
    x = x + jnp.square(jax.nn.relu(h2 @ w_1)) @ w_2
    return x


def setup_inputs(seed: int = 0) -> dict:
    key = jax.random.key(seed)
    ks = jax.random.split(key, 20)
    f32 = jnp.float32
    nrm = lambda k, shape, s: (jax.random.normal(k, shape, f32) * s)
    return {
        "x": nrm(ks[0], (BATCH, SEQ, D_MODEL), 1.0),
        "g_mix": 1.0 + nrm(ks[1], (DEPTH, D_MODEL), 0.05),
        "w_in": nrm(ks[2], (DEPTH, D_MODEL, IN_COLS), D_MODEL ** -0.5),
        "b_f": FORGET_BIAS_INIT + nrm(ks[3], (DEPTH, FOX_HEADS), 0.1),
        "lam_q1": nrm(ks[4], (DEPTH, HEAD_DIM), 0.1),
        "lam_k1": nrm(ks[5], (DEPTH, HEAD_DIM), 0.1),
        "lam_q2": nrm(ks[6], (DEPTH, HEAD_DIM), 0.1),
        "lam_k2": nrm(ks[7], (DEPTH, HEAD_DIM), 0.1),
        "g_subln": 1.0 + nrm(ks[8], (DEPTH, DIFF_V_DIM), 0.05),
        "w_pa": nrm(ks[9], (DEPTH, DIFF_WIDTH, D_MODEL), DIFF_WIDTH ** -0.5),
        "w_pb": nrm(ks[10], (DEPTH, FOX_WIDTH, D_MODEL), FOX_WIDTH ** -0.5),
        "w_o": nrm(ks[11], (DEPTH, D_MODEL, D_MODEL), D_MODEL ** -0.5),
        "g_mlp": 1.0 + nrm(ks[12], (DEPTH, D_MODEL), 0.05),
        "w_1": nrm(ks[13], (DEPTH, D_MODEL, D_FF), D_MODEL ** -0.5),
        "w_2": nrm(ks[14], (DEPTH, D_FF, D_MODEL), D_FF ** -0.5),
        "rel_table": nrm(ks[15], (REL_BUCKETS, DIFF_HEADS), 0.5),
        "g_final": 1.0 + nrm(ks[16], (D_MODEL,), 0.05),
    }


def reference(x, g_mix, w_in, b_f, lam_q1, lam_k1, lam_q2, lam_k2, g_subln, w_pa, w_pb, w_o,
              g_mlp, w_1, w_2, rel_table, g_final):
    for l in range(DEPTH):
        x = hybrid_layer(x, l, g_mix[l], w_in[l], b_f[l], lam_q1[l], lam_k1[l], lam_q2[l],
                         lam_k2[l], g_subln[l], w_pa[l], w_pb[l], w_o[l], g_mlp[l], w_1[l],
                         w_2[l], rel_table)
    return rmsnorm(x, g_final)
```

```python
import functools
import math

import numpy as np
import jax
import jax.numpy as jnp
from jax import lax
from jax.experimental import pallas as pl
from jax.experimental.pallas import tpu as pltpu

D_MODEL = 1024
CHUNK = 64
HEAD_DIM = 64
DIFF_HEADS = 4
FOX_HEADS = 8
BRANCH_WIDTH = 512
D_FF = 4 * D_MODEL
REL_BUCKETS = 32
REL_MAX_DIST = 128
EPS = 1e-6
LANES = 128
MASKED_BUCKET = REL_BUCKETS

TQ = 256
TK = 256
TM_PROJ = 512
VMEM_LIMIT = 56 * 1024 * 1024

F32 = jnp.float32
BF16 = jnp.bfloat16
NT_DIMS = (((1,), (1,)), ((), ()))
TN_DIMS = (((0,), (0,)), ((), ()))


def _lambda_init(layer_idx):
    return 0.8 - 0.6 * math.exp(-0.3 * layer_idx)


def _rms(xf, g):
    return xf * lax.rsqrt(jnp.mean(xf * xf, axis=-1, keepdims=True) + EPS) * g


def _in_proj_kernel(x_ref, g_ref, w_ref, wfl_ref, bf_ref,
                    qd_ref, kd_ref, vd_ref, qf_ref, kf_ref, vf_ref, fl_ref):
    h = _rms(x_ref[...], g_ref[...]).astype(BF16)
    outs = (qd_ref, kd_ref, vd_ref, qf_ref, kf_ref, vf_ref)
    for c, o_ref in enumerate(outs):
        w = w_ref[:, c * BRANCH_WIDTH:(c + 1) * BRANCH_WIDTH]
        o_ref[...] = jnp.dot(h, w, preferred_element_type=F32).astype(BF16)
    fl = jnp.dot(h, wfl_ref[...], preferred_element_type=F32)
    fl_ref[...] = fl[:, :FOX_HEADS] + bf_ref[...]


def _in_proj(x2, g_mix, w_qkv, w_fl, b_f):
    n = x2.shape[0]
    tm = TM_PROJ
    const = lambda i: (0, 0)
    row = lambda i: (i, 0)
    out_bf = jax.ShapeDtypeStruct((n, BRANCH_WIDTH), BF16)
    return pl.pallas_call(
        _in_proj_kernel,
        grid=(n // tm,),
        in_specs=[
            pl.BlockSpec((tm, D_MODEL), row),
            pl.BlockSpec((1, D_MODEL), const),
            pl.BlockSpec(w_qkv.shape, const),
            pl.BlockSpec(w_fl.shape, const),
            pl.BlockSpec((1, FOX_HEADS), const),
        ],
        out_specs=[pl.BlockSpec((tm, BRANCH_WIDTH), row)] * 6
        + [pl.BlockSpec((tm, FOX_HEADS), row)],
        out_shape=[out_bf] * 6 + [jax.ShapeDtypeStruct((n, FOX_HEADS), F32)],
        compiler_params=pltpu.CompilerParams(
            dimension_semantics=("parallel",), vmem_limit_bytes=VMEM_LIMIT),
        name="in_proj",
    )(x2, g_mix, w_qkv, w_fl, b_f)


def _forget_kernel(fl_ref, out_ref):
    x = fl_ref[...]
    logf = jnp.minimum(x, 0.0) - jnp.log1p(jnp.exp(-jnp.abs(x)))
    rows = x.shape[0]
    row = lax.broadcasted_iota(jnp.int32, x.shape, 0)
    acc = logf
    d = 1
    while d < rows:
        acc = acc + jnp.where(row >= d, pltpu.roll(acc, d, axis=0), 0.0)
        d *= 2
    out_ref[...] = -acc


def _neg_forget_cumsum(fl_t):
    return pl.pallas_call(
        _forget_kernel,
        out_shape=jax.ShapeDtypeStruct(fl_t.shape, F32),
        name="forget_cumsum",
    )(fl_t)


def _rel_bucket_np(rel):
    nb = REL_BUCKETS // 2
    ret = np.where(rel > 0, nb, 0)
    n = np.abs(rel)
    max_exact = nb // 2
    nf = np.maximum(n, 1).astype(np.float64)
    large = max_exact + (np.log(nf / max_exact) / math.log(REL_MAX_DIST / max_exact)
                         * (nb - max_exact)).astype(np.int32)
    large = np.minimum(large, nb - 1)
    return (ret + np.where(n < max_exact, n, large)).astype(np.int32)


def _bias_index_maps(seq):
    kk = np.arange(TK, dtype=np.int64)[:, None]
    qq = np.arange(TQ, dtype=np.int64)[None, :]
    diag = _rel_bucket_np(kk - qq)
    diag = np.where(kk // CHUNK <= qq // CHUNK, diag, MASKED_BUCKET).astype(np.int32)
    prev = _rel_bucket_np(kk - TK - qq)
    far = _rel_bucket_np(np.arange(-seq, -TK, dtype=np.int64))
    far_bucket = int(far[0])
    assert (far == far_bucket).all(), "keys two tiles back must share one bucket"
    return diag, prev, far_bucket


def _bias_kernel(far_bucket, tab_ref, idxd_ref, idxp_ref, bd_ref, bp_ref):
    for h in range(DIFF_HEADS):
        far = tab_ref[far_bucket, h]
        for idx_ref, out_ref in ((idxd_ref, bd_ref), (idxp_ref, bp_ref)):
            idx = idx_ref[...]
            acc = jnp.full(idx.shape, -jnp.inf, F32)
            for b in range(REL_BUCKETS):
                acc = jnp.where(idx == b, tab_ref[b, h] - far, acc)
            out_ref[h] = acc


def _bias_tiles(rel_table, seq):
    diag, prev, far_bucket = _bias_index_maps(seq)
    shape = jax.ShapeDtypeStruct((DIFF_HEADS, TK, TQ), F32)
    vmem = pl.BlockSpec(memory_space=pltpu.VMEM)
    return pl.pallas_call(
        functools.partial(_bias_kernel, far_bucket),
        in_specs=[pl.BlockSpec(memory_space=pltpu.SMEM), vmem, vmem],
        out_specs=[vmem, vmem],
        out_shape=[shape, shape],
        name="bias_tiles",
    )(rel_table, jnp.asarray(diag), jnp.asarray(prev))


def _softmax_tile(carry, k_t, v_t, q_m, extra):
    m, l, acc = carry
    s = lax.dot_general(k_t, q_m, NT_DIMS, preferred_element_type=F32)
    if extra is not None:
        s = s + extra
    m_new = jnp.maximum(m, jnp.max(s, axis=0, keepdims=True))
    alpha = jnp.exp(m - m_new)
    p = jnp.exp(s - m_new)
    l = alpha * l + jnp.sum(p, axis=0, keepdims=True)
    pv = lax.dot_general(v_t, p.astype(BF16), TN_DIMS, preferred_element_type=F32)
    return m_new, l, alpha * acc + pv


def _attn_kernel(lam_init, lam_ref, gsub_ref, bd_ref, bp_ref,
                 qd_ref, kd_ref, vd_ref, qf_ref, kf_ref, vf_ref, nf_ref,
                 od_ref, of_ref):
    i = pl.program_id(1)
    lane = lax.broadcasted_iota(jnp.int32, (TQ, LANES), 1)
    half = (lane < HEAD_DIM, lane >= HEAD_DIM)
    sub = lax.broadcasted_iota(jnp.int32, (LANES, TQ), 0)
    kk = lax.broadcasted_iota(jnp.int32, (TK, TQ), 0)
    qq = lax.broadcasted_iota(jnp.int32, (TK, TQ), 1)
    causal = jnp.where(kk <= qq, 0.0, -jnp.inf).astype(F32)
    prev_start = pl.multiple_of(jnp.maximum(i - 1, 0) * TK, TK)
    diag_start = pl.multiple_of(i * TK, TK)
    no_prev = jnp.where(i > 0, 0.0, -jnp.inf).astype(F32)

    lam_v = lam_ref[...]
    lam = (jnp.exp(jnp.sum(lam_v[0:1] * lam_v[1:2], axis=-1, keepdims=True))
           - jnp.exp(jnp.sum(lam_v[2:3] * lam_v[3:4], axis=-1, keepdims=True))
           + lam_init)

    def init():
        return (jnp.full((1, TQ), -jnp.inf, F32), jnp.zeros((1, TQ), F32),
                jnp.zeros((LANES, TQ), F32))

    def sweep(k_ref, v_ref, cols, q_m, diag_extra, prev_extra, far_extra, n_far):
        def tile(start):
            return k_ref[0, pl.ds(start, TK), cols], v_ref[0, pl.ds(start, TK), cols]

        carry = _softmax_tile(init(), *tile(diag_start), q_m, diag_extra(diag_start))
        if prev_extra is not None:
            carry = _softmax_tile(carry, *tile(prev_start), q_m, prev_extra(prev_start))

        def body(j, c):
            start = pl.multiple_of(j * TK, TK)
            return _softmax_tile(c, *tile(start), q_m, far_extra(start))

        m, l, acc = lax.fori_loop(0, n_far, body, carry)
        return acc / l

    for h in range(DIFF_HEADS):
        cols = slice(h * LANES, (h + 1) * LANES)
        q = qd_ref[0, :, cols]
        maps = []
        for e in range(2):
            q_m = jnp.where(half[e], q, jnp.zeros_like(q))
            maps.append(sweep(
                kd_ref, vd_ref, cols, q_m,
                diag_extra=lambda start: bd_ref[h],
                prev_extra=lambda start: bp_ref[h] + no_prev,
                far_extra=lambda start: None,
                n_far=jnp.maximum(i - 1, 0)))
        o = (maps[0] - lam * maps[1]).T
        o = _rms(o, gsub_ref[...]) * (1.0 - lam_init)
        od_ref[0, :, cols] = o.astype(BF16)

    for pair in range(FOX_HEADS // 2):
        cols = slice(pair * LANES, (pair + 1) * LANES)
        q = qf_ref[0, :, cols]
        heads = []
        for e in range(2):
            head = 2 * pair + e
            q_m = jnp.where(half[e], q, jnp.zeros_like(q))
            decay = lambda start: nf_ref[0, pl.ds(start, TK), head:head + 1]
            heads.append(sweep(
                kf_ref, vf_ref, cols, q_m,
                diag_extra=lambda start: decay(start) + causal,
                prev_extra=None,
                far_extra=decay,
                n_far=i))
        o = jnp.where(sub < HEAD_DIM, heads[0], heads[1]).T
        of_ref[0, :, cols] = o.astype(BF16)


def _attention(lam_vecs, g_subln, bias_diag, bias_prev, qd, kd, vd, qf, kf, vf, neg_fcum,
               lam_init):
    batch, seq, _ = qd.shape
    const2 = lambda b, i: (0, 0)
    const3 = lambda b, i: (0, 0, 0)
    qblk = pl.BlockSpec((1, TQ, BRANCH_WIDTH), lambda b, i: (b, i, 0))
    full = pl.BlockSpec((1, seq, BRANCH_WIDTH), lambda b, i: (b, 0, 0))
    out = jax.ShapeDtypeStruct((batch, seq, BRANCH_WIDTH), BF16)
    return pl.pallas_call(
        functools.partial(_attn_kernel, lam_init),
        grid=(batch, seq // TQ),
        in_specs=[
            pl.BlockSpec(lam_vecs.shape, const2),
            pl.BlockSpec(g_subln.shape, const2),
            pl.BlockSpec(bias_diag.shape, const3),
            pl.BlockSpec(bias_prev.shape, const3),
            qblk, full, full, qblk, full, full,
            pl.BlockSpec((1, seq, FOX_HEADS), lambda b, i: (b, 0, 0)),
        ],
        out_specs=[qblk, qblk],
        out_shape=[out, out],
        compiler_params=pltpu.CompilerParams(
            dimension_semantics=("parallel", "arbitrary"), vmem_limit_bytes=VMEM_LIMIT),
        name="attention",
    )(lam_vecs, g_subln, bias_diag, bias_prev, qd, kd, vd, qf, kf, vf, neg_fcum)


def _merge_kernel(x_ref, od_ref, of_ref, g_ref, wg_ref, wpa_ref, wpb_ref, wo_ref, y_ref):
    x = x_ref[...]
    h = _rms(x, g_ref[...]).astype(BF16)
    a = jnp.dot(od_ref[...], wpa_ref[...], preferred_element_type=F32)
    b = jnp.dot(of_ref[...], wpb_ref[...], preferred_element_type=F32)
    ga = jax.nn.sigmoid(jnp.dot(h, wg_ref[:, :D_MODEL], preferred_element_type=F32))
    merged = ga * a
    gb = jax.nn.sigmoid(jnp.dot(h, wg_ref[:, D_MODEL:], preferred_element_type=F32))
    merged = (merged + gb * b).astype(BF16)
    y_ref[...] = x + jnp.dot(merged, wo_ref[...], preferred_element_type=F32)


def _merge(x2, od, of, g_mix, w_gate, w_pa, w_pb, w_o):
    n = x2.shape[0]
    tm = TM_PROJ
    const = lambda i: (0, 0)
    row = lambda i: (i, 0)
    return pl.pallas_call(
        _merge_kernel,
        grid=(n // tm,),
        in_specs=[
            pl.BlockSpec((tm, D_MODEL), row),
            pl.BlockSpec((tm, BRANCH_WIDTH), row),
            pl.BlockSpec((tm, BRANCH_WIDTH), row),
            pl.BlockSpec((1, D_MODEL), const),
            pl.BlockSpec(w_gate.shape, const),
            pl.BlockSpec(w_pa.shape, const),
            pl.BlockSpec(w_pb.shape, const),
            pl.BlockSpec(w_o.shape, const),
        ],
        out_specs=pl.BlockSpec((tm, D_MODEL), row),
        out_shape=jax.ShapeDtypeStruct((n, D_MODEL), F32),
        compiler_params=pltpu.CompilerParams(
            dimension_semantics=("parallel",), vmem_limit_bytes=VMEM_LIMIT),
        name="merge",
    )(x2, od, of, g_mix, w_gate, w_pa, w_pb, w_o)


FF_CHUNK = 1024


def _mlp_kernel(final_norm, x_ref, g_ref, w1_ref, w2_ref, gf_ref, y_ref):
    x = x_ref[...]
    h = _rms(x, g_ref[...]).astype(BF16)
    y = x
    for c in range(D_FF // FF_CHUNK):
        cols = slice(c * FF_CHUNK, (c + 1) * FF_CHUNK)
        u = jnp.maximum(jnp.dot(h, w1_ref[:, cols], preferred_element_type=F32), 0.0)
        y = y + jnp.dot((u * u).astype(BF16), w2_ref[cols, :], preferred_element_type=F32)
    y_ref[...] = _rms(y, gf_ref[...]) if final_norm else y


def _mlp(x2, g_mlp, w_1, w_2, g_final, final_norm):
    n = x2.shape[0]
    tm = TM_PROJ
    const = lambda i: (0, 0)
    row = lambda i: (i, 0)
    single = pl.Buffered(1)
    return pl.pallas_call(
        functools.partial(_mlp_kernel, final_norm),
        grid=(n // tm,),
        in_specs=[
            pl.BlockSpec((tm, D_MODEL), row),
            pl.BlockSpec((1, D_MODEL), const),
            pl.BlockSpec(w_1.shape, const, pipeline_mode=single),
            pl.BlockSpec(w_2.shape, const, pipeline_mode=single),
            pl.BlockSpec((1, D_MODEL), const),
        ],
        out_specs=pl.BlockSpec((tm, D_MODEL), row),
        out_shape=jax.ShapeDtypeStruct((n, D_MODEL), F32),
        compiler_params=pltpu.CompilerParams(
            dimension_semantics=("parallel",), vmem_limit_bytes=VMEM_LIMIT),
        name="mlp",
    )(x2, g_mlp, w_1, w_2, g_final)


def _layer(x, layer_idx, g_mix, w_in, b_f, lam_q1, lam_k1, lam_q2, lam_k2, g_subln,
           w_pa, w_pb, w_o, g_mlp, w_1, w_2, bias_diag, bias_prev, g_final, final_norm):
    batch, seq, d = x.shape
    n = batch * seq
    x2 = x.reshape(n, d)
    qkv_cols = 6 * BRANCH_WIDTH
    scale = HEAD_DIM ** -0.5

    col_scale = np.ones((qkv_cols,), np.float32)
    col_scale[0:BRANCH_WIDTH] = scale
    col_scale[3 * BRANCH_WIDTH:4 * BRANCH_WIDTH] = scale
    w_qkv = (w_in[:, :qkv_cols] * col_scale).astype(BF16)
    w_fl = jnp.pad(w_in[:, qkv_cols:qkv_cols + FOX_HEADS],
                   ((0, 0), (0, LANES - FOX_HEADS))).astype(BF16)
    w_gate = w_in[:, qkv_cols + FOX_HEADS:].astype(BF16)
    g_mix2 = g_mix.reshape(1, d)

    qd, kd, vd, qf, kf, vf, fl = _in_proj(x2, g_mix2, w_qkv, w_fl, b_f.reshape(1, FOX_HEADS))

    fl_t = fl.reshape(batch, seq, FOX_HEADS).transpose(1, 0, 2).reshape(seq, batch * FOX_HEADS)
    neg_fcum = _neg_forget_cumsum(fl_t)
    neg_fcum = neg_fcum.reshape(seq, batch, FOX_HEADS).transpose(1, 0, 2)

    lam_vecs = jnp.stack([lam_q1, lam_k1, lam_q2, lam_k2]).astype(F32)
    shape3 = (batch, seq, BRANCH_WIDTH)
    od, of = _attention(lam_vecs, g_subln.reshape(1, LANES), bias_diag, bias_prev,
                        qd.reshape(shape3), kd.reshape(shape3), vd.reshape(shape3),
                        qf.reshape(shape3), kf.reshape(shape3), vf.reshape(shape3),
                        neg_fcum, _lambda_init(layer_idx))

    x1 = _merge(x2, od.reshape(n, BRANCH_WIDTH), of.reshape(n, BRANCH_WIDTH), g_mix2,
                w_gate, w_pa.astype(BF16), w_pb.astype(BF16), w_o.astype(BF16))
    y = _mlp(x1, g_mlp.reshape(1, d), w_1.astype(BF16), w_2.astype(BF16), g_final, final_norm)
    return y.reshape(batch, seq, d)


def kernel(x, g_mix, w_in, b_f, lam_q1, lam_k1, lam_q2, lam_k2, g_subln, w_pa, w_pb, w_o,
           g_mlp, w_1, w_2, rel_table, g_final):
    depth = g_mix.shape[0]
    bias_diag, bias_prev = _bias_tiles(rel_table, x.shape[1])
    for l in range(depth):
        x = _layer(x, l, g_mix[l], w_in[l], b_f[l], lam_q1[l], lam_k1[l], lam_q2[l], lam_k2[l],
                   g_subln[l], w_pa[l], w_pb[l], w_o[l], g_mlp[l], w_1[l], w_2[l],
                   bias_diag, bias_prev, g_final.reshape(1, -1), l == depth - 1)
    return x
```

```python
import functools
import math

import numpy as np
import jax
import jax.numpy as jnp
from jax import lax
from jax.experimental import pallas as pl
from jax.experimental.pallas import tpu as pltpu

D_MODEL = 1024
CHUNK = 64
HEAD_DIM = 64
DIFF_HEADS = 4
FOX_HEADS = 8
BRANCH_WIDTH = 512
D_FF = 4 * D_MODEL
REL_BUCKETS = 32
REL_MAX_DIST = 128
EPS = 1e-6
LANES = 128
MASKED_BUCKET = REL_BUCKETS

TQ = 256
TK = 256
TM_PROJ = 512
VMEM_LIMIT = 56 * 1024 * 1024

F32 = jnp.float32
BF16 = jnp.bfloat16
NT_DIMS = (((1,), (1,)), ((), ()))
TN_DIMS = (((0,), (0,)), ((), ()))


def _lambda_init(layer_idx):
    return 0.8 - 0.6 * math.exp(-0.3 * layer_idx)


def _rms(xf, g):
    return xf * lax.rsqrt(jnp.mean(xf * xf, axis=-1, keepdims=True) + EPS) * g


DECAY_PARTS = 3


def _in_proj_kernel(tiles_per_seq, x_ref, g_ref, w_ref, wfl_ref, bf_ref,
                    qd_ref, kd_ref, vd_ref, qf_ref, kf_ref, vf_ref, dec_ref, carry_ref):
    h = _rms(x_ref[...], g_ref[...]).astype(BF16)
    outs = (qd_ref, kd_ref, vd_ref, qf_ref, kf_ref, vf_ref)
    for c, o_ref in enumerate(outs):
        w = w_ref[:, c * BRANCH_WIDTH:(c + 1) * BRANCH_WIDTH]
        o_ref[...] = jnp.dot(h, w, preferred_element_type=F32).astype(BF16)

    @pl.when(pl.program_id(0) % tiles_per_seq == 0)
    def _():
        carry_ref[...] = jnp.zeros_like(carry_ref)

    z = jnp.dot(h, wfl_ref[...], preferred_element_type=F32) + bf_ref[...]
    acc = jnp.minimum(z, 0.0) - jnp.log1p(jnp.exp(-jnp.abs(z)))
    rows = acc.shape[0]
    row = lax.broadcasted_iota(jnp.int32, acc.shape, 0)
    d = 1
    while d < rows:
        acc = acc + jnp.where(row >= d, pltpu.roll(acc, d, axis=0), 0.0)
        d *= 2
    acc = acc + carry_ref[...]
    carry_ref[...] = acc[rows - 1:rows, :]
    neg = -acc
    hi = neg.astype(BF16).astype(F32)
    mid = (neg - hi).astype(BF16).astype(F32)
    lo = neg - hi - mid
    lane = lax.broadcasted_iota(jnp.int32, acc.shape, 1)
    piece = jnp.where(lane < FOX_HEADS, hi, jnp.where(lane < 2 * FOX_HEADS, mid, lo))
    dec_ref[...] = jnp.where(lane < DECAY_PARTS * FOX_HEADS, piece, 0.0).astype(BF16)


def _in_proj(x2, g_mix, w_qkv, w_fl, b_f, seq):
    n = x2.shape[0]
    tm = TM_PROJ
    const = lambda i: (0, 0)
    row = lambda i: (i, 0)
    out_bf = jax.ShapeDtypeStruct((n, BRANCH_WIDTH), BF16)
    return pl.pallas_call(
        functools.partial(_in_proj_kernel, seq // tm),
        grid=(n // tm,),
        in_specs=[
            pl.BlockSpec((tm, D_MODEL), row),
            pl.BlockSpec((1, D_MODEL), const),
            pl.BlockSpec(w_qkv.shape, const),
            pl.BlockSpec(w_fl.shape, const),
            pl.BlockSpec((1, LANES), const),
        ],
        out_specs=[pl.BlockSpec((tm, BRANCH_WIDTH), row)] * 6
        + [pl.BlockSpec((tm, LANES), row)],
        out_shape=[out_bf] * 6 + [jax.ShapeDtypeStruct((n, LANES), BF16)],
        scratch_shapes=[pltpu.VMEM((1, LANES), F32)],
        compiler_params=pltpu.CompilerParams(
            dimension_semantics=("arbitrary",), vmem_limit_bytes=VMEM_LIMIT),
        name="in_proj",
    )(x2, g_mix, w_qkv, w_fl, b_f)


def _rel_bucket_np(rel):
    nb = REL_BUCKETS // 2
    ret = np.where(rel > 0, nb, 0)
    n = np.abs(rel)
    max_exact = nb // 2
    nf = np.maximum(n, 1).astype(np.float64)
    large = max_exact + (np.log(nf / max_exact) / math.log(REL_MAX_DIST / max_exact)
                         * (nb - max_exact)).astype(np.int32)
    large = np.minimum(large, nb - 1)
    return (ret + np.where(n < max_exact, n, large)).astype(np.int32)


def _bias_index_maps(seq):
    kk = np.arange(TK, dtype=np.int64)[:, None]
    qq = np.arange(TQ, dtype=np.int64)[None, :]
    diag = _rel_bucket_np(kk - qq)
    diag = np.where(kk // CHUNK <= qq // CHUNK, diag, MASKED_BUCKET).astype(np.int32)
    prev = _rel_bucket_np(kk - TK - qq)
    far = _rel_bucket_np(np.arange(-seq, -TK, dtype=np.int64))
    far_bucket = int(far[0])
    assert (far == far_bucket).all(), "keys two tiles back must share one bucket"
    return diag, prev, far_bucket


def _bias_kernel(far_bucket, tab_ref, idxd_ref, idxp_ref, bd_ref, bp_ref):
    for h in range(DIFF_HEADS):
        far = tab_ref[far_bucket, h]
        for idx_ref, out_ref in ((idxd_ref, bd_ref), (idxp_ref, bp_ref)):
            idx = idx_ref[...]
            acc = jnp.full(idx.shape, -jnp.inf, F32)
            for b in range(REL_BUCKETS):
                acc = jnp.where(idx == b, tab_ref[b, h] - far, acc)
            out_ref[h] = acc


def _bias_tiles(rel_table, seq):
    diag, prev, far_bucket = _bias_index_maps(seq)
    shape = jax.ShapeDtypeStruct((DIFF_HEADS, TK, TQ), F32)
    vmem = pl.BlockSpec(memory_space=pltpu.VMEM)
    return pl.pallas_call(
        functools.partial(_bias_kernel, far_bucket),
        in_specs=[pl.BlockSpec(memory_space=pltpu.SMEM), vmem, vmem],
        out_specs=[vmem, vmem],
        out_shape=[shape, shape],
        name="bias_tiles",
    )(rel_table, jnp.asarray(diag), jnp.asarray(prev))


CHAINS = 8
QK_AHEAD = 3


def _attn_kernel(lam_init, lam_ref, gsub_ref, bd_ref, bp_ref,
                 qd_ref, kd_ref, vd_ref, qf_ref, kf_ref, vf_ref, dec_ref,
                 od_ref, of_ref, qm_ref, m_ref, l_ref, acc_ref):
    i = pl.program_id(1)
    lane = lax.broadcasted_iota(jnp.int32, (TQ, LANES), 1)
    half = (lane < HEAD_DIM, lane >= HEAD_DIM)
    sub = lax.broadcasted_iota(jnp.int32, (LANES, TQ), 0)
    kk = lax.broadcasted_iota(jnp.int32, (TK, TQ), 0)
    qq = lax.broadcasted_iota(jnp.int32, (TK, TQ), 1)
    causal = jnp.where(kk <= qq, 0.0, -jnp.inf).astype(F32)
    prev_start = pl.multiple_of(jnp.maximum(i - 1, 0) * TK, TK)
    diag_start = pl.multiple_of(i * TK, TK)
    no_prev = jnp.where(i > 0, 0.0, -jnp.inf).astype(F32)

    lam_v = lam_ref[...]
    lam = (jnp.exp(jnp.sum(lam_v[0:1] * lam_v[1:2], axis=-1, keepdims=True))
           - jnp.exp(jnp.sum(lam_v[2:3] * lam_v[3:4], axis=-1, keepdims=True))
           + lam_init)

    def load_queries(q_ref, with_decay):
        for blk in range(CHAINS // 2):
            q = q_ref[0, :, blk * LANES:(blk + 1) * LANES]
            for e in range(2):
                c = 2 * blk + e
                qm_ref[c, :, :LANES] = jnp.where(half[e], q, jnp.zeros_like(q))
                if with_decay:
                    pick = (lane < DECAY_PARTS * FOX_HEADS) & (lane % FOX_HEADS == c)
                    qm_ref[c, :, LANES:] = jnp.where(pick, 1.0, 0.0).astype(BF16)

    def step(k_ref, v_ref, start, extra, first, with_decay=False):
        def cols(c):
            return slice((c // 2) * LANES, (c // 2 + 1) * LANES)

        def scores(c):
            k_t = k_ref[0, pl.ds(start, TK), cols(c)]
            if not with_decay:
                return lax.dot_general(k_t, qm_ref[c, :, :LANES], NT_DIMS,
                                       preferred_element_type=F32)
            k_t = jnp.concatenate([k_t, dec_ref[0, pl.ds(start, TK), :]], axis=1)
            return lax.dot_general(k_t, qm_ref[c], NT_DIMS, preferred_element_type=F32)

        pending = {c: scores(c) for c in range(QK_AHEAD)}
        for c in range(CHAINS):
            s = pending.pop(c)
            add = extra(c)
            if add is not None:
                s = s + add
            m_new = jnp.max(s, axis=0, keepdims=True)
            if not first:
                m_old = m_ref[c]
                m_new = jnp.maximum(m_old, m_new)
                alpha = jnp.exp(m_old - m_new)
            p = jnp.exp(s - m_new)
            l_new = jnp.sum(p, axis=0, keepdims=True)
            v_t = v_ref[0, pl.ds(start, TK), cols(c)]
            pv = lax.dot_general(v_t, p.astype(BF16), TN_DIMS, preferred_element_type=F32)
            if c + QK_AHEAD < CHAINS:
                pending[c + QK_AHEAD] = scores(c + QK_AHEAD)
            if not first:
                l_new = alpha * l_ref[c] + l_new
                pv = alpha * acc_ref[c] + pv
            m_ref[c] = m_new
            l_ref[c] = l_new
            acc_ref[c] = pv

    def far_loop(k_ref, v_ref, n_far, with_decay):
        def body(j, carry):
            start = pl.multiple_of(j * TK, TK)
            step(k_ref, v_ref, start, lambda c: None, first=False, with_decay=with_decay)
            return carry
        lax.fori_loop(0, n_far, body, 0)

    load_queries(qd_ref, with_decay=False)
    step(kd_ref, vd_ref, diag_start, lambda c: bd_ref[c // 2], first=True)
    step(kd_ref, vd_ref, prev_start, lambda c: bp_ref[c // 2] + no_prev, first=False)
    far_loop(kd_ref, vd_ref, jnp.maximum(i - 1, 0), with_decay=False)
    for h in range(DIFF_HEADS):
        o = (acc_ref[2 * h] / l_ref[2 * h] - lam * (acc_ref[2 * h + 1] / l_ref[2 * h + 1])).T
        o = _rms(o, gsub_ref[...]) * (1.0 - lam_init)
        od_ref[0, :, h * LANES:(h + 1) * LANES] = o.astype(BF16)

    load_queries(qf_ref, with_decay=True)
    step(kf_ref, vf_ref, diag_start, lambda c: causal, first=True, with_decay=True)
    far_loop(kf_ref, vf_ref, i, with_decay=True)
    for pair in range(FOX_HEADS // 2):
        o = jnp.where(sub < HEAD_DIM, acc_ref[2 * pair] / l_ref[2 * pair],
                      acc_ref[2 * pair + 1] / l_ref[2 * pair + 1]).T
        of_ref[0, :, pair * LANES:(pair + 1) * LANES] = o.astype(BF16)


def _attention(lam_vecs, g_subln, bias_diag, bias_prev, qd, kd, vd, qf, kf, vf, decay,
               lam_init):
    batch, seq, _ = qd.shape
    const2 = lambda b, i: (0, 0)
    const3 = lambda b, i: (0, 0, 0)
    qblk = pl.BlockSpec((1, TQ, BRANCH_WIDTH), lambda b, i: (b, i, 0))
    full = pl.BlockSpec((1, seq, BRANCH_WIDTH), lambda b, i: (b, 0, 0))
    out = jax.ShapeDtypeStruct((batch, seq, BRANCH_WIDTH), BF16)
    return pl.pallas_call(
        functools.partial(_attn_kernel, lam_init),
        grid=(batch, seq // TQ),
        in_specs=[
            pl.BlockSpec(lam_vecs.shape, const2),
            pl.BlockSpec(g_subln.shape, const2),
            pl.BlockSpec(bias_diag.shape, const3),
            pl.BlockSpec(bias_prev.shape, const3),
            qblk, full, full, qblk, full, full,
            pl.BlockSpec((1, seq, LANES), lambda b, i: (b, 0, 0)),
        ],
        out_specs=[qblk, qblk],
        out_shape=[out, out],
        scratch_shapes=[
            pltpu.VMEM((CHAINS, TQ, 2 * LANES), BF16),
            pltpu.VMEM((CHAINS, 1, TQ), F32),
            pltpu.VMEM((CHAINS, 1, TQ), F32),
            pltpu.VMEM((CHAINS, LANES, TQ), F32),
        ],
        compiler_params=pltpu.CompilerParams(
            dimension_semantics=("parallel", "arbitrary"), vmem_limit_bytes=VMEM_LIMIT),
        name="attention",
    )(lam_vecs, g_subln, bias_diag, bias_prev, qd, kd, vd, qf, kf, vf, decay)


def _merge_kernel(x_ref, od_ref, of_ref, g_ref, wg_ref, wpa_ref, wpb_ref, wo_ref, y_ref):
    x = x_ref[...]
    h = _rms(x, g_ref[...]).astype(BF16)
    a = jnp.dot(od_ref[...], wpa_ref[...], preferred_element_type=F32)
    b = jnp.dot(of_ref[...], wpb_ref[...], preferred_element_type=F32)
    ga = jax.nn.sigmoid(jnp.dot(h, wg_ref[:, :D_MODEL], preferred_element_type=F32))
    merged = ga * a
    gb = jax.nn.sigmoid(jnp.dot(h, wg_ref[:, D_MODEL:], preferred_element_type=F32))
    merged = (merged + gb * b).astype(BF16)
    y_ref[...] = x + jnp.dot(merged, wo_ref[...], preferred_element_type=F32)


def _merge(x2, od, of, g_mix, w_gate, w_pa, w_pb, w_o):
    n = x2.shape[0]
    tm = TM_PROJ
    const = lambda i: (0, 0)
    row = lambda i: (i, 0)
    return pl.pallas_call(
        _merge_kernel,
        grid=(n // tm,),
        in_specs=[
            pl.BlockSpec((tm, D_MODEL), row),
            pl.BlockSpec((tm, BRANCH_WIDTH), row),
            pl.BlockSpec((tm, BRANCH_WIDTH), row),
            pl.BlockSpec((1, D_MODEL), const),
            pl.BlockSpec(w_gate.shape, const),
            pl.BlockSpec(w_pa.shape, const),
            pl.BlockSpec(w_pb.shape, const),
            pl.BlockSpec(w_o.shape, const),
        ],
        out_specs=pl.BlockSpec((tm, D_MODEL), row),
        out_shape=jax.ShapeDtypeStruct((n, D_MODEL), F32),
        compiler_params=pltpu.CompilerParams(
            dimension_semantics=("parallel",), vmem_limit_bytes=VMEM_LIMIT),
        name="merge",
    )(x2, od, of, g_mix, w_gate, w_pa, w_pb, w_o)


FF_CHUNK = 1024


def _mlp_kernel(final_norm, x_ref, g_ref, w1_ref, w2_ref, gf_ref, y_ref):
    x = x_ref[...]
    h = _rms(x, g_ref[...]).astype(BF16)
    y = x
    for c in range(D_FF // FF_CHUNK):
        cols = slice(c * FF_CHUNK, (c + 1) * FF_CHUNK)
        u = jnp.maximum(jnp.dot(h, w1_ref[:, cols], preferred_element_type=F32), 0.0)
        y = y + jnp.dot((u * u).astype(BF16), w2_ref[cols, :], preferred_element_type=F32)
    y_ref[...] = _rms(y, gf_ref[...]) if final_norm else y


def _mlp(x2, g_mlp, w_1, w_2, g_final, final_norm):
    n = x2.shape[0]
    tm = TM_PROJ
    const = lambda i: (0, 0)
    row = lambda i: (i, 0)
    single = pl.Buffered(1)
    return pl.pallas_call(
        functools.partial(_mlp_kernel, final_norm),
        grid=(n // tm,),
        in_specs=[
            pl.BlockSpec((tm, D_MODEL), row),
            pl.BlockSpec((1, D_MODEL), const),
            pl.BlockSpec(w_1.shape, const, pipeline_mode=single),
            pl.BlockSpec(w_2.shape, const, pipeline_mode=single),
            pl.BlockSpec((1, D_MODEL), const),
        ],
        out_specs=pl.BlockSpec((tm, D_MODEL), row),
        out_shape=jax.ShapeDtypeStruct((n, D_MODEL), F32),
        compiler_params=pltpu.CompilerParams(
            dimension_semantics=("parallel",), vmem_limit_bytes=VMEM_LIMIT),
        name="mlp",
    )(x2, g_mlp, w_1, w_2, g_final)


def _layer(x, layer_idx, g_mix, w_in, b_f, lam_q1, lam_k1, lam_q2, lam_k2, g_subln,
           w_pa, w_pb, w_o, g_mlp, w_1, w_2, bias_diag, bias_prev, g_final, final_norm):
    batch, seq, d = x.shape
    n = batch * seq
    x2 = x.reshape(n, d)
    qkv_cols = 6 * BRANCH_WIDTH
    scale = HEAD_DIM ** -0.5

    col_scale = np.ones((qkv_cols,), np.float32)
    col_scale[0:BRANCH_WIDTH] = scale
    col_scale[3 * BRANCH_WIDTH:4 * BRANCH_WIDTH] = scale
    w_qkv = (w_in[:, :qkv_cols] * col_scale).astype(BF16)
    pad = LANES - DECAY_PARTS * FOX_HEADS
    w_fl = jnp.pad(jnp.tile(w_in[:, qkv_cols:qkv_cols + FOX_HEADS], (1, DECAY_PARTS)),
                   ((0, 0), (0, pad))).astype(BF16)
    b_fl = jnp.pad(jnp.tile(b_f, DECAY_PARTS), (0, pad)).reshape(1, LANES)
    w_gate = w_in[:, qkv_cols + FOX_HEADS:].astype(BF16)
    g_mix2 = g_mix.reshape(1, d)

    qd, kd, vd, qf, kf, vf, decay = _in_proj(x2, g_mix2, w_qkv, w_fl, b_fl, seq)

    lam_vecs = jnp.stack([lam_q1, lam_k1, lam_q2, lam_k2]).astype(F32)
    shape3 = (batch, seq, BRANCH_WIDTH)
    od, of = _attention(lam_vecs, g_subln.reshape(1, LANES), bias_diag, bias_prev,
                        qd.reshape(shape3), kd.reshape(shape3), vd.reshape(shape3),
                        qf.reshape(shape3), kf.reshape(shape3), vf.reshape(shape3),
                        decay.reshape(batch, seq, LANES), _lambda_init(layer_idx))

    x1 = _merge(x2, od.reshape(n, BRANCH_WIDTH), of.reshape(n, BRANCH_WIDTH), g_mix2,
                w_gate, w_pa.astype(BF16), w_pb.astype(BF16), w_o.astype(BF16))
    y = _mlp(x1, g_mlp.reshape(1, d), w_1.astype(BF16), w_2.astype(BF16), g_final, final_norm)
    return y.reshape(batch, seq, d)


def kernel(x, g_mix, w_in, b_f, lam_q1, lam_k1, lam_q2, lam_k2, g_subln, w_pa, w_pb, w_o,
           g_mlp, w_1, w_2, rel_table, g_final):
    depth = g_mix.shape[0]
    bias_diag, bias_prev = _bias_tiles(rel_table, x.shape[1])
    for l in range(depth):
        x = _layer(x, l, g_mix[l], w_in[l], b_f[l], lam_q1[l], lam_k1[l], lam_q2[l], lam_k2[l],
                   g_subln[l], w_pa[l], w_pb[l], w_o[l], g_mlp[l], w_1[l], w_2[l],
                   bias_diag, bias_prev, g_final.reshape(1, -1), l == depth - 1)
    return x
```

```python
import functools
import math

import numpy as np
import jax
import jax.numpy as jnp
from jax import lax
from jax.experimental import pallas as pl
from jax.experimental.pallas import tpu as pltpu

D_MODEL = 1024
CHUNK = 64
HEAD_DIM = 64
DIFF_HEADS = 4
FOX_HEADS = 8
BRANCH_WIDTH = 512
D_FF = 4 * D_MODEL
REL_BUCKETS = 32
REL_MAX_DIST = 128
EPS = 1e-6
LANES = 128
MASKED_BUCKET = REL_BUCKETS

TQ = 256
TK = 256
TM_PROJ = 512
VMEM_LIMIT = 56 * 1024 * 1024

F32 = jnp.float32
BF16 = jnp.bfloat16
NT_DIMS = (((1,), (1,)), ((), ()))
TN_DIMS = (((0,), (0,)), ((), ()))


def _lambda_init(layer_idx):
    return 0.8 - 0.6 * math.exp(-0.3 * layer_idx)


def _rms(xf, g):
    return xf * lax.rsqrt(jnp.mean(xf * xf, axis=-1, keepdims=True) + EPS) * g


DECAY_PARTS = 3


def _in_proj_kernel(tiles_per_seq, x_ref, g_ref, w_ref, wfl_ref, bf_ref,
                    qd_ref, kd_ref, vd_ref, qf_ref, kf_ref, vf_ref, dec_ref, carry_ref):
    h = _rms(x_ref[...], g_ref[...]).astype(BF16)

    @pl.when(pl.program_id(0) % tiles_per_seq == 0)
    def _():
        carry_ref[...] = jnp.zeros_like(carry_ref)

    z = jnp.dot(h, wfl_ref[...], preferred_element_type=F32) + bf_ref[...]
    acc = jnp.minimum(z, 0.0) - jnp.log1p(jnp.exp(-jnp.abs(z)))
    rows = acc.shape[0]
    row = lax.broadcasted_iota(jnp.int32, acc.shape, 0)
    d = 1
    while d < rows:
        acc = acc + jnp.where(row >= d, pltpu.roll(acc, d, axis=0), 0.0)
        d *= 2
    acc = acc + carry_ref[...]
    carry_ref[...] = acc[rows - 1:rows, :]
    neg = -acc
    hi = neg.astype(BF16).astype(F32)
    mid = (neg - hi).astype(BF16).astype(F32)
    lo = neg - hi - mid
    lane = lax.broadcasted_iota(jnp.int32, acc.shape, 1)
    piece = jnp.where(lane < FOX_HEADS, hi, jnp.where(lane < 2 * FOX_HEADS, mid, lo))
    dec_ref[...] = jnp.where(lane < DECAY_PARTS * FOX_HEADS, piece, 0.0).astype(BF16)

    outs = (qd_ref, kd_ref, vd_ref, qf_ref, kf_ref, vf_ref)
    for c, o_ref in enumerate(outs):
        w = w_ref[:, c * BRANCH_WIDTH:(c + 1) * BRANCH_WIDTH]
        o_ref[...] = jnp.dot(h, w, preferred_element_type=F32).astype(BF16)


def _in_proj(x2, g_mix, w_qkv, w_fl, b_f, seq):
    n = x2.shape[0]
    tm = TM_PROJ
    const = lambda i: (0, 0)
    row = lambda i: (i, 0)
    out_bf = jax.ShapeDtypeStruct((n, BRANCH_WIDTH), BF16)
    return pl.pallas_call(
        functools.partial(_in_proj_kernel, seq // tm),
        grid=(n // tm,),
        in_specs=[
            pl.BlockSpec((tm, D_MODEL), row),
            pl.BlockSpec((1, D_MODEL), const),
            pl.BlockSpec(w_qkv.shape, const),
            pl.BlockSpec(w_fl.shape, const),
            pl.BlockSpec((1, LANES), const),
        ],
        out_specs=[pl.BlockSpec((tm, BRANCH_WIDTH), row)] * 6
        + [pl.BlockSpec((tm, LANES), row)],
        out_shape=[out_bf] * 6 + [jax.ShapeDtypeStruct((n, LANES), BF16)],
        scratch_shapes=[pltpu.VMEM((1, LANES), F32)],
        compiler_params=pltpu.CompilerParams(
            dimension_semantics=("arbitrary",), vmem_limit_bytes=VMEM_LIMIT),
        name="in_proj",
    )(x2, g_mix, w_qkv, w_fl, b_f)


def _rel_bucket_np(rel):
    nb = REL_BUCKETS // 2
    ret = np.where(rel > 0, nb, 0)
    n = np.abs(rel)
    max_exact = nb // 2
    nf = np.maximum(n, 1).astype(np.float64)
    large = max_exact + (np.log(nf / max_exact) / math.log(REL_MAX_DIST / max_exact)
                         * (nb - max_exact)).astype(np.int32)
    large = np.minimum(large, nb - 1)
    return (ret + np.where(n < max_exact, n, large)).astype(np.int32)


def _bias_index_maps(seq):
    kk = np.arange(TK, dtype=np.int64)[:, None]
    qq = np.arange(TQ, dtype=np.int64)[None, :]
    diag = _rel_bucket_np(kk - qq)
    diag = np.where(kk // CHUNK <= qq // CHUNK, diag, MASKED_BUCKET).astype(np.int32)
    prev = _rel_bucket_np(kk - TK - qq)
    far = _rel_bucket_np(np.arange(-seq, -TK, dtype=np.int64))
    far_bucket = int(far[0])
    assert (far == far_bucket).all(), "keys two tiles back must share one bucket"
    return diag, prev, far_bucket


def _bias_kernel(far_bucket, tab_ref, idx_ref, out_ref):
    for h in range(DIFF_HEADS):
        far = tab_ref[far_bucket, h]
        for t in range(idx_ref.shape[0]):
            idx = idx_ref[t]
            acc = jnp.full(idx.shape, -jnp.inf, F32)
            for b in range(REL_BUCKETS):
                acc = jnp.where(idx == b, tab_ref[b, h] - far, acc)
            out_ref[t, h] = acc


def _bias_tiles(rel_table, seq):
    diag, prev, far_bucket = _bias_index_maps(seq)
    idx = np.stack([diag, prev])
    vmem = pl.BlockSpec(memory_space=pltpu.VMEM)
    return pl.pallas_call(
        functools.partial(_bias_kernel, far_bucket),
        in_specs=[pl.BlockSpec(memory_space=pltpu.SMEM), vmem],
        out_specs=vmem,
        out_shape=jax.ShapeDtypeStruct((idx.shape[0], DIFF_HEADS, TK, TQ), F32),
        name="bias_tiles",
    )(rel_table, jnp.asarray(idx))


CHAINS = 8
QK_AHEAD = 3


def _attn_kernel(lam_init, lam_ref, gsub_ref, bnear_ref, causal_ref,
                 qd_ref, kd_ref, vd_ref, qf_ref, kf_ref, vf_ref, dec_ref,
                 od_ref, of_ref, qm_ref, m_ref, l_ref, acc_ref):
    i = pl.program_id(1)
    lane = lax.broadcasted_iota(jnp.int32, (TQ, LANES), 1)
    half = (lane < HEAD_DIM, lane >= HEAD_DIM)
    sub = lax.broadcasted_iota(jnp.int32, (LANES, TQ), 0)
    lam_v = lam_ref[...]
    lam = (jnp.exp(jnp.sum(lam_v[0:1] * lam_v[1:2], axis=-1, keepdims=True))
           - jnp.exp(jnp.sum(lam_v[2:3] * lam_v[3:4], axis=-1, keepdims=True))
           + lam_init)

    def load_queries(q_ref, with_decay):
        for blk in range(CHAINS // 2):
            q = q_ref[0, :, blk * LANES:(blk + 1) * LANES]
            for e in range(2):
                c = 2 * blk + e
                qm_ref[c, :, :LANES] = jnp.where(half[e], q, jnp.zeros_like(q))
                if with_decay:
                    pick = (lane < DECAY_PARTS * FOX_HEADS) & (lane % FOX_HEADS == c)
                    qm_ref[c, :, LANES:] = jnp.where(pick, 1.0, 0.0).astype(BF16)

    def step(k_ref, v_ref, start, extra, with_decay):
        def cols(c):
            return slice((c // 2) * LANES, (c // 2 + 1) * LANES)

        def scores(c):
            k_t = k_ref[0, pl.ds(start, TK), cols(c)]
            q_t = qm_ref[c, :, :LANES]
            if with_decay:
                k_t = jnp.concatenate([k_t, dec_ref[0, pl.ds(start, TK), :]], axis=1)
                q_t = qm_ref[c]
            s = lax.dot_general(k_t, q_t, NT_DIMS, preferred_element_type=F32)
            add = extra(c)
            return s if add is None else s + add

        pending = {c: scores(c) for c in range(QK_AHEAD)}
        for c in range(CHAINS):
            s = pending.pop(c)
            m_old = m_ref[c]
            m_new = jnp.maximum(m_old, jnp.max(s, axis=0, keepdims=True))
            alpha = jnp.exp(m_old - m_new)
            p = jnp.exp(s - m_new)
            l_new = alpha * l_ref[c] + jnp.sum(p, axis=0, keepdims=True)
            v_t = v_ref[0, pl.ds(start, TK), cols(c)]
            pv = lax.dot_general(v_t, p.astype(BF16), TN_DIMS, preferred_element_type=F32)
            if c + QK_AHEAD < CHAINS:
                pending[c + QK_AHEAD] = scores(c + QK_AHEAD)
            m_ref[c] = m_new
            l_ref[c] = l_new
            acc_ref[c] = alpha * acc_ref[c] + pv

    def sweep(k_ref, v_ref, bias_ref, n_biased, chains_per_bias, with_decay):
        m_ref[...] = jnp.full(m_ref.shape, -jnp.inf, F32)
        l_ref[...] = jnp.zeros(l_ref.shape, F32)
        acc_ref[...] = jnp.zeros(acc_ref.shape, F32)

        def body(t, carry):
            start = pl.multiple_of((i - t) * TK, TK)

            @pl.when(t < n_biased)
            def _():
                step(k_ref, v_ref, start, lambda c: bias_ref[t, c // chains_per_bias], with_decay)

            @pl.when(t >= n_biased)
            def _():
                step(k_ref, v_ref, start, lambda c: None, with_decay)

            return carry

        lax.fori_loop(0, i + 1, body, 0)

    load_queries(qd_ref, with_decay=False)
    sweep(kd_ref, vd_ref, bnear_ref, 2, 2, with_decay=False)
    for h in range(DIFF_HEADS):
        o = (acc_ref[2 * h] / l_ref[2 * h] - lam * (acc_ref[2 * h + 1] / l_ref[2 * h + 1])).T
        o = _rms(o, gsub_ref[...]) * (1.0 - lam_init)
        od_ref[0, :, h * LANES:(h + 1) * LANES] = o.astype(BF16)

    load_queries(qf_ref, with_decay=True)
    sweep(kf_ref, vf_ref, causal_ref, 1, CHAINS, with_decay=True)
    for pair in range(FOX_HEADS // 2):
        o = jnp.where(sub < HEAD_DIM, acc_ref[2 * pair] / l_ref[2 * pair],
                      acc_ref[2 * pair + 1] / l_ref[2 * pair + 1]).T
        of_ref[0, :, pair * LANES:(pair + 1) * LANES] = o.astype(BF16)


def _attention(lam_vecs, g_subln, bias_near, qd, kd, vd, qf, kf, vf, decay, lam_init):
    batch, seq, _ = qd.shape
    const2 = lambda b, i: (0, 0)
    const4 = lambda b, i: (0, 0, 0, 0)
    kk = np.arange(TK)[:, None]
    qq = np.arange(TQ)[None, :]
    causal = jnp.asarray(np.where(kk <= qq, 0.0, -np.inf).astype(np.float32)[None, None])
    qblk = pl.BlockSpec((1, TQ, BRANCH_WIDTH), lambda b, i: (b, i, 0))
    full = pl.BlockSpec((1, seq, BRANCH_WIDTH), lambda b, i: (b, 0, 0))
    out = jax.ShapeDtypeStruct((batch, seq, BRANCH_WIDTH), BF16)
    return pl.pallas_call(
        functools.partial(_attn_kernel, lam_init),
        grid=(batch, seq // TQ),
        in_specs=[
            pl.BlockSpec(lam_vecs.shape, const2),
            pl.BlockSpec(g_subln.shape, const2),
            pl.BlockSpec(bias_near.shape, const4),
            pl.BlockSpec(causal.shape, const4),
            qblk, full, full, qblk, full, full,
            pl.BlockSpec((1, seq, LANES), lambda b, i: (b, 0, 0)),
        ],
        out_specs=[qblk, qblk],
        out_shape=[out, out],
        scratch_shapes=[
            pltpu.VMEM((CHAINS, TQ, 2 * LANES), BF16),
            pltpu.VMEM((CHAINS, 1, TQ), F32),
            pltpu.VMEM((CHAINS, 1, TQ), F32),
            pltpu.VMEM((CHAINS, LANES, TQ), F32),
        ],
        compiler_params=pltpu.CompilerParams(
            dimension_semantics=("parallel", "arbitrary"), vmem_limit_bytes=VMEM_LIMIT),
        name="attention",
    )(lam_vecs, g_subln, bias_near, causal, qd, kd, vd, qf, kf, vf, decay)


def _merge_kernel(x_ref, od_ref, of_ref, g_ref, wg_ref, wpa_ref, wpb_ref, wo_ref, y_ref):
    x = x_ref[...]
    h = _rms(x, g_ref[...]).astype(BF16)
    a = jnp.dot(od_ref[...], wpa_ref[...], preferred_element_type=F32)
    b = jnp.dot(of_ref[...], wpb_ref[...], preferred_element_type=F32)
    ga = jax.nn.sigmoid(jnp.dot(h, wg_ref[:, :D_MODEL], preferred_element_type=F32))
    merged = ga * a
    gb = jax.nn.sigmoid(jnp.dot(h, wg_ref[:, D_MODEL:], preferred_element_type=F32))
    merged = (merged + gb * b).astype(BF16)
    y_ref[...] = x + jnp.dot(merged, wo_ref[...], preferred_element_type=F32)


def _merge(x2, od, of, g_mix, w_gate, w_pa, w_pb, w_o):
    n = x2.shape[0]
    tm = TM_PROJ
    const = lambda i: (0, 0)
    row = lambda i: (i, 0)
    return pl.pallas_call(
        _merge_kernel,
        grid=(n // tm,),
        in_specs=[
            pl.BlockSpec((tm, D_MODEL), row),
            pl.BlockSpec((tm, BRANCH_WIDTH), row),
            pl.BlockSpec((tm, BRANCH_WIDTH), row),
            pl.BlockSpec((1, D_MODEL), const),
            pl.BlockSpec(w_gate.shape, const),
            pl.BlockSpec(w_pa.shape, const),
            pl.BlockSpec(w_pb.shape, const),
            pl.BlockSpec(w_o.shape, const),
        ],
        out_specs=pl.BlockSpec((tm, D_MODEL), row),
        out_shape=jax.ShapeDtypeStruct((n, D_MODEL), F32),
        compiler_params=pltpu.CompilerParams(
            dimension_semantics=("parallel",), vmem_limit_bytes=VMEM_LIMIT),
        name="merge",
    )(x2, od, of, g_mix, w_gate, w_pa, w_pb, w_o)


FF_CHUNK = 1024


def _mlp_kernel(final_norm, x_ref, g_ref, w1_ref, w2_ref, gf_ref, y_ref):
    x = x_ref[...]
    h = _rms(x, g_ref[...]).astype(BF16)
    y = x
    for c in range(D_FF // FF_CHUNK):
        cols = slice(c * FF_CHUNK, (c + 1) * FF_CHUNK)
        u = jnp.maximum(jnp.dot(h, w1_ref[:, cols], preferred_element_type=F32), 0.0)
        y = y + jnp.dot((u * u).astype(BF16), w2_ref[cols, :], preferred_element_type=F32)
    y_ref[...] = _rms(y, gf_ref[...]) if final_norm else y


def _mlp(x2, g_mlp, w_1, w_2, g_final, final_norm):
    n = x2.shape[0]
    tm = TM_PROJ
    const = lambda i: (0, 0)
    row = lambda i: (i, 0)
    single = pl.Buffered(1)
    return pl.pallas_call(
        functools.partial(_mlp_kernel, final_norm),
        grid=(n // tm,),
        in_specs=[
            pl.BlockSpec((tm, D_MODEL), row),
            pl.BlockSpec((1, D_MODEL), const),
            pl.BlockSpec(w_1.shape, const, pipeline_mode=single),
            pl.BlockSpec(w_2.shape, const, pipeline_mode=single),
            pl.BlockSpec((1, D_MODEL), const),
        ],
        out_specs=pl.BlockSpec((tm, D_MODEL), row),
        out_shape=jax.ShapeDtypeStruct((n, D_MODEL), F32),
        compiler_params=pltpu.CompilerParams(
            dimension_semantics=("parallel",), vmem_limit_bytes=VMEM_LIMIT),
        name="mlp",
    )(x2, g_mlp, w_1, w_2, g_final)


def _layer(x, layer_idx, g_mix, w_in, b_f, lam_q1, lam_k1, lam_q2, lam_k2, g_subln,
           w_pa, w_pb, w_o, g_mlp, w_1, w_2, bias_near, g_final, final_norm):
    batch, seq, d = x.shape
    n = batch * seq
    x2 = x.reshape(n, d)
    qkv_cols = 6 * BRANCH_WIDTH
    scale = HEAD_DIM ** -0.5

    col_scale = np.ones((qkv_cols,), np.float32)
    col_scale[0:BRANCH_WIDTH] = scale
    col_scale[3 * BRANCH_WIDTH:4 * BRANCH_WIDTH] = scale
    w_qkv = (w_in[:, :qkv_cols] * col_scale).astype(BF16)
    pad = LANES - DECAY_PARTS * FOX_HEADS
    w_fl = jnp.pad(jnp.tile(w_in[:, qkv_cols:qkv_cols + FOX_HEADS], (1, DECAY_PARTS)),
                   ((0, 0), (0, pad))).astype(BF16)
    b_fl = jnp.pad(jnp.tile(b_f, DECAY_PARTS), (0, pad)).reshape(1, LANES)
    w_gate = w_in[:, qkv_cols + FOX_HEADS:].astype(BF16)
    g_mix2 = g_mix.reshape(1, d)

    qd, kd, vd, qf, kf, vf, decay = _in_proj(x2, g_mix2, w_qkv, w_fl, b_fl, seq)

    lam_vecs = jnp.stack([lam_q1, lam_k1, lam_q2, lam_k2]).astype(F32)
    shape3 = (batch, seq, BRANCH_WIDTH)
    od, of = _attention(lam_vecs, g_subln.reshape(1, LANES), bias_near,
                        qd.reshape(shape3), kd.reshape(shape3), vd.reshape(shape3),
                        qf.reshape(shape3), kf.reshape(shape3), vf.reshape(shape3),
                        decay.reshape(batch, seq, LANES), _lambda_init(layer_idx))

    x1 = _merge(x2, od.reshape(n, BRANCH_WIDTH), of.reshape(n, BRANCH_WIDTH), g_mix2,
                w_gate, w_pa.astype(BF16), w_pb.astype(BF16), w_o.astype(BF16))
    y = _mlp(x1, g_mlp.reshape(1, d), w_1.astype(BF16), w_2.astype(BF16), g_final, final_norm)
    return y.reshape(batch, seq, d)


def kernel(x, g_mix, w_in, b_f, lam_q1, lam_k1, lam_q2, lam_k2, g_subln, w_pa, w_pb, w_o,
           g_mlp, w_1, w_2, rel_table, g_final):
    depth = g_mix.shape[0]
    bias_near = _bias_tiles(rel_table, x.shape[1])
    for l in range(depth):
        x = _layer(x, l, g_mix[l], w_in[l], b_f[l], lam_q1[l], lam_k1[l], lam_q2[l], lam_k2[l],
                   g_subln[l], w_pa[l], w_pb[l], w_o[l], g_mlp[l], w_1[l], w_2[l],
                   bias_near, g_final.reshape(1, -1), l == depth - 1)
    return x
```

```python
import functools
import math

import numpy as np
import jax
import jax.numpy as jnp
from jax import lax
from jax.experimental import pallas as pl
from jax.experimental.pallas import tpu as pltpu

D_MODEL = 1024
CHUNK = 64
HEAD_DIM = 64
DIFF_HEADS = 4
FOX_HEADS = 8
BRANCH_WIDTH = 512
D_FF = 4 * D_MODEL
REL_BUCKETS = 32
REL_MAX_DIST = 128
EPS = 1e-6
LANES = 128
MASKED_BUCKET = REL_BUCKETS

TQ = 256
TK = 256
TM_PROJ = 512
VMEM_LIMIT = 56 * 1024 * 1024

LOG2E = math.log2(math.e)
Q_SCALE = HEAD_DIM ** -0.5 * LOG2E

F32 = jnp.float32
BF16 = jnp.bfloat16
NT_DIMS = (((1,), (1,)), ((), ()))
TN_DIMS = (((0,), (0,)), ((), ()))


def _lambda_init(layer_idx):
    return 0.8 - 0.6 * math.exp(-0.3 * layer_idx)


def _rms(xf, g):
    return xf * lax.rsqrt(jnp.mean(xf * xf, axis=-1, keepdims=True) + EPS) * g


DECAY_PARTS = 3


def _in_proj_kernel(tiles_per_seq, x_ref, g_ref, w_ref, wfl_ref, bf_ref,
                    qd_ref, kd_ref, vd_ref, qf_ref, kf_ref, vf_ref, dec_ref, carry_ref):
    h = _rms(x_ref[...], g_ref[...]).astype(BF16)

    @pl.when(pl.program_id(0) % tiles_per_seq == 0)
    def _():
        carry_ref[...] = jnp.zeros_like(carry_ref)

    z = jnp.dot(h, wfl_ref[...], preferred_element_type=F32) + bf_ref[...]
    acc = jnp.minimum(z, 0.0) - jnp.log1p(jnp.exp(-jnp.abs(z)))
    rows = acc.shape[0]
    row = lax.broadcasted_iota(jnp.int32, acc.shape, 0)
    d = 1
    while d < rows:
        acc = acc + jnp.where(row >= d, pltpu.roll(acc, d, axis=0), 0.0)
        d *= 2
    acc = acc + carry_ref[...]
    carry_ref[...] = acc[rows - 1:rows, :]
    neg = acc * -LOG2E
    hi = neg.astype(BF16).astype(F32)
    mid = (neg - hi).astype(BF16).astype(F32)
    lo = neg - hi - mid
    lane = lax.broadcasted_iota(jnp.int32, acc.shape, 1)
    piece = jnp.where(lane < FOX_HEADS, hi, jnp.where(lane < 2 * FOX_HEADS, mid, lo))
    dec_ref[...] = jnp.where(lane < DECAY_PARTS * FOX_HEADS, piece, 0.0).astype(BF16)

    outs = (qd_ref, kd_ref, vd_ref, qf_ref, kf_ref, vf_ref)
    for c, o_ref in enumerate(outs):
        w = w_ref[:, c * BRANCH_WIDTH:(c + 1) * BRANCH_WIDTH]
        o = jnp.dot(h, w, preferred_element_type=F32)
        if o_ref is qd_ref or o_ref is qf_ref:
            o = o * Q_SCALE
        o_ref[...] = o.astype(BF16)


def _in_proj(x2, g_mix, w_qkv, w_fl, b_f, seq):
    n = x2.shape[0]
    tm = TM_PROJ
    const = lambda i: (0, 0)
    row = lambda i: (i, 0)
    out_bf = jax.ShapeDtypeStruct((n, BRANCH_WIDTH), BF16)
    return pl.pallas_call(
        functools.partial(_in_proj_kernel, seq // tm),
        grid=(n // tm,),
        in_specs=[
            pl.BlockSpec((tm, D_MODEL), row),
            pl.BlockSpec((1, D_MODEL), const),
            pl.BlockSpec(w_qkv.shape, const),
            pl.BlockSpec(w_fl.shape, const),
            pl.BlockSpec((1, LANES), const),
        ],
        out_specs=[pl.BlockSpec((tm, BRANCH_WIDTH), row)] * 6
        + [pl.BlockSpec((tm, LANES), row)],
        out_shape=[out_bf] * 6 + [jax.ShapeDtypeStruct((n, LANES), BF16)],
        scratch_shapes=[pltpu.VMEM((1, LANES), F32)],
        compiler_params=pltpu.CompilerParams(
            dimension_semantics=("arbitrary",), vmem_limit_bytes=VMEM_LIMIT),
        name="in_proj",
    )(x2, g_mix, w_qkv, w_fl, b_f)


def _rel_bucket_np(rel):
    nb = REL_BUCKETS // 2
    ret = np.where(rel > 0, nb, 0)
    n = np.abs(rel)
    max_exact = nb // 2
    nf = np.maximum(n, 1).astype(np.float64)
    large = max_exact + (np.log(nf / max_exact) / math.log(REL_MAX_DIST / max_exact)
                         * (nb - max_exact)).astype(np.int32)
    large = np.minimum(large, nb - 1)
    return (ret + np.where(n < max_exact, n, large)).astype(np.int32)


def _bias_index_maps(seq):
    kk = np.arange(TK, dtype=np.int64)[:, None]
    qq = np.arange(TQ, dtype=np.int64)[None, :]
    diag = _rel_bucket_np(kk - qq)
    diag = np.where(kk // CHUNK <= qq // CHUNK, diag, MASKED_BUCKET).astype(np.int32)
    prev = _rel_bucket_np(kk - TK - qq)
    far = _rel_bucket_np(np.arange(-seq, -TK, dtype=np.int64))
    far_bucket = int(far[0])
    assert (far == far_bucket).all(), "keys two tiles back must share one bucket"
    return diag, prev, far_bucket


def _bias_kernel(far_bucket, tab_ref, idx_ref, out_ref):
    for h in range(DIFF_HEADS):
        far = tab_ref[far_bucket, h]
        for t in range(idx_ref.shape[0]):
            idx = idx_ref[t]
            acc = jnp.full(idx.shape, -jnp.inf, F32)
            for b in range(REL_BUCKETS):
                acc = jnp.where(idx == b, (tab_ref[b, h] - far) * LOG2E, acc)
            out_ref[t, h] = acc


def _bias_tiles(rel_table, seq):
    diag, prev, far_bucket = _bias_index_maps(seq)
    idx = np.stack([diag, prev])
    vmem = pl.BlockSpec(memory_space=pltpu.VMEM)
    return pl.pallas_call(
        functools.partial(_bias_kernel, far_bucket),
        in_specs=[pl.BlockSpec(memory_space=pltpu.SMEM), vmem],
        out_specs=vmem,
        out_shape=jax.ShapeDtypeStruct((idx.shape[0], DIFF_HEADS, TK, TQ), F32),
        name="bias_tiles",
    )(rel_table, jnp.asarray(idx))


CHAINS = 8
QK_AHEAD = 8


def _attn_kernel(lam_init, lam_ref, gsub_ref, bnear_ref, causal_ref,
                 qd_ref, kd_ref, vd_ref, qf_ref, kf_ref, vf_ref, dec_ref,
                 od_ref, of_ref, qm_ref, m_ref, l_ref, acc_ref):
    i = pl.program_id(1)
    lane = lax.broadcasted_iota(jnp.int32, (TQ, LANES), 1)
    half = (lane < HEAD_DIM, lane >= HEAD_DIM)
    sub = lax.broadcasted_iota(jnp.int32, (LANES, TQ), 0)
    lam_v = lam_ref[...]
    lam = (jnp.exp(jnp.sum(lam_v[0:1] * lam_v[1:2], axis=-1, keepdims=True))
           - jnp.exp(jnp.sum(lam_v[2:3] * lam_v[3:4], axis=-1, keepdims=True))
           + lam_init)

    def load_queries(q_ref, with_decay):
        for blk in range(CHAINS // 2):
            q = q_ref[0, :, blk * LANES:(blk + 1) * LANES]
            for e in range(2):
                c = 2 * blk + e
                qm_ref[c, :, :LANES] = jnp.where(half[e], q, jnp.zeros_like(q))
                if with_decay:
                    pick = (lane < DECAY_PARTS * FOX_HEADS) & (lane % FOX_HEADS == c)
                    qm_ref[c, :, LANES:] = jnp.where(pick, 1.0, 0.0).astype(BF16)

    def step(k_ref, v_ref, start, extra, with_decay):
        def cols(c):
            return slice((c // 2) * LANES, (c // 2 + 1) * LANES)

        def scores(c):
            k_t = k_ref[0, pl.ds(start, TK), cols(c)]
            q_t = qm_ref[c, :, :LANES]
            if with_decay:
                k_t = jnp.concatenate([k_t, dec_ref[0, pl.ds(start, TK), :]], axis=1)
                q_t = qm_ref[c]
            s = lax.dot_general(k_t, q_t, NT_DIMS, preferred_element_type=F32)
            add = extra(c)
            return s if add is None else s + add

        pending = {c: scores(c) for c in range(QK_AHEAD)}
        for c in range(CHAINS):
            s = pending.pop(c)
            m_old = m_ref[c]
            m_new = jnp.maximum(m_old, jnp.max(s, axis=0, keepdims=True))
            alpha = jnp.exp2(m_old - m_new)
            p = jnp.exp2(s - m_new)
            l_new = alpha * l_ref[c] + jnp.sum(p, axis=0, keepdims=True)
            v_t = v_ref[0, pl.ds(start, TK), cols(c)]
            pv = lax.dot_general(v_t, p.astype(BF16), TN_DIMS, preferred_element_type=F32)
            if c + QK_AHEAD < CHAINS:
                pending[c + QK_AHEAD] = scores(c + QK_AHEAD)
            m_ref[c] = m_new
            l_ref[c] = l_new
            acc_ref[c] = alpha * acc_ref[c] + pv

    def sweep(k_ref, v_ref, bias_ref, n_biased, chains_per_bias, with_decay):
        m_ref[...] = jnp.full(m_ref.shape, -jnp.inf, F32)
        l_ref[...] = jnp.zeros(l_ref.shape, F32)
        acc_ref[...] = jnp.zeros(acc_ref.shape, F32)

        def body(t, carry):
            start = pl.multiple_of((i - t) * TK, TK)

            @pl.when(t < n_biased)
            def _():
                step(k_ref, v_ref, start, lambda c: bias_ref[t, c // chains_per_bias], with_decay)

            @pl.when(t >= n_biased)
            def _():
                step(k_ref, v_ref, start, lambda c: None, with_decay)

            return carry

        lax.fori_loop(0, i + 1, body, 0)

    load_queries(qd_ref, with_decay=False)
    sweep(kd_ref, vd_ref, bnear_ref, 2, 2, with_decay=False)
    for h in range(DIFF_HEADS):
        o = (acc_ref[2 * h] / l_ref[2 * h] - lam * (acc_ref[2 * h + 1] / l_ref[2 * h + 1])).T
        o = _rms(o, gsub_ref[...]) * (1.0 - lam_init)
        od_ref[0, :, h * LANES:(h + 1) * LANES] = o.astype(BF16)

    load_queries(qf_ref, with_decay=True)
    sweep(kf_ref, vf_ref, causal_ref, 1, CHAINS, with_decay=True)
    for pair in range(FOX_HEADS // 2):
        o = jnp.where(sub < HEAD_DIM, acc_ref[2 * pair] / l_ref[2 * pair],
                      acc_ref[2 * pair + 1] / l_ref[2 * pair + 1]).T
        of_ref[0, :, pair * LANES:(pair + 1) * LANES] = o.astype(BF16)


def _attention(lam_vecs, g_subln, bias_near, qd, kd, vd, qf, kf, vf, decay, lam_init):
    batch, seq, _ = qd.shape
    const2 = lambda b, i: (0, 0)
    const4 = lambda b, i: (0, 0, 0, 0)
    kk = np.arange(TK)[:, None]
    qq = np.arange(TQ)[None, :]
    causal = jnp.asarray(np.where(kk <= qq, 0.0, -np.inf).astype(np.float32)[None, None])
    qblk = pl.BlockSpec((1, TQ, BRANCH_WIDTH), lambda b, i: (b, i, 0))
    full = pl.BlockSpec((1, seq, BRANCH_WIDTH), lambda b, i: (b, 0, 0))
    out = jax.ShapeDtypeStruct((batch, seq, BRANCH_WIDTH), BF16)
    return pl.pallas_call(
        functools.partial(_attn_kernel, lam_init),
        grid=(batch, seq // TQ),
        in_specs=[
            pl.BlockSpec(lam_vecs.shape, const2),
            pl.BlockSpec(g_subln.shape, const2),
            pl.BlockSpec(bias_near.shape, const4),
            pl.BlockSpec(causal.shape, const4),
            qblk, full, full, qblk, full, full,
            pl.BlockSpec((1, seq, LANES), lambda b, i: (b, 0, 0)),
        ],
        out_specs=[qblk, qblk],
        out_shape=[out, out],
        scratch_shapes=[
            pltpu.VMEM((CHAINS, TQ, 2 * LANES), BF16),
            pltpu.VMEM((CHAINS, 1, TQ), F32),
            pltpu.VMEM((CHAINS, 1, TQ), F32),
            pltpu.VMEM((CHAINS, LANES, TQ), F32),
        ],
        compiler_params=pltpu.CompilerParams(
            dimension_semantics=("parallel", "arbitrary"), vmem_limit_bytes=VMEM_LIMIT),
        name="attention",
    )(lam_vecs, g_subln, bias_near, causal, qd, kd, vd, qf, kf, vf, decay)


def _merge_kernel(x_ref, od_ref, of_ref, g_ref, wg_ref, wpa_ref, wpb_ref, wo_ref, y_ref):
    x = x_ref[...]
    h = _rms(x, g_ref[...]).astype(BF16)
    a = jnp.dot(od_ref[...], wpa_ref[...], preferred_element_type=F32)
    b = jnp.dot(of_ref[...], wpb_ref[...], preferred_element_type=F32)
    ga = jax.nn.sigmoid(jnp.dot(h, wg_ref[:, :D_MODEL], preferred_element_type=F32))
    merged = ga * a
    gb = jax.nn.sigmoid(jnp.dot(h, wg_ref[:, D_MODEL:], preferred_element_type=F32))
    merged = (merged + gb * b).astype(BF16)
    y_ref[...] = x + jnp.dot(merged, wo_ref[...], preferred_element_type=F32)


def _merge(x2, od, of, g_mix, w_gate, w_pa, w_pb, w_o):
    n = x2.shape[0]
    tm = TM_PROJ
    const = lambda i: (0, 0)
    row = lambda i: (i, 0)
    return pl.pallas_call(
        _merge_kernel,
        grid=(n // tm,),
        in_specs=[
            pl.BlockSpec((tm, D_MODEL), row),
            pl.BlockSpec((tm, BRANCH_WIDTH), row),
            pl.BlockSpec((tm, BRANCH_WIDTH), row),
            pl.BlockSpec((1, D_MODEL), const),
            pl.BlockSpec(w_gate.shape, const),
            pl.BlockSpec(w_pa.shape, const),
            pl.BlockSpec(w_pb.shape, const),
            pl.BlockSpec(w_o.shape, const),
        ],
        out_specs=pl.BlockSpec((tm, D_MODEL), row),
        out_shape=jax.ShapeDtypeStruct((n, D_MODEL), F32),
        compiler_params=pltpu.CompilerParams(
            dimension_semantics=("parallel",), vmem_limit_bytes=VMEM_LIMIT),
        name="merge",
    )(x2, od, of, g_mix, w_gate, w_pa, w_pb, w_o)


FF_CHUNK = 1024


def _mlp_kernel(final_norm, x_ref, g_ref, w1_ref, w2_ref, gf_ref, y_ref):
    x = x_ref[...]
    h = _rms(x, g_ref[...]).astype(BF16)
    y = x
    for c in range(D_FF // FF_CHUNK):
        cols = slice(c * FF_CHUNK, (c + 1) * FF_CHUNK)
        u = jnp.maximum(jnp.dot(h, w1_ref[:, cols], preferred_element_type=F32), 0.0)
        y = y + jnp.dot((u * u).astype(BF16), w2_ref[cols, :], preferred_element_type=F32)
    y_ref[...] = _rms(y, gf_ref[...]) if final_norm else y


def _mlp(x2, g_mlp, w_1, w_2, g_final, final_norm):
    n = x2.shape[0]
    tm = TM_PROJ
    const = lambda i: (0, 0)
    row = lambda i: (i, 0)
    single = pl.Buffered(1)
    return pl.pallas_call(
        functools.partial(_mlp_kernel, final_norm),
        grid=(n // tm,),
        in_specs=[
            pl.BlockSpec((tm, D_MODEL), row),
            pl.BlockSpec((1, D_MODEL), const),
            pl.BlockSpec(w_1.shape, const, pipeline_mode=single),
            pl.BlockSpec(w_2.shape, const, pipeline_mode=single),
            pl.BlockSpec((1, D_MODEL), const),
        ],
        out_specs=pl.BlockSpec((tm, D_MODEL), row),
        out_shape=jax.ShapeDtypeStruct((n, D_MODEL), F32),
        compiler_params=pltpu.CompilerParams(
            dimension_semantics=("parallel",), vmem_limit_bytes=VMEM_LIMIT),
        name="mlp",
    )(x2, g_mlp, w_1, w_2, g_final)


def _layer(x, layer_idx, g_mix, w_in, b_f, lam_q1, lam_k1, lam_q2, lam_k2, g_subln,
           w_pa, w_pb, w_o, g_mlp, w_1, w_2, bias_near, g_final, final_norm):
    batch, seq, d = x.shape
    n = batch * seq
    x2 = x.reshape(n, d)
    qkv_cols = 6 * BRANCH_WIDTH
    w_qkv = w_in[:, :qkv_cols].astype(BF16)
    pad = LANES - DECAY_PARTS * FOX_HEADS
    w_fl = jnp.pad(jnp.tile(w_in[:, qkv_cols:qkv_cols + FOX_HEADS], (1, DECAY_PARTS)),
                   ((0, 0), (0, pad))).astype(BF16)
    b_fl = jnp.pad(jnp.tile(b_f, DECAY_PARTS), (0, pad)).reshape(1, LANES)
    w_gate = w_in[:, qkv_cols + FOX_HEADS:].astype(BF16)
    g_mix2 = g_mix.reshape(1, d)

    qd, kd, vd, qf, kf, vf, decay = _in_proj(x2, g_mix2, w_qkv, w_fl, b_fl, seq)

    lam_vecs = jnp.stack([lam_q1, lam_k1, lam_q2, lam_k2]).astype(F32)
    shape3 = (batch, seq, BRANCH_WIDTH)
    od, of = _attention(lam_vecs, g_subln.reshape(1, LANES), bias_near,
                        qd.reshape(shape3), kd.reshape(shape3), vd.reshape(shape3),
                        qf.reshape(shape3), kf.reshape(shape3), vf.reshape(shape3),
                        decay.reshape(batch, seq, LANES), _lambda_init(layer_idx))

    x1 = _merge(x2, od.reshape(n, BRANCH_WIDTH), of.reshape(n, BRANCH_WIDTH), g_mix2,
                w_gate, w_pa.astype(BF16), w_pb.astype(BF16), w_o.astype(BF16))
    y = _mlp(x1, g_mlp.reshape(1, d), w_1.astype(BF16), w_2.astype(BF16), g_final, final_norm)
    return y.reshape(batch, seq, d)


def kernel(x, g_mix, w_in, b_f, lam_q1, lam_k1, lam_q2, lam_k2, g_subln, w_pa, w_pb, w_o,
           g_mlp, w_1, w_2, rel_table, g_final):
    depth = g_mix.shape[0]
    bias_near = _bias_tiles(rel_table, x.shape[1])
    for l in range(depth):
        x = _layer(x, l, g_mix[l], w_in[l], b_f[l], lam_q1[l], lam_k1[l], lam_q2[l], lam_k2[l],
                   g_subln[l], w_pa[l], w_pb[l], w_o[l], g_mlp[l], w_1[l], w_2[l],
                   bias_near, g_final.reshape(1, -1), l == depth - 1)
    return x
```

```python
import functools
import math

import numpy as np
import jax
import jax.numpy as jnp
from jax import lax
from jax.experimental import pallas as pl
from jax.experimental.pallas import tpu as pltpu

D_MODEL = 1024
CHUNK = 64
HEAD_DIM = 64
DIFF_HEADS = 4
FOX_HEADS = 8
BRANCH_WIDTH = 512
D_FF = 4 * D_MODEL
REL_BUCKETS = 32
REL_MAX_DIST = 128
EPS = 1e-6
LANES = 128
MASKED_BUCKET = REL_BUCKETS

TQ = 256
TK = 256
TM_PROJ = 512
VMEM_LIMIT = 56 * 1024 * 1024

LOG2E = math.log2(math.e)
Q_SCALE = HEAD_DIM ** -0.5 * LOG2E

F32 = jnp.float32
BF16 = jnp.bfloat16
NT_DIMS = (((1,), (1,)), ((), ()))
TN_DIMS = (((0,), (0,)), ((), ()))


def _lambda_init(layer_idx):
    return 0.8 - 0.6 * math.exp(-0.3 * layer_idx)


def _rms(xf, g):
    return xf * lax.rsqrt(jnp.mean(xf * xf, axis=-1, keepdims=True) + EPS) * g


DECAY_PARTS = 3


def _in_proj_kernel(tiles_per_seq, x_ref, g_ref, w_ref, wfl_ref, bf_ref,
                    qd_ref, kd_ref, vd_ref, qf_ref, kf_ref, vf_ref, dec_ref, carry_ref):
    h = _rms(x_ref[...], g_ref[...]).astype(BF16)

    @pl.when(pl.program_id(0) % tiles_per_seq == 0)
    def _():
        carry_ref[...] = jnp.zeros_like(carry_ref)

    z = jnp.dot(h, wfl_ref[...], preferred_element_type=F32) + bf_ref[...]
    acc = jnp.minimum(z, 0.0) - jnp.log1p(jnp.exp(-jnp.abs(z)))
    rows = acc.shape[0]
    row = lax.broadcasted_iota(jnp.int32, acc.shape, 0)
    d = 1
    while d < rows:
        acc = acc + jnp.where(row >= d, pltpu.roll(acc, d, axis=0), 0.0)
        d *= 2
    acc = acc + carry_ref[...]
    carry_ref[...] = acc[rows - 1:rows, :]
    neg = acc * -LOG2E
    hi = neg.astype(BF16).astype(F32)
    mid = (neg - hi).astype(BF16).astype(F32)
    lo = neg - hi - mid
    lane = lax.broadcasted_iota(jnp.int32, acc.shape, 1)
    piece = jnp.where(lane < FOX_HEADS, hi, jnp.where(lane < 2 * FOX_HEADS, mid, lo))
    dec_ref[...] = jnp.where(lane < DECAY_PARTS * FOX_HEADS, piece, 0.0).astype(BF16)

    outs = (qd_ref, kd_ref, vd_ref, qf_ref, kf_ref, vf_ref)
    for c, o_ref in enumerate(outs):
        w = w_ref[:, c * BRANCH_WIDTH:(c + 1) * BRANCH_WIDTH]
        o = jnp.dot(h, w, preferred_element_type=F32)
        if o_ref is qd_ref or o_ref is qf_ref:
            o = o * Q_SCALE
        o_ref[...] = o.astype(BF16)


def _in_proj(x2, g_mix, w_qkv, w_fl, b_f, seq):
    n = x2.shape[0]
    tm = TM_PROJ
    const = lambda i: (0, 0)
    row = lambda i: (i, 0)
    out_bf = jax.ShapeDtypeStruct((n, BRANCH_WIDTH), BF16)
    return pl.pallas_call(
        functools.partial(_in_proj_kernel, seq // tm),
        grid=(n // tm,),
        in_specs=[
            pl.BlockSpec((tm, D_MODEL), row),
            pl.BlockSpec((1, D_MODEL), const),
            pl.BlockSpec(w_qkv.shape, const),
            pl.BlockSpec(w_fl.shape, const),
            pl.BlockSpec((1, LANES), const),
        ],
        out_specs=[pl.BlockSpec((tm, BRANCH_WIDTH), row)] * 6
        + [pl.BlockSpec((tm, LANES), row)],
        out_shape=[out_bf] * 6 + [jax.ShapeDtypeStruct((n, LANES), BF16)],
        scratch_shapes=[pltpu.VMEM((1, LANES), F32)],
        compiler_params=pltpu.CompilerParams(
            dimension_semantics=("arbitrary",), vmem_limit_bytes=VMEM_LIMIT),
        name="in_proj",
    )(x2, g_mix, w_qkv, w_fl, b_f)


def _rel_bucket_np(rel):
    nb = REL_BUCKETS // 2
    ret = np.where(rel > 0, nb, 0)
    n = np.abs(rel)
    max_exact = nb // 2
    nf = np.maximum(n, 1).astype(np.float64)
    large = max_exact + (np.log(nf / max_exact) / math.log(REL_MAX_DIST / max_exact)
                         * (nb - max_exact)).astype(np.int32)
    large = np.minimum(large, nb - 1)
    return (ret + np.where(n < max_exact, n, large)).astype(np.int32)


def _bias_index_maps(seq):
    kk = np.arange(TK, dtype=np.int64)[:, None]
    qq = np.arange(TQ, dtype=np.int64)[None, :]
    diag = _rel_bucket_np(kk - qq)
    diag = np.where(kk // CHUNK <= qq // CHUNK, diag, MASKED_BUCKET).astype(np.int32)
    prev = _rel_bucket_np(kk - TK - qq)
    far = _rel_bucket_np(np.arange(-seq, -TK, dtype=np.int64))
    far_bucket = int(far[0])
    assert (far == far_bucket).all(), "keys two tiles back must share one bucket"
    return diag, prev, far_bucket


def _bias_kernel(far_bucket, tab_ref, idx_ref, out_ref):
    for h in range(DIFF_HEADS):
        far = tab_ref[far_bucket, h]
        for t in range(idx_ref.shape[0]):
            idx = idx_ref[t]
            acc = jnp.full(idx.shape, -jnp.inf, F32)
            for b in range(REL_BUCKETS):
                acc = jnp.where(idx == b, (tab_ref[b, h] - far) * LOG2E, acc)
            out_ref[t, h] = acc


def _bias_tiles(rel_table, seq):
    diag, prev, far_bucket = _bias_index_maps(seq)
    idx = np.stack([diag, prev])
    vmem = pl.BlockSpec(memory_space=pltpu.VMEM)
    return pl.pallas_call(
        functools.partial(_bias_kernel, far_bucket),
        in_specs=[pl.BlockSpec(memory_space=pltpu.SMEM), vmem],
        out_specs=vmem,
        out_shape=jax.ShapeDtypeStruct((idx.shape[0], DIFF_HEADS, TK, TQ), F32),
        name="bias_tiles",
    )(rel_table, jnp.asarray(idx))


DIFF_CHAINS = 2 * DIFF_HEADS
CHAINS = DIFF_CHAINS + FOX_HEADS
QK_AHEAD = 8


def _attn_kernel(lam_init, lam_ref, gsub_ref, bnear_ref, causal_ref,
                 qd_ref, kd_ref, vd_ref, qf_ref, kf_ref, vf_ref, dec_ref,
                 od_ref, of_ref, qm_ref, m_ref, l_ref, acc_ref):
    i = pl.program_id(1)
    lane = lax.broadcasted_iota(jnp.int32, (TQ, LANES), 1)
    half = (lane < HEAD_DIM, lane >= HEAD_DIM)
    sub = lax.broadcasted_iota(jnp.int32, (LANES, TQ), 0)
    lam_v = lam_ref[...]
    lam = (jnp.exp(jnp.sum(lam_v[0:1] * lam_v[1:2], axis=-1, keepdims=True))
           - jnp.exp(jnp.sum(lam_v[2:3] * lam_v[3:4], axis=-1, keepdims=True))
           + lam_init)

    def is_fox(c):
        return c >= DIFF_CHAINS

    def cols(c):
        blk = (c % DIFF_CHAINS) // 2
        return slice(blk * LANES, (blk + 1) * LANES)

    for c in range(CHAINS):
        q = (qf_ref if is_fox(c) else qd_ref)[0, :, cols(c)]
        qm_ref[c, :, :LANES] = jnp.where(half[c % 2], q, jnp.zeros_like(q))
        if is_fox(c):
            head = c - DIFF_CHAINS
            pick = (lane < DECAY_PARTS * FOX_HEADS) & (lane % FOX_HEADS == head)
            qm_ref[c, :, LANES:] = jnp.where(pick, 1.0, 0.0).astype(BF16)

    m_ref[...] = jnp.full(m_ref.shape, -jnp.inf, F32)
    l_ref[...] = jnp.zeros(l_ref.shape, F32)
    acc_ref[...] = jnp.zeros(acc_ref.shape, F32)

    def step(start, extra):
        def scores(c):
            if is_fox(c):
                k_t = jnp.concatenate([kf_ref[0, pl.ds(start, TK), cols(c)],
                                       dec_ref[0, pl.ds(start, TK), :]], axis=1)
                q_t = qm_ref[c]
            else:
                k_t = kd_ref[0, pl.ds(start, TK), cols(c)]
                q_t = qm_ref[c, :, :LANES]
            s = lax.dot_general(k_t, q_t, NT_DIMS, preferred_element_type=F32)
            add = extra(c)
            return s if add is None else s + add

        pending = {c: scores(c) for c in range(QK_AHEAD)}
        for c in range(CHAINS):
            s = pending.pop(c)
            m_old = m_ref[c]
            m_new = jnp.maximum(m_old, jnp.max(s, axis=0, keepdims=True))
            alpha = jnp.exp2(m_old - m_new)
            p = jnp.exp2(s - m_new)
            l_new = alpha * l_ref[c] + jnp.sum(p, axis=0, keepdims=True)
            v_t = (vf_ref if is_fox(c) else vd_ref)[0, pl.ds(start, TK), cols(c)]
            pv = lax.dot_general(v_t, p.astype(BF16), TN_DIMS, preferred_element_type=F32)
            if c + QK_AHEAD < CHAINS:
                pending[c + QK_AHEAD] = scores(c + QK_AHEAD)
            m_ref[c] = m_new
            l_ref[c] = l_new
            acc_ref[c] = alpha * acc_ref[c] + pv

    def diag_extra(c):
        return causal_ref[...] if is_fox(c) else bnear_ref[0, c // 2]

    def prev_extra(c):
        return None if is_fox(c) else bnear_ref[1, c // 2]

    def body(t, carry):
        start = pl.multiple_of((i - t) * TK, TK)
        pl.when(t == 0)(lambda: step(start, diag_extra))
        pl.when(t == 1)(lambda: step(start, prev_extra))
        pl.when(t >= 2)(lambda: step(start, lambda c: None))
        return carry

    lax.fori_loop(0, i + 1, body, 0)

    for h in range(DIFF_HEADS):
        o = (acc_ref[2 * h] / l_ref[2 * h] - lam * (acc_ref[2 * h + 1] / l_ref[2 * h + 1])).T
        o = _rms(o, gsub_ref[...]) * (1.0 - lam_init)
        od_ref[0, :, h * LANES:(h + 1) * LANES] = o.astype(BF16)
    for pair in range(FOX_HEADS // 2):
        c = DIFF_CHAINS + 2 * pair
        o = jnp.where(sub < HEAD_DIM, acc_ref[c] / l_ref[c], acc_ref[c + 1] / l_ref[c + 1]).T
        of_ref[0, :, pair * LANES:(pair + 1) * LANES] = o.astype(BF16)


def _attention(lam_vecs, g_subln, bias_near, qd, kd, vd, qf, kf, vf, decay, lam_init):
    batch, seq, _ = qd.shape
    const2 = lambda b, i: (0, 0)
    const4 = lambda b, i: (0, 0, 0, 0)
    kk = np.arange(TK)[:, None]
    qq = np.arange(TQ)[None, :]
    causal = jnp.asarray(np.where(kk <= qq, 0.0, -np.inf).astype(np.float32))
    qblk = pl.BlockSpec((1, TQ, BRANCH_WIDTH), lambda b, i: (b, i, 0))
    full = pl.BlockSpec((1, seq, BRANCH_WIDTH), lambda b, i: (b, 0, 0))
    out = jax.ShapeDtypeStruct((batch, seq, BRANCH_WIDTH), BF16)
    return pl.pallas_call(
        functools.partial(_attn_kernel, lam_init),
        grid=(batch, seq // TQ),
        in_specs=[
            pl.BlockSpec(lam_vecs.shape, const2),
            pl.BlockSpec(g_subln.shape, const2),
            pl.BlockSpec(bias_near.shape, const4),
            pl.BlockSpec(causal.shape, const2),
            qblk, full, full, qblk, full, full,
            pl.BlockSpec((1, seq, LANES), lambda b, i: (b, 0, 0)),
        ],
        out_specs=[qblk, qblk],
        out_shape=[out, out],
        scratch_shapes=[
            pltpu.VMEM((CHAINS, TQ, 2 * LANES), BF16),
            pltpu.VMEM((CHAINS, 1, TQ), F32),
            pltpu.VMEM((CHAINS, 1, TQ), F32),
            pltpu.VMEM((CHAINS, LANES, TQ), F32),
        ],
        compiler_params=pltpu.CompilerParams(
            dimension_semantics=("parallel", "arbitrary"), vmem_limit_bytes=VMEM_LIMIT),
        name="attention",
    )(lam_vecs, g_subln, bias_near, causal, qd, kd, vd, qf, kf, vf, decay)


def _merge_kernel(x_ref, od_ref, of_ref, g_ref, wg_ref, wpa_ref, wpb_ref, wo_ref, y_ref):
    x = x_ref[...]
    h = _rms(x, g_ref[...]).astype(BF16)
    a = jnp.dot(od_ref[...], wpa_ref[...], preferred_element_type=F32)
    b = jnp.dot(of_ref[...], wpb_ref[...], preferred_element_type=F32)
    ga = jax.nn.sigmoid(jnp.dot(h, wg_ref[:, :D_MODEL], preferred_element_type=F32))
    merged = ga * a
    gb = jax.nn.sigmoid(jnp.dot(h, wg_ref[:, D_MODEL:], preferred_element_type=F32))
    merged = (merged + gb * b).astype(BF16)
    y_ref[...] = x + jnp.dot(merged, wo_ref[...], preferred_element_type=F32)


def _merge(x2, od, of, g_mix, w_gate, w_pa, w_pb, w_o):
    n = x2.shape[0]
    tm = TM_PROJ
    const = lambda i: (0, 0)
    row = lambda i: (i, 0)
    return pl.pallas_call(
        _merge_kernel,
        grid=(n // tm,),
        in_specs=[
            pl.BlockSpec((tm, D_MODEL), row),
            pl.BlockSpec((tm, BRANCH_WIDTH), row),
            pl.BlockSpec((tm, BRANCH_WIDTH), row),
            pl.BlockSpec((1, D_MODEL), const),
            pl.BlockSpec(w_gate.shape, const),
            pl.BlockSpec(w_pa.shape, const),
            pl.BlockSpec(w_pb.shape, const),
            pl.BlockSpec(w_o.shape, const),
        ],
        out_specs=pl.BlockSpec((tm, D_MODEL), row),
        out_shape=jax.ShapeDtypeStruct((n, D_MODEL), F32),
        compiler_params=pltpu.CompilerParams(
            dimension_semantics=("parallel",), vmem_limit_bytes=VMEM_LIMIT),
        name="merge",
    )(x2, od, of, g_mix, w_gate, w_pa, w_pb, w_o)


FF_CHUNK = 1024


def _mlp_kernel(final_norm, x_ref, g_ref, w1_ref, w2_ref, gf_ref, y_ref):
    x = x_ref[...]
    h = _rms(x, g_ref[...]).astype(BF16)
    y = x
    for c in range(D_FF // FF_CHUNK):
        cols = slice(c * FF_CHUNK, (c + 1) * FF_CHUNK)
        u = jnp.maximum(jnp.dot(h, w1_ref[:, cols], preferred_element_type=F32), 0.0)
        y = y + jnp.dot((u * u).astype(BF16), w2_ref[cols, :], preferred_element_type=F32)
    y_ref[...] = _rms(y, gf_ref[...]) if final_norm else y


def _mlp(x2, g_mlp, w_1, w_2, g_final, final_norm):
    n = x2.shape[0]
    tm = TM_PROJ
    const = lambda i: (0, 0)
    row = lambda i: (i, 0)
    single = pl.Buffered(1)
    return pl.pallas_call(
        functools.partial(_mlp_kernel, final_norm),
        grid=(n // tm,),
        in_specs=[
            pl.BlockSpec((tm, D_MODEL), row),
            pl.BlockSpec((1, D_MODEL), const),
            pl.BlockSpec(w_1.shape, const, pipeline_mode=single),
            pl.BlockSpec(w_2.shape, const, pipeline_mode=single),
            pl.BlockSpec((1, D_MODEL), const),
        ],
        out_specs=pl.BlockSpec((tm, D_MODEL), row),
        out_shape=jax.ShapeDtypeStruct((n, D_MODEL), F32),
        compiler_params=pltpu.CompilerParams(
            dimension_semantics=("parallel",), vmem_limit_bytes=VMEM_LIMIT),
        name="mlp",
    )(x2, g_mlp, w_1, w_2, g_final)


def _layer(x, layer_idx, g_mix, w_in, b_f, lam_q1, lam_k1, lam_q2, lam_k2, g_subln,
           w_pa, w_pb, w_o, g_mlp, w_1, w_2, bias_near, g_final, final_norm):
    batch, seq, d = x.shape
    n = batch * seq
    x2 = x.reshape(n, d)
    qkv_cols = 6 * BRANCH_WIDTH
    w_qkv = w_in[:, :qkv_cols].astype(BF16)
    pad = LANES - DECAY_PARTS * FOX_HEADS
    w_fl = jnp.pad(jnp.tile(w_in[:, qkv_cols:qkv_cols + FOX_HEADS], (1, DECAY_PARTS)),
                   ((0, 0), (0, pad))).astype(BF16)
    b_fl = jnp.pad(jnp.tile(b_f, DECAY_PARTS), (0, pad)).reshape(1, LANES)
    w_gate = w_in[:, qkv_cols + FOX_HEADS:].astype(BF16)
    g_mix2 = g_mix.reshape(1, d)

    qd, kd, vd, qf, kf, vf, decay = _in_proj(x2, g_mix2, w_qkv, w_fl, b_fl, seq)

    lam_vecs = jnp.stack([lam_q1, lam_k1, lam_q2, lam_k2]).astype(F32)
    shape3 = (batch, seq, BRANCH_WIDTH)
    od, of = _attention(lam_vecs, g_subln.reshape(1, LANES), bias_near,
                        qd.reshape(shape3), kd.reshape(shape3), vd.reshape(shape3),
                        qf.reshape(shape3), kf.reshape(shape3), vf.reshape(shape3),
                        decay.reshape(batch, seq, LANES), _lambda_init(layer_idx))

    x1 = _merge(x2, od.reshape(n, BRANCH_WIDTH), of.reshape(n, BRANCH_WIDTH), g_mix2,
                w_gate, w_pa.astype(BF16), w_pb.astype(BF16), w_o.astype(BF16))
    y = _mlp(x1, g_mlp.reshape(1, d), w_1.astype(BF16), w_2.astype(BF16), g_final, final_norm)
    return y.reshape(batch, seq, d)


def kernel(x, g_mix, w_in, b_f, lam_q1, lam_k1, lam_q2, lam_k2, g_subln, w_pa, w_pb, w_o,
           g_mlp, w_1, w_2, rel_table, g_final):
    depth = g_mix.shape[0]
    bias_near = _bias_tiles(rel_table, x.shape[1])
    for l in range(depth):
        x = _layer(x, l, g_mix[l], w_in[l], b_f[l], lam_q1[l], lam_k1[l], lam_q2[l], lam_k2[l],
                   g_subln[l], w_pa[l], w_pb[l], w_o[l], g_mlp[l], w_1[l], w_2[l],
                   bias_near, g_final.reshape(1, -1), l == depth - 1)
    return x
```

```python
import functools
import math

import numpy as np
import jax
import jax.numpy as jnp
from jax import lax
from jax.experimental import pallas as pl
from jax.experimental.pallas import tpu as pltpu

D_MODEL = 1024
CHUNK = 64
HEAD_DIM = 64
DIFF_HEADS = 4
FOX_HEADS = 8
BRANCH_WIDTH = 512
D_FF = 4 * D_MODEL
REL_BUCKETS = 32
REL_MAX_DIST = 128
EPS = 1e-6
LANES = 128
MASKED_BUCKET = REL_BUCKETS

TQ = 256
TK = 256
TM_PROJ = 512
VMEM_LIMIT = 56 * 1024 * 1024

LOG2E = math.log2(math.e)
Q_SCALE = HEAD_DIM ** -0.5 * LOG2E

F32 = jnp.float32
BF16 = jnp.bfloat16
NT_DIMS = (((1,), (1,)), ((), ()))
TN_DIMS = (((0,), (0,)), ((), ()))


def _lambda_init(layer_idx):
    return 0.8 - 0.6 * math.exp(-0.3 * layer_idx)


def _rms(xf, g):
    return xf * lax.rsqrt(jnp.mean(xf * xf, axis=-1, keepdims=True) + EPS) * g


DECAY_PARTS = 3


def _in_proj_kernel(tiles_per_seq, x_ref, g_ref, w_ref, wfl_ref, bf_ref,
                    qd_ref, kd_ref, vd_ref, qf_ref, kf_ref, vf_ref, dec_ref, carry_ref):
    h = _rms(x_ref[...], g_ref[...]).astype(BF16)

    @pl.when(pl.program_id(0) % tiles_per_seq == 0)
    def _():
        carry_ref[...] = jnp.zeros_like(carry_ref)

    z = jnp.dot(h, wfl_ref[...], preferred_element_type=F32) + bf_ref[...]
    acc = jnp.minimum(z, 0.0) - jnp.log1p(jnp.exp(-jnp.abs(z)))
    rows = acc.shape[0]
    row = lax.broadcasted_iota(jnp.int32, acc.shape, 0)
    d = 1
    while d < rows:
        acc = acc + jnp.where(row >= d, pltpu.roll(acc, d, axis=0), 0.0)
        d *= 2
    acc = acc + carry_ref[...]
    carry_ref[...] = acc[rows - 1:rows, :]
    neg = acc * -LOG2E
    hi = neg.astype(BF16).astype(F32)
    mid = (neg - hi).astype(BF16).astype(F32)
    lo = neg - hi - mid
    lane = lax.broadcasted_iota(jnp.int32, acc.shape, 1)
    piece = jnp.where(lane < FOX_HEADS, hi, jnp.where(lane < 2 * FOX_HEADS, mid, lo))
    dec_ref[...] = jnp.where(lane < DECAY_PARTS * FOX_HEADS, piece, 0.0).astype(BF16)

    outs = (qd_ref, kd_ref, vd_ref, qf_ref, kf_ref, vf_ref)
    for c, o_ref in enumerate(outs):
        w = w_ref[:, c * BRANCH_WIDTH:(c + 1) * BRANCH_WIDTH]
        o = jnp.dot(h, w, preferred_element_type=F32)
        if o_ref is qd_ref or o_ref is qf_ref:
            o = o * Q_SCALE
        o_ref[...] = o.astype(BF16)


def _in_proj(x2, g_mix, w_qkv, w_fl, b_f, seq):
    n = x2.shape[0]
    tm = TM_PROJ
    const = lambda i: (0, 0)
    row = lambda i: (i, 0)
    out_bf = jax.ShapeDtypeStruct((n, BRANCH_WIDTH), BF16)
    return pl.pallas_call(
        functools.partial(_in_proj_kernel, seq // tm),
        grid=(n // tm,),
        in_specs=[
            pl.BlockSpec((tm, D_MODEL), row),
            pl.BlockSpec((1, D_MODEL), const),
            pl.BlockSpec(w_qkv.shape, const),
            pl.BlockSpec(w_fl.shape, const),
            pl.BlockSpec((1, LANES), const),
        ],
        out_specs=[pl.BlockSpec((tm, BRANCH_WIDTH), row)] * 6
        + [pl.BlockSpec((tm, LANES), row)],
        out_shape=[out_bf] * 6 + [jax.ShapeDtypeStruct((n, LANES), BF16)],
        scratch_shapes=[pltpu.VMEM((1, LANES), F32)],
        compiler_params=pltpu.CompilerParams(
            dimension_semantics=("arbitrary",), vmem_limit_bytes=VMEM_LIMIT),
        name="in_proj",
    )(x2, g_mix, w_qkv, w_fl, b_f)


def _rel_bucket_np(rel):
    nb = REL_BUCKETS // 2
    ret = np.where(rel > 0, nb, 0)
    n = np.abs(rel)
    max_exact = nb // 2
    nf = np.maximum(n, 1).astype(np.float64)
    large = max_exact + (np.log(nf / max_exact) / math.log(REL_MAX_DIST / max_exact)
                         * (nb - max_exact)).astype(np.int32)
    large = np.minimum(large, nb - 1)
    return (ret + np.where(n < max_exact, n, large)).astype(np.int32)


def _bias_index_maps(seq):
    kk = np.arange(TK, dtype=np.int64)[:, None]
    qq = np.arange(TQ, dtype=np.int64)[None, :]
    diag = _rel_bucket_np(kk - qq)
    diag = np.where(kk // CHUNK <= qq // CHUNK, diag, MASKED_BUCKET).astype(np.int32)
    prev = _rel_bucket_np(kk - TK - qq)
    far = _rel_bucket_np(np.arange(-seq, -TK, dtype=np.int64))
    far_bucket = int(far[0])
    assert (far == far_bucket).all(), "keys two tiles back must share one bucket"
    return diag, prev, far_bucket


def _bias_kernel(far_bucket, tab_ref, idx_ref, out_ref):
    for h in range(DIFF_HEADS):
        far = tab_ref[far_bucket, h]
        for t in range(idx_ref.shape[0]):
            idx = idx_ref[t]
            acc = jnp.full(idx.shape, -jnp.inf, F32)
            for b in range(REL_BUCKETS):
                acc = jnp.where(idx == b, (tab_ref[b, h] - far) * LOG2E, acc)
            out_ref[t, h] = acc


def _bias_tiles(rel_table, seq):
    diag, prev, far_bucket = _bias_index_maps(seq)
    idx = np.stack([diag, prev])
    vmem = pl.BlockSpec(memory_space=pltpu.VMEM)
    return pl.pallas_call(
        functools.partial(_bias_kernel, far_bucket),
        in_specs=[pl.BlockSpec(memory_space=pltpu.SMEM), vmem],
        out_specs=vmem,
        out_shape=jax.ShapeDtypeStruct((idx.shape[0], DIFF_HEADS, TK, TQ), F32),
        name="bias_tiles",
    )(rel_table, jnp.asarray(idx))


DIFF_CHAINS = 2 * DIFF_HEADS
CHAINS = DIFF_CHAINS + FOX_HEADS
QK_AHEAD = 8


def _attn_kernel(lam_init, lam_ref, gsub_ref, bnear_ref, causal_ref,
                 qd_ref, kd_ref, vd_ref, qf_ref, kf_ref, vf_ref, dec_ref,
                 od_ref, of_ref, qm_ref, m_ref, l_ref, accd_ref, accf_ref):
    i = pl.program_id(1)
    lane = lax.broadcasted_iota(jnp.int32, (TQ, LANES), 1)
    half = (lane < HEAD_DIM, lane >= HEAD_DIM)
    lam_v = lam_ref[...]
    lam = (jnp.exp(jnp.sum(lam_v[0:1] * lam_v[1:2], axis=-1, keepdims=True))
           - jnp.exp(jnp.sum(lam_v[2:3] * lam_v[3:4], axis=-1, keepdims=True))
           + lam_init)

    def is_fox(c):
        return c >= DIFF_CHAINS

    def cols(c):
        blk = (c % DIFF_CHAINS) // 2
        return slice(blk * LANES, (blk + 1) * LANES)

    for c in range(CHAINS):
        q = (qf_ref if is_fox(c) else qd_ref)[0, :, cols(c)]
        qm_ref[c, :, :LANES] = jnp.where(half[c % 2], q, jnp.zeros_like(q))
        if is_fox(c):
            head = c - DIFF_CHAINS
            pick = (lane < DECAY_PARTS * FOX_HEADS) & (lane % FOX_HEADS == head)
            qm_ref[c, :, LANES:] = jnp.where(pick, 1.0, 0.0).astype(BF16)

    def run(seq):
        def scores(start, c, add, first):
            if is_fox(c):
                k_t = jnp.concatenate([kf_ref[0, pl.ds(start, TK), cols(c)],
                                       dec_ref[0, pl.ds(start, TK), :]], axis=1)
                q_t = qm_ref[c]
            else:
                k_t = kd_ref[0, pl.ds(start, TK), cols(c)]
                q_t = qm_ref[c, :, :LANES]
            s = lax.dot_general(k_t, q_t, NT_DIMS, preferred_element_type=F32)
            return s if add is None else s + add()

        pending = {g: scores(*seq[g]) for g in range(min(QK_AHEAD, len(seq)))}
        for g, (start, c, _, first) in enumerate(seq):
            s = pending.pop(g)
            m_new = jnp.max(s, axis=0, keepdims=True)
            if not first:
                m_old = m_ref[c]
                m_new = jnp.maximum(m_old, m_new)
                alpha = jnp.exp2(m_old - m_new)
            p = jnp.exp2(s - m_new)
            l_new = jnp.sum(p, axis=0, keepdims=True)
            v_t = (vf_ref if is_fox(c) else vd_ref)[0, pl.ds(start, TK), cols(c)]
            pv = lax.dot_general(v_t, p.astype(BF16), TN_DIMS, preferred_element_type=F32)
            if g + QK_AHEAD < len(seq):
                pending[g + QK_AHEAD] = scores(*seq[g + QK_AHEAD])
            if is_fox(c):
                h = c - DIFF_CHAINS
                acc, pv = accf_ref.at[h], pv[(h % 2) * HEAD_DIM:(h % 2 + 1) * HEAD_DIM]
            else:
                acc = accd_ref.at[c]
            if not first:
                l_new = alpha * l_ref[c] + l_new
                pv = alpha * acc[...] + pv
            m_ref[c] = m_new
            l_ref[c] = l_new
            acc[...] = pv

    def tile(t, kind, first=False):
        start = pl.multiple_of((i - t) * TK, TK)
        def add(c):
            if kind == "diag":
                return (lambda: causal_ref[...]) if is_fox(c) else (lambda: bnear_ref[0, c // 2])
            if kind == "prev" and not is_fox(c):
                return lambda: bnear_ref[1, c // 2]
            return None
        return [(start, c, add(c), first) for c in range(CHAINS)]

    odd_count = i % 2 == 0
    pl.when(odd_count)(lambda: run(tile(0, "diag", first=True)))
    pl.when(jnp.logical_not(odd_count))(
        lambda: run(tile(0, "diag", first=True) + tile(1, "prev")))

    def body(r, carry):
        t = 1 + i % 2 + 2 * r
        prev_here = jnp.logical_and(odd_count, r == 0)
        pl.when(prev_here)(lambda: run(tile(t, "prev") + tile(t + 1, "far")))
        pl.when(jnp.logical_not(prev_here))(lambda: run(tile(t, "far") + tile(t + 1, "far")))
        return carry

    lax.fori_loop(0, i // 2, body, 0)

    inv_l = [1.0 / l_ref[c] for c in range(CHAINS)]
    for h in range(DIFF_HEADS):
        o = (accd_ref[2 * h] * inv_l[2 * h]
             - lam * (accd_ref[2 * h + 1] * inv_l[2 * h + 1])).T
        o = _rms(o, gsub_ref[...]) * (1.0 - lam_init)
        od_ref[0, :, h * LANES:(h + 1) * LANES] = o.astype(BF16)
    for pair in range(FOX_HEADS // 2):
        o = jnp.concatenate([accf_ref[2 * pair + e] * inv_l[DIFF_CHAINS + 2 * pair + e]
                             for e in range(2)], axis=0).T
        of_ref[0, :, pair * LANES:(pair + 1) * LANES] = o.astype(BF16)


def _attention(lam_vecs, g_subln, bias_near, qd, kd, vd, qf, kf, vf, decay, lam_init):
    batch, seq, _ = qd.shape
    const2 = lambda b, i: (0, 0)
    const4 = lambda b, i: (0, 0, 0, 0)
    kk = np.arange(TK)[:, None]
    qq = np.arange(TQ)[None, :]
    causal = jnp.asarray(np.where(kk <= qq, 0.0, -np.inf).astype(np.float32))
    qblk = pl.BlockSpec((1, TQ, BRANCH_WIDTH), lambda b, i: (b, i, 0))
    full = pl.BlockSpec((1, seq, BRANCH_WIDTH), lambda b, i: (b, 0, 0))
    out = jax.ShapeDtypeStruct((batch, seq, BRANCH_WIDTH), BF16)
    return pl.pallas_call(
        functools.partial(_attn_kernel, lam_init),
        grid=(batch, seq // TQ),
        in_specs=[
            pl.BlockSpec(lam_vecs.shape, const2),
            pl.BlockSpec(g_subln.shape, const2),
            pl.BlockSpec(bias_near.shape, const4),
            pl.BlockSpec(causal.shape, const2),
            qblk, full, full, qblk, full, full,
            pl.BlockSpec((1, seq, LANES), lambda b, i: (b, 0, 0)),
        ],
        out_specs=[qblk, qblk],
        out_shape=[out, out],
        scratch_shapes=[
            pltpu.VMEM((CHAINS, TQ, 2 * LANES), BF16),
            pltpu.VMEM((CHAINS, 1, TQ), F32),
            pltpu.VMEM((CHAINS, 1, TQ), F32),
            pltpu.VMEM((DIFF_CHAINS, LANES, TQ), F32),
            pltpu.VMEM((FOX_HEADS, HEAD_DIM, TQ), F32),
        ],
        compiler_params=pltpu.CompilerParams(
            dimension_semantics=("parallel", "arbitrary"), vmem_limit_bytes=VMEM_LIMIT),
        name="attention",
    )(lam_vecs, g_subln, bias_near, causal, qd, kd, vd, qf, kf, vf, decay)


def _merge_kernel(x_ref, od_ref, of_ref, g_ref, wg_ref, wpa_ref, wpb_ref, wo_ref, y_ref):
    x = x_ref[...]
    h = _rms(x, g_ref[...]).astype(BF16)
    a = jnp.dot(od_ref[...], wpa_ref[...], preferred_element_type=F32)
    b = jnp.dot(of_ref[...], wpb_ref[...], preferred_element_type=F32)
    ga = jax.nn.sigmoid(jnp.dot(h, wg_ref[:, :D_MODEL], preferred_element_type=F32))
    merged = ga * a
    gb = jax.nn.sigmoid(jnp.dot(h, wg_ref[:, D_MODEL:], preferred_element_type=F32))
    merged = (merged + gb * b).astype(BF16)
    y_ref[...] = x + jnp.dot(merged, wo_ref[...], preferred_element_type=F32)


def _merge(x2, od, of, g_mix, w_gate, w_pa, w_pb, w_o):
    n = x2.shape[0]
    tm = TM_PROJ
    const = lambda i: (0, 0)
    row = lambda i: (i, 0)
    return pl.pallas_call(
        _merge_kernel,
        grid=(n // tm,),
        in_specs=[
            pl.BlockSpec((tm, D_MODEL), row),
            pl.BlockSpec((tm, BRANCH_WIDTH), row),
            pl.BlockSpec((tm, BRANCH_WIDTH), row),
            pl.BlockSpec((1, D_MODEL), const),
            pl.BlockSpec(w_gate.shape, const),
            pl.BlockSpec(w_pa.shape, const),
            pl.BlockSpec(w_pb.shape, const),
            pl.BlockSpec(w_o.shape, const),
        ],
        out_specs=pl.BlockSpec((tm, D_MODEL), row),
        out_shape=jax.ShapeDtypeStruct((n, D_MODEL), F32),
        compiler_params=pltpu.CompilerParams(
            dimension_semantics=("parallel",), vmem_limit_bytes=VMEM_LIMIT),
        name="merge",
    )(x2, od, of, g_mix, w_gate, w_pa, w_pb, w_o)


FF_CHUNK = 1024


def _mlp_kernel(final_norm, x_ref, g_ref, w1_ref, w2_ref, gf_ref, y_ref):
    x = x_ref[...]
    h = _rms(x, g_ref[...]).astype(BF16)
    y = x
    for c in range(D_FF // FF_CHUNK):
        cols = slice(c * FF_CHUNK, (c + 1) * FF_CHUNK)
        u = jnp.maximum(jnp.dot(h, w1_ref[:, cols], preferred_element_type=F32), 0.0)
        y = y + jnp.dot((u * u).astype(BF16), w2_ref[cols, :], preferred_element_type=F32)
    y_ref[...] = _rms(y, gf_ref[...]) if final_norm else y


def _mlp(x2, g_mlp, w_1, w_2, g_final, final_norm):
    n = x2.shape[0]
    tm = TM_PROJ
    const = lambda i: (0, 0)
    row = lambda i: (i, 0)
    single = pl.Buffered(1)
    return pl.pallas_call(
        functools.partial(_mlp_kernel, final_norm),
        grid=(n // tm,),
        in_specs=[
            pl.BlockSpec((tm, D_MODEL), row),
            pl.BlockSpec((1, D_MODEL), const),
            pl.BlockSpec(w_1.shape, const, pipeline_mode=single),
            pl.BlockSpec(w_2.shape, const, pipeline_mode=single),
            pl.BlockSpec((1, D_MODEL), const),
        ],
        out_specs=pl.BlockSpec((tm, D_MODEL), row),
        out_shape=jax.ShapeDtypeStruct((n, D_MODEL), F32),
        compiler_params=pltpu.CompilerParams(
            dimension_semantics=("parallel",), vmem_limit_bytes=VMEM_LIMIT),
        name="mlp",
    )(x2, g_mlp, w_1, w_2, g_final)


def _layer(x, layer_idx, g_mix, w_in, b_f, lam_q1, lam_k1, lam_q2, lam_k2, g_subln,
           w_pa, w_pb, w_o, g_mlp, w_1, w_2, bias_near, g_final, final_norm):
    batch, seq, d = x.shape
    n = batch * seq
    x2 = x.reshape(n, d)
    qkv_cols = 6 * BRANCH_WIDTH
    w_qkv = w_in[:, :qkv_cols].astype(BF16)
    pad = LANES - DECAY_PARTS * FOX_HEADS
    w_fl = jnp.pad(jnp.tile(w_in[:, qkv_cols:qkv_cols + FOX_HEADS], (1, DECAY_PARTS)),
                   ((0, 0), (0, pad))).astype(BF16)
    b_fl = jnp.pad(jnp.tile(b_f, DECAY_PARTS), (0, pad)).reshape(1, LANES)
    w_gate = w_in[:, qkv_cols + FOX_HEADS:].astype(BF16)
    g_mix2 = g_mix.reshape(1, d)

    qd, kd, vd, qf, kf, vf, decay = _in_proj(x2, g_mix2, w_qkv, w_fl, b_fl, seq)

    lam_vecs = jnp.stack([lam_q1, lam_k1, lam_q2, lam_k2]).astype(F32)
    shape3 = (batch, seq, BRANCH_WIDTH)
    od, of = _attention(lam_vecs, g_subln.reshape(1, LANES), bias_near,
                        qd.reshape(shape3), kd.reshape(shape3), vd.reshape(shape3),
                        qf.reshape(shape3), kf.reshape(shape3), vf.reshape(shape3),
                        decay.reshape(batch, seq, LANES), _lambda_init(layer_idx))

    x1 = _merge(x2, od.reshape(n, BRANCH_WIDTH), of.reshape(n, BRANCH_WIDTH), g_mix2,
                w_gate, w_pa.astype(BF16), w_pb.astype(BF16), w_o.astype(BF16))
    y = _mlp(x1, g_mlp.reshape(1, d), w_1.astype(BF16), w_2.astype(BF16), g_final, final_norm)
    return y.reshape(batch, seq, d)


def kernel(x, g_mix, w_in, b_f, lam_q1, lam_k1, lam_q2, lam_k2, g_subln, w_pa, w_pb, w_o,
           g_mlp, w_1, w_2, rel_table, g_final):
    depth = g_mix.shape[0]
    bias_near = _bias_tiles(rel_table, x.shape[1])
    for l in range(depth):
        x = _layer(x, l, g_mix[l], w_in[l], b_f[l], lam_q1[l], lam_k1[l], lam_q2[l], lam_k2[l],
                   g_subln[l], w_pa[l], w_pb[l], w_o[l], g_mlp[l], w_1[l], w_2[l],
                   bias_near, g_final.reshape(1, -1), l == depth - 1)
    return x
```

```python
import functools
import math

import numpy as np
import jax
import jax.numpy as jnp
from jax import lax
from jax.experimental import pallas as pl
from jax.experimental.pallas import tpu as pltpu

D_MODEL = 1024
CHUNK = 64
HEAD_DIM = 64
DIFF_HEADS = 4
FOX_HEADS = 8
BRANCH_WIDTH = 512
D_FF = 4 * D_MODEL
REL_BUCKETS = 32
REL_MAX_DIST = 128
EPS = 1e-6
LANES = 128
MASKED_BUCKET = REL_BUCKETS

TQ = 256
TK = 256
TM_PROJ = 1024
VMEM_LIMIT = 56 * 1024 * 1024

LOG2E = math.log2(math.e)
Q_SCALE = HEAD_DIM ** -0.5 * LOG2E

F32 = jnp.float32
BF16 = jnp.bfloat16
NT_DIMS = (((1,), (1,)), ((), ()))
TN_DIMS = (((0,), (0,)), ((), ()))


def _lambda_init(layer_idx):
    return 0.8 - 0.6 * math.exp(-0.3 * layer_idx)


def _rms(xf, g):
    return xf * lax.rsqrt(jnp.mean(xf * xf, axis=-1, keepdims=True) + EPS) * g


DECAY_PARTS = 3


def _in_proj_kernel(tiles_per_seq, x_ref, g_ref, w_ref, wfl_ref, bf_ref,
                    qd_ref, kd_ref, vd_ref, qf_ref, kf_ref, vf_ref, dec_ref, carry_ref):
    h = _rms(x_ref[...], g_ref[...]).astype(BF16)

    @pl.when(pl.program_id(0) % tiles_per_seq == 0)
    def _():
        carry_ref[...] = jnp.zeros_like(carry_ref)

    z = jnp.dot(h, wfl_ref[...], preferred_element_type=F32) + bf_ref[...]
    acc = jnp.minimum(z, 0.0) - jnp.log1p(jnp.exp(-jnp.abs(z)))
    rows = acc.shape[0]
    row = lax.broadcasted_iota(jnp.int32, acc.shape, 0)
    d = 1
    while d < rows:
        acc = acc + jnp.where(row >= d, pltpu.roll(acc, d, axis=0), 0.0)
        d *= 2
    acc = acc + carry_ref[...]
    carry_ref[...] = acc[rows - 1:rows, :]
    neg = acc * -LOG2E
    hi = neg.astype(BF16).astype(F32)
    mid = (neg - hi).astype(BF16).astype(F32)
    lo = neg - hi - mid
    lane = lax.broadcasted_iota(jnp.int32, acc.shape, 1)
    piece = jnp.where(lane < FOX_HEADS, hi, jnp.where(lane < 2 * FOX_HEADS, mid, lo))
    dec_ref[...] = jnp.where(lane < DECAY_PARTS * FOX_HEADS, piece, 0.0).astype(BF16)

    outs = (qd_ref, kd_ref, vd_ref, qf_ref, kf_ref, vf_ref)
    for c, o_ref in enumerate(outs):
        w = w_ref[:, c * BRANCH_WIDTH:(c + 1) * BRANCH_WIDTH]
        o = jnp.dot(h, w, preferred_element_type=F32)
        if o_ref is qd_ref or o_ref is qf_ref:
            o = (o * Q_SCALE).astype(BF16)
            low = lax.broadcasted_iota(jnp.int32, o.shape, 1) % LANES < HEAD_DIM
            o_ref[0] = jnp.where(low, o, jnp.zeros_like(o))
            o_ref[1] = jnp.where(low, jnp.zeros_like(o), o)
        else:
            o_ref[...] = o.astype(BF16)


def _in_proj(x2, g_mix, w_qkv, w_fl, b_f, seq):
    n = x2.shape[0]
    tm = TM_PROJ
    const = lambda i: (0, 0)
    row = lambda i: (i, 0)
    out_bf = jax.ShapeDtypeStruct((n, BRANCH_WIDTH), BF16)
    out_q = jax.ShapeDtypeStruct((2, n, BRANCH_WIDTH), BF16)
    spec_bf = pl.BlockSpec((tm, BRANCH_WIDTH), row)
    spec_q = pl.BlockSpec((2, tm, BRANCH_WIDTH), lambda i: (0, i, 0))
    return pl.pallas_call(
        functools.partial(_in_proj_kernel, seq // tm),
        grid=(n // tm,),
        in_specs=[
            pl.BlockSpec((tm, D_MODEL), row),
            pl.BlockSpec((1, D_MODEL), const),
            pl.BlockSpec(w_qkv.shape, const),
            pl.BlockSpec(w_fl.shape, const),
            pl.BlockSpec((1, LANES), const),
        ],
        out_specs=[spec_q, spec_bf, spec_bf, spec_q, spec_bf, spec_bf,
                   pl.BlockSpec((tm, LANES), row)],
        out_shape=[out_q, out_bf, out_bf, out_q, out_bf, out_bf,
                   jax.ShapeDtypeStruct((n, LANES), BF16)],
        scratch_shapes=[pltpu.VMEM((1, LANES), F32)],
        compiler_params=pltpu.CompilerParams(
            dimension_semantics=("arbitrary",), vmem_limit_bytes=VMEM_LIMIT),
        name="in_proj",
    )(x2, g_mix, w_qkv, w_fl, b_f)


def _rel_bucket_np(rel):
    nb = REL_BUCKETS // 2
    ret = np.where(rel > 0, nb, 0)
    n = np.abs(rel)
    max_exact = nb // 2
    nf = np.maximum(n, 1).astype(np.float64)
    large = max_exact + (np.log(nf / max_exact) / math.log(REL_MAX_DIST / max_exact)
                         * (nb - max_exact)).astype(np.int32)
    large = np.minimum(large, nb - 1)
    return (ret + np.where(n < max_exact, n, large)).astype(np.int32)


def _bias_index_maps(seq):
    kk = np.arange(TK, dtype=np.int64)[:, None]
    qq = np.arange(TQ, dtype=np.int64)[None, :]
    diag = _rel_bucket_np(kk - qq)
    diag = np.where(kk // CHUNK <= qq // CHUNK, diag, MASKED_BUCKET).astype(np.int32)
    prev = _rel_bucket_np(kk - TK - qq)
    far = _rel_bucket_np(np.arange(-seq, -TK, dtype=np.int64))
    far_bucket = int(far[0])
    assert (far == far_bucket).all(), "keys two tiles back must share one bucket"
    return diag, prev, far_bucket


def _bias_kernel(far_bucket, tab_ref, idx_ref, out_ref):
    for h in range(DIFF_HEADS):
        far = tab_ref[far_bucket, h]
        for t in range(idx_ref.shape[0]):
            idx = idx_ref[t]
            acc = jnp.full(idx.shape, -jnp.inf, F32)
            for b in range(REL_BUCKETS):
                acc = jnp.where(idx == b, (tab_ref[b, h] - far) * LOG2E, acc)
            out_ref[t, h] = acc


def _bias_tiles(rel_table, seq):
    diag, prev, far_bucket = _bias_index_maps(seq)
    idx = np.stack([diag, prev])
    vmem = pl.BlockSpec(memory_space=pltpu.VMEM)
    return pl.pallas_call(
        functools.partial(_bias_kernel, far_bucket),
        in_specs=[pl.BlockSpec(memory_space=pltpu.SMEM), vmem],
        out_specs=vmem,
        out_shape=jax.ShapeDtypeStruct((idx.shape[0], DIFF_HEADS, TK, TQ), F32),
        name="bias_tiles",
    )(rel_table, jnp.asarray(idx))


DIFF_CHAINS = 2 * DIFF_HEADS
CHAINS = DIFF_CHAINS + FOX_HEADS
QK_AHEAD = 8
Q_BLOCKS = 2


def _attn_kernel(lam_init, lam_ref, gsub_ref, bnear_ref, causal_ref, pick_ref,
                 qd_ref, kd_ref, vd_ref, qf_ref, kf_ref, vf_ref, dec_ref,
                 od_ref, of_ref, m_ref, l_ref, accd_ref, accf_ref):
    g = pl.program_id(1)
    lam_v = lam_ref[...]
    lam = (jnp.exp(jnp.sum(lam_v[0:1] * lam_v[1:2], axis=-1, keepdims=True))
           - jnp.exp(jnp.sum(lam_v[2:3] * lam_v[3:4], axis=-1, keepdims=True))
           + lam_init)

    def is_fox(c):
        return c >= DIFF_CHAINS

    def cols(c):
        blk = (c % DIFF_CHAINS) // 2
        return slice(blk * LANES, (blk + 1) * LANES)

    def rows(qb):
        return slice(qb * TQ, (qb + 1) * TQ)

    def run(seq):
        tiles = {}

        def load(kind, tile, c):
            key = (kind, id(tile), is_fox(c), cols(c).start)
            if key not in tiles:
                at = pl.ds(pl.multiple_of(tile * TK, TK), TK)
                if kind == "v":
                    tiles[key] = (vf_ref if is_fox(c) else vd_ref)[0, at, cols(c)]
                elif is_fox(c):
                    tiles[key] = jnp.concatenate([kf_ref[0, at, cols(c)], dec_ref[0, at, :]],
                                                 axis=1)
                else:
                    tiles[key] = kd_ref[0, at, cols(c)]
            return tiles[key]

        def scores(tile, qb, c, add, first):
            if is_fox(c):
                q_t = jnp.concatenate([qf_ref[c % 2, 0, rows(qb), cols(c)],
                                       pick_ref[c - DIFF_CHAINS]], axis=1)
            else:
                q_t = qd_ref[c % 2, 0, rows(qb), cols(c)]
            s = lax.dot_general(load("k", tile, c), q_t, NT_DIMS, preferred_element_type=F32)
            return s if add is None else s + add()

        pending = {j: scores(*seq[j]) for j in range(min(QK_AHEAD, len(seq)))}
        for j, (tile, qb, c, _, first) in enumerate(seq):
            s = pending.pop(j)
            state = qb * CHAINS + c
            m_new = jnp.max(s, axis=0, keepdims=True)
            if not first:
                m_old = m_ref[state]
                m_new = jnp.maximum(m_old, m_new)
                alpha = jnp.exp2(m_old - m_new)
            p = jnp.exp2(s - m_new)
            l_new = jnp.sum(p, axis=0, keepdims=True)
            pv = lax.dot_general(load("v", tile, c), p.astype(BF16), TN_DIMS,
                                 preferred_element_type=F32)
            if j + QK_AHEAD < len(seq):
                pending[j + QK_AHEAD] = scores(*seq[j + QK_AHEAD])
            if is_fox(c):
                h = c - DIFF_CHAINS
                acc = accf_ref.at[qb * FOX_HEADS + h]
                pv = pv[(h % 2) * HEAD_DIM:(h % 2 + 1) * HEAD_DIM]
            else:
                acc = accd_ref.at[qb * DIFF_CHAINS + c]
            if not first:
                l_new = alpha * l_ref[state] + l_new
                pv = alpha * acc[...] + pv
            m_ref[state] = m_new
            l_ref[state] = l_new
            acc[...] = pv

    def bias(kind, c):
        if kind == "diag":
            return (lambda: causal_ref[...]) if is_fox(c) else (lambda: bnear_ref[0, c // 2])
        if kind == "prev" and not is_fox(c):
            return lambda: bnear_ref[1, c // 2]
        return None

    def steps(tile, kinds, first=False):
        return [(tile, qb, c0 + e, bias(kind, c0 + e), first)
                for c0 in range(0, CHAINS, 2) for qb, kind in kinds.items() for e in range(2)]

    diag_a = 2 * g
    diag_b = diag_a + 1
    run(steps(diag_b, {1: "diag"}, first=True)
        + steps(diag_a, {0: "diag"}, first=True) + steps(diag_a, {1: "prev"}))

    def body(r, carry):
        hi = diag_a - 1 - 2 * r
        lo = hi - 1
        pl.when(r == 0)(lambda: run(steps(hi, {0: "prev", 1: "far"})
                                    + steps(lo, {0: "far", 1: "far"})))
        pl.when(r > 0)(lambda: run(steps(hi, {0: "far", 1: "far"})
                                   + steps(lo, {0: "far", 1: "far"})))
        return carry

    lax.fori_loop(0, g, body, 0)

    for qb in range(Q_BLOCKS):
        inv_l = [1.0 / l_ref[qb * CHAINS + c] for c in range(CHAINS)]
        for h in range(DIFF_HEADS):
            acc = [accd_ref[qb * DIFF_CHAINS + 2 * h + e] * inv_l[2 * h + e] for e in range(2)]
            o = (acc[0] - lam * acc[1]).T
            o = _rms(o, gsub_ref[...]) * (1.0 - lam_init)
            od_ref[0, rows(qb), h * LANES:(h + 1) * LANES] = o.astype(BF16)
        for pair in range(FOX_HEADS // 2):
            o = jnp.concatenate(
                [accf_ref[qb * FOX_HEADS + 2 * pair + e] * inv_l[DIFF_CHAINS + 2 * pair + e]
                 for e in range(2)], axis=0).T
            of_ref[0, rows(qb), pair * LANES:(pair + 1) * LANES] = o.astype(BF16)


def _attention(lam_vecs, g_subln, bias_near, qd, kd, vd, qf, kf, vf, decay, lam_init):
    batch, seq, _ = kd.shape
    const2 = lambda b, i: (0, 0)
    const4 = lambda b, i: (0, 0, 0, 0)
    kk = np.arange(TK)[:, None]
    qq = np.arange(TQ)[None, :]
    causal = jnp.asarray(np.where(kk <= qq, 0.0, -np.inf).astype(np.float32))
    lane = np.arange(LANES)
    pick = (lane[None] < DECAY_PARTS * FOX_HEADS) & (lane[None] % FOX_HEADS
                                                     == np.arange(FOX_HEADS)[:, None])
    pick = jnp.asarray(np.broadcast_to(pick[:, None, :], (FOX_HEADS, TQ, LANES)), BF16)
    qblk = pl.BlockSpec((1, Q_BLOCKS * TQ, BRANCH_WIDTH), lambda b, i: (b, i, 0))
    qsel = pl.BlockSpec((2, 1, Q_BLOCKS * TQ, BRANCH_WIDTH), lambda b, i: (0, b, i, 0))
    full = pl.BlockSpec((1, seq, BRANCH_WIDTH), lambda b, i: (b, 0, 0))
    out = jax.ShapeDtypeStruct((batch, seq, BRANCH_WIDTH), BF16)
    return pl.pallas_call(
        functools.partial(_attn_kernel, lam_init),
        grid=(batch, seq // (Q_BLOCKS * TQ)),
        in_specs=[
            pl.BlockSpec(lam_vecs.shape, const2),
            pl.BlockSpec(g_subln.shape, const2),
            pl.BlockSpec(bias_near.shape, const4),
            pl.BlockSpec(causal.shape, const2),
            pl.BlockSpec(pick.shape, lambda b, i: (0, 0, 0)),
            qsel, full, full, qsel, full, full,
            pl.BlockSpec((1, seq, LANES), lambda b, i: (b, 0, 0)),
        ],
        out_specs=[qblk, qblk],
        out_shape=[out, out],
        scratch_shapes=[
            pltpu.VMEM((Q_BLOCKS * CHAINS, 1, TQ), F32),
            pltpu.VMEM((Q_BLOCKS * CHAINS, 1, TQ), F32),
            pltpu.VMEM((Q_BLOCKS * DIFF_CHAINS, LANES, TQ), F32),
            pltpu.VMEM((Q_BLOCKS * FOX_HEADS, HEAD_DIM, TQ), F32),
        ],
        compiler_params=pltpu.CompilerParams(
            dimension_semantics=("parallel", "arbitrary"), vmem_limit_bytes=VMEM_LIMIT),
        name="attention",
    )(lam_vecs, g_subln, bias_near, causal, pick, qd, kd, vd, qf, kf, vf, decay)


def _merge_kernel(x_ref, od_ref, of_ref, g_ref, wg_ref, wpa_ref, wpb_ref, wo_ref, y_ref):
    x = x_ref[...]
    h = _rms(x, g_ref[...]).astype(BF16)
    a = jnp.dot(od_ref[...], wpa_ref[...], preferred_element_type=F32)
    b = jnp.dot(of_ref[...], wpb_ref[...], preferred_element_type=F32)
    ga = jax.nn.sigmoid(jnp.dot(h, wg_ref[:, :D_MODEL], preferred_element_type=F32))
    merged = ga * a
    gb = jax.nn.sigmoid(jnp.dot(h, wg_ref[:, D_MODEL:], preferred_element_type=F32))
    merged = (merged + gb * b).astype(BF16)
    y_ref[...] = x + jnp.dot(merged, wo_ref[...], preferred_element_type=F32)


def _merge(x2, od, of, g_mix, w_gate, w_pa, w_pb, w_o):
    n = x2.shape[0]
    tm = TM_PROJ
    const = lambda i: (0, 0)
    row = lambda i: (i, 0)
    return pl.pallas_call(
        _merge_kernel,
        grid=(n // tm,),
        in_specs=[
            pl.BlockSpec((tm, D_MODEL), row),
            pl.BlockSpec((tm, BRANCH_WIDTH), row),
            pl.BlockSpec((tm, BRANCH_WIDTH), row),
            pl.BlockSpec((1, D_MODEL), const),
            pl.BlockSpec(w_gate.shape, const),
            pl.BlockSpec(w_pa.shape, const),
            pl.BlockSpec(w_pb.shape, const),
            pl.BlockSpec(w_o.shape, const),
        ],
        out_specs=pl.BlockSpec((tm, D_MODEL), row),
        out_shape=jax.ShapeDtypeStruct((n, D_MODEL), F32),
        compiler_params=pltpu.CompilerParams(
            dimension_semantics=("parallel",), vmem_limit_bytes=VMEM_LIMIT),
        name="merge",
    )(x2, od, of, g_mix, w_gate, w_pa, w_pb, w_o)


FF_CHUNK = 1024


def _mlp_kernel(final_norm, x_ref, g_ref, w1_ref, w2_ref, gf_ref, y_ref):
    x = x_ref[...]
    h = _rms(x, g_ref[...]).astype(BF16)
    y = x
    for c in range(D_FF // FF_CHUNK):
        cols = slice(c * FF_CHUNK, (c + 1) * FF_CHUNK)
        u = jnp.maximum(jnp.dot(h, w1_ref[:, cols], preferred_element_type=F32), 0.0)
        y = y + jnp.dot((u * u).astype(BF16), w2_ref[cols, :], preferred_element_type=F32)
    y_ref[...] = _rms(y, gf_ref[...]) if final_norm else y


def _mlp(x2, g_mlp, w_1, w_2, g_final, final_norm):
    n = x2.shape[0]
    tm = TM_PROJ
    const = lambda i: (0, 0)
    row = lambda i: (i, 0)
    single = pl.Buffered(1)
    return pl.pallas_call(
        functools.partial(_mlp_kernel, final_norm),
        grid=(n // tm,),
        in_specs=[
            pl.BlockSpec((tm, D_MODEL), row),
            pl.BlockSpec((1, D_MODEL), const),
            pl.BlockSpec(w_1.shape, const, pipeline_mode=single),
            pl.BlockSpec(w_2.shape, const, pipeline_mode=single),
            pl.BlockSpec((1, D_MODEL), const),
        ],
        out_specs=pl.BlockSpec((tm, D_MODEL), row),
        out_shape=jax.ShapeDtypeStruct((n, D_MODEL), F32),
        compiler_params=pltpu.CompilerParams(
            dimension_semantics=("parallel",), vmem_limit_bytes=VMEM_LIMIT),
        name="mlp",
    )(x2, g_mlp, w_1, w_2, g_final)


def _layer(x, layer_idx, g_mix, w_in, b_f, lam_q1, lam_k1, lam_q2, lam_k2, g_subln,
           w_pa, w_pb, w_o, g_mlp, w_1, w_2, bias_near, g_final, final_norm):
    batch, seq, d = x.shape
    n = batch * seq
    x2 = x.reshape(n, d)
    qkv_cols = 6 * BRANCH_WIDTH
    w_qkv = w_in[:, :qkv_cols].astype(BF16)
    pad = LANES - DECAY_PARTS * FOX_HEADS
    w_fl = jnp.pad(jnp.tile(w_in[:, qkv_cols:qkv_cols + FOX_HEADS], (1, DECAY_PARTS)),
                   ((0, 0), (0, pad))).astype(BF16)
    b_fl = jnp.pad(jnp.tile(b_f, DECAY_PARTS), (0, pad)).reshape(1, LANES)
    w_gate = w_in[:, qkv_cols + FOX_HEADS:].astype(BF16)
    g_mix2 = g_mix.reshape(1, d)

    qd, kd, vd, qf, kf, vf, decay = _in_proj(x2, g_mix2, w_qkv, w_fl, b_fl, seq)

    lam_vecs = jnp.stack([lam_q1, lam_k1, lam_q2, lam_k2]).astype(F32)
    shape3 = (batch, seq, BRANCH_WIDTH)
    shape4 = (2,) + shape3
    od, of = _attention(lam_vecs, g_subln.reshape(1, LANES), bias_near,
                        qd.reshape(shape4), kd.reshape(shape3), vd.reshape(shape3),
                        qf.reshape(shape4), kf.reshape(shape3), vf.reshape(shape3),
                        decay.reshape(batch, seq, LANES), _lambda_init(layer_idx))

    x1 = _merge(x2, od.reshape(n, BRANCH_WIDTH), of.reshape(n, BRANCH_WIDTH), g_mix2,
                w_gate, w_pa.astype(BF16), w_pb.astype(BF16), w_o.astype(BF16))
    y = _mlp(x1, g_mlp.reshape(1, d), w_1.astype(BF16), w_2.astype(BF16), g_final, final_norm)
    return y.reshape(batch, seq, d)


def kernel(x, g_mix, w_in, b_f, lam_q1, lam_k1, lam_q2, lam_k2, g_subln, w_pa, w_pb, w_o,
           g_mlp, w_1, w_2, rel_table, g_final):
    depth = g_mix.shape[0]
    bias_near = _bias_tiles(rel_table, x.shape[1])
    for l in range(depth):
        x = _layer(x, l, g_mix[l], w_in[l], b_f[l], lam_q1[l], lam_k1[l], lam_q2[l], lam_k2[l],
                   g_subln[l], w_pa[l], w_pb[l], w_o[l], g_mlp[l], w_1[l], w_2[l],
                   bias_near, g_final.reshape(1, -1), l == depth - 1)
    return x
```

```python
import functools
import math

import numpy as np
import jax
import jax.numpy as jnp
from jax import lax
from jax.experimental import pallas as pl
from jax.experimental.pallas import tpu as pltpu

D_MODEL = 1024
CHUNK = 64
HEAD_DIM = 64
DIFF_HEADS = 4
FOX_HEADS = 8
BRANCH_WIDTH = 512
D_FF = 4 * D_MODEL
REL_BUCKETS = 32
REL_MAX_DIST = 128
EPS = 1e-6
LANES = 128
MASKED_BUCKET = REL_BUCKETS

TQ = 256
TK = 256
TM_IN_PROJ = 512
TM_PROJ = 1024
VMEM_LIMIT = 56 * 1024 * 1024

LOG2E = math.log2(math.e)
Q_SCALE = HEAD_DIM ** -0.5 * LOG2E

F32 = jnp.float32
BF16 = jnp.bfloat16
NT_DIMS = (((1,), (1,)), ((), ()))
TN_DIMS = (((0,), (0,)), ((), ()))


def _lambda_init(layer_idx):
    return 0.8 - 0.6 * math.exp(-0.3 * layer_idx)


def _rms(xf, g):
    return xf * lax.rsqrt(jnp.mean(xf * xf, axis=-1, keepdims=True) + EPS) * g


DECAY_PARTS = 3


def _in_proj_kernel(tiles_per_seq, x_ref, g_ref, w32_ref, wfl_ref, bf_ref,
                    qd_ref, kd_ref, vd_ref, qf_ref, kf_ref, vf_ref, dec_ref, carry_ref, w_ref):
    @pl.when(pl.program_id(0) == 0)
    def _():
        w_ref[...] = w32_ref[...].astype(BF16)

    h = _rms(x_ref[...], g_ref[...]).astype(BF16)

    @pl.when(pl.program_id(0) % tiles_per_seq == 0)
    def _():
        carry_ref[...] = jnp.zeros_like(carry_ref)

    z = jnp.dot(h, wfl_ref[...], preferred_element_type=F32) + bf_ref[...]
    acc = jnp.minimum(z, 0.0) - jnp.log1p(jnp.exp(-jnp.abs(z)))
    rows = acc.shape[0]
    row = lax.broadcasted_iota(jnp.int32, acc.shape, 0)
    d = 1
    while d < rows:
        acc = acc + jnp.where(row >= d, pltpu.roll(acc, d, axis=0), 0.0)
        d *= 2
    acc = acc + carry_ref[...]
    carry_ref[...] = acc[rows - 1:rows, :]
    neg = acc * -LOG2E
    hi = neg.astype(BF16).astype(F32)
    mid = (neg - hi).astype(BF16).astype(F32)
    lo = neg - hi - mid
    lane = lax.broadcasted_iota(jnp.int32, acc.shape, 1)
    piece = jnp.where(lane < FOX_HEADS, hi, jnp.where(lane < 2 * FOX_HEADS, mid, lo))
    dec_ref[...] = jnp.where(lane < DECAY_PARTS * FOX_HEADS, piece, 0.0).astype(BF16)

    outs = (qd_ref, kd_ref, vd_ref, qf_ref, kf_ref, vf_ref)
    for c, o_ref in enumerate(outs):
        w = w_ref[:, c * BRANCH_WIDTH:(c + 1) * BRANCH_WIDTH]
        o = jnp.dot(h, w, preferred_element_type=F32)
        if o_ref is qd_ref or o_ref is qf_ref:
            o = (o * Q_SCALE).astype(BF16)
            low = lax.broadcasted_iota(jnp.int32, o.shape, 1) % LANES < HEAD_DIM
            o_ref[0] = jnp.where(low, o, jnp.zeros_like(o))
            o_ref[1] = jnp.where(low, jnp.zeros_like(o), o)
        else:
            o_ref[...] = o.astype(BF16)


def _in_proj(x2, g_mix, w_in, w_fl, b_f, seq):
    n = x2.shape[0]
    tm = TM_IN_PROJ
    qkv_cols = 6 * BRANCH_WIDTH
    const = lambda i: (0, 0)
    row = lambda i: (i, 0)
    out_bf = jax.ShapeDtypeStruct((n, BRANCH_WIDTH), BF16)
    out_q = jax.ShapeDtypeStruct((2, n, BRANCH_WIDTH), BF16)
    spec_bf = pl.BlockSpec((tm, BRANCH_WIDTH), row)
    spec_q = pl.BlockSpec((2, tm, BRANCH_WIDTH), lambda i: (0, i, 0))
    return pl.pallas_call(
        functools.partial(_in_proj_kernel, seq // tm),
        grid=(n // tm,),
        in_specs=[
            pl.BlockSpec((tm, D_MODEL), row),
            pl.BlockSpec((1, D_MODEL), const),
            pl.BlockSpec((D_MODEL, qkv_cols), const, pipeline_mode=pl.Buffered(1)),
            pl.BlockSpec(w_fl.shape, const),
            pl.BlockSpec((1, LANES), const),
        ],
        out_specs=[spec_q, spec_bf, spec_bf, spec_q, spec_bf, spec_bf,
                   pl.BlockSpec((tm, LANES), row)],
        out_shape=[out_q, out_bf, out_bf, out_q, out_bf, out_bf,
                   jax.ShapeDtypeStruct((n, LANES), BF16)],
        scratch_shapes=[pltpu.VMEM((1, LANES), F32),
                        pltpu.VMEM((D_MODEL, qkv_cols), BF16)],
        compiler_params=pltpu.CompilerParams(
            dimension_semantics=("arbitrary",), vmem_limit_bytes=VMEM_LIMIT),
        name="in_proj",
    )(x2, g_mix, w_in, w_fl, b_f)


def _rel_bucket_np(rel):
    nb = REL_BUCKETS // 2
    ret = np.where(rel > 0, nb, 0)
    n = np.abs(rel)
    max_exact = nb // 2
    nf = np.maximum(n, 1).astype(np.float64)
    large = max_exact + (np.log(nf / max_exact) / math.log(REL_MAX_DIST / max_exact)
                         * (nb - max_exact)).astype(np.int32)
    large = np.minimum(large, nb - 1)
    return (ret + np.where(n < max_exact, n, large)).astype(np.int32)


def _bias_index_maps(seq):
    kk = np.arange(TK, dtype=np.int64)[:, None]
    qq = np.arange(TQ, dtype=np.int64)[None, :]
    diag = _rel_bucket_np(kk - qq)
    diag = np.where(kk // CHUNK <= qq // CHUNK, diag, MASKED_BUCKET).astype(np.int32)
    prev = _rel_bucket_np(kk - TK - qq)
    far = _rel_bucket_np(np.arange(-seq, -TK, dtype=np.int64))
    far_bucket = int(far[0])
    assert (far == far_bucket).all(), "keys two tiles back must share one bucket"
    return diag, prev, far_bucket


def _bias_kernel(far_bucket, tab_ref, idx_ref, out_ref):
    for h in range(DIFF_HEADS):
        far = tab_ref[far_bucket, h]
        for t in range(idx_ref.shape[0]):
            idx = idx_ref[t]
            acc = jnp.full(idx.shape, -jnp.inf, F32)
            for b in range(REL_BUCKETS):
                acc = jnp.where(idx == b, (tab_ref[b, h] - far) * LOG2E, acc)
            out_ref[t, h] = acc


def _bias_tiles(rel_table, seq):
    diag, prev, far_bucket = _bias_index_maps(seq)
    idx = np.stack([diag, prev])
    vmem = pl.BlockSpec(memory_space=pltpu.VMEM)
    return pl.pallas_call(
        functools.partial(_bias_kernel, far_bucket),
        in_specs=[pl.BlockSpec(memory_space=pltpu.SMEM), vmem],
        out_specs=vmem,
        out_shape=jax.ShapeDtypeStruct((idx.shape[0], DIFF_HEADS, TK, TQ), F32),
        name="bias_tiles",
    )(rel_table, jnp.asarray(idx))


DIFF_CHAINS = 2 * DIFF_HEADS
CHAINS = DIFF_CHAINS + FOX_HEADS
QK_AHEAD = 8
Q_BLOCKS = 2


def _attn_kernel(lam_init, lam_ref, gsub_ref, bnear_ref, causal_ref, pick_ref,
                 qd_ref, kd_ref, vd_ref, qf_ref, kf_ref, vf_ref, dec_ref,
                 od_ref, of_ref, m_ref, l_ref, accd_ref, accf_ref):
    g = pl.program_id(1)
    lam_v = lam_ref[...]
    lam = (jnp.exp(jnp.sum(lam_v[0:1] * lam_v[1:2], axis=-1, keepdims=True))
           - jnp.exp(jnp.sum(lam_v[2:3] * lam_v[3:4], axis=-1, keepdims=True))
           + lam_init)

    def is_fox(c):
        return c >= DIFF_CHAINS

    def cols(c):
        blk = (c % DIFF_CHAINS) // 2
        return slice(blk * LANES, (blk + 1) * LANES)

    def rows(qb):
        return slice(qb * TQ, (qb + 1) * TQ)

    def run(seq):
        tiles = {}

        def load(kind, tile, c):
            key = (kind, id(tile), is_fox(c), cols(c).start)
            if key not in tiles:
                at = pl.ds(pl.multiple_of(tile * TK, TK), TK)
                if kind == "v":
                    tiles[key] = (vf_ref if is_fox(c) else vd_ref)[0, at, cols(c)]
                elif is_fox(c):
                    tiles[key] = jnp.concatenate([kf_ref[0, at, cols(c)], dec_ref[0, at, :]],
                                                 axis=1)
                else:
                    tiles[key] = kd_ref[0, at, cols(c)]
            return tiles[key]

        def scores(tile, qb, c, add, first):
            if is_fox(c):
                q_t = jnp.concatenate([qf_ref[c % 2, 0, rows(qb), cols(c)],
                                       pick_ref[c - DIFF_CHAINS]], axis=1)
            else:
                q_t = qd_ref[c % 2, 0, rows(qb), cols(c)]
            s = lax.dot_general(load("k", tile, c), q_t, NT_DIMS, preferred_element_type=F32)
            return s if add is None else s + add()

        pending = {j: scores(*seq[j]) for j in range(min(QK_AHEAD, len(seq)))}
        for j, (tile, qb, c, _, first) in enumerate(seq):
            s = pending.pop(j)
            state = qb * CHAINS + c
            m_new = jnp.max(s, axis=0, keepdims=True)
            if not first:
                m_old = m_ref[state]
                m_new = jnp.maximum(m_old, m_new)
                alpha = jnp.exp2(m_old - m_new)
            p = jnp.exp2(s - m_new)
            l_new = jnp.sum(p, axis=0, keepdims=True)
            pv = lax.dot_general(load("v", tile, c), p.astype(BF16), TN_DIMS,
                                 preferred_element_type=F32)
            if j + QK_AHEAD < len(seq):
                pending[j + QK_AHEAD] = scores(*seq[j + QK_AHEAD])
            if is_fox(c):
                h = c - DIFF_CHAINS
                acc = accf_ref.at[qb * FOX_HEADS + h]
                pv = pv[(h % 2) * HEAD_DIM:(h % 2 + 1) * HEAD_DIM]
            else:
                acc = accd_ref.at[qb * DIFF_CHAINS + c]
            if not first:
                l_new = alpha * l_ref[state] + l_new
                pv = alpha * acc[...] + pv
            m_ref[state] = m_new
            l_ref[state] = l_new
            acc[...] = pv

    def bias(kind, c):
        if kind == "diag":
            return (lambda: causal_ref[...]) if is_fox(c) else (lambda: bnear_ref[0, c // 2])
        if kind == "prev" and not is_fox(c):
            return lambda: bnear_ref[1, c // 2]
        return None

    def steps(tile, kinds, first=False):
        return [(tile, qb, c0 + e, bias(kind, c0 + e), first)
                for c0 in range(0, CHAINS, 2) for qb, kind in kinds.items() for e in range(2)]

    diag_a = 2 * g
    diag_b = diag_a + 1
    run(steps(diag_b, {1: "diag"}, first=True)
        + steps(diag_a, {0: "diag"}, first=True) + steps(diag_a, {1: "prev"}))

    def body(r, carry):
        hi = diag_a - 1 - 2 * r
        lo = hi - 1
        pl.when(r == 0)(lambda: run(steps(hi, {0: "prev", 1: "far"})
                                    + steps(lo, {0: "far", 1: "far"})))
        pl.when(r > 0)(lambda: run(steps(hi, {0: "far", 1: "far"})
                                   + steps(lo, {0: "far", 1: "far"})))
        return carry

    lax.fori_loop(0, g, body, 0)

    for qb in range(Q_BLOCKS):
        inv_l = [1.0 / l_ref[qb * CHAINS + c] for c in range(CHAINS)]
        for h in range(DIFF_HEADS):
            acc = [accd_ref[qb * DIFF_CHAINS + 2 * h + e] * inv_l[2 * h + e] for e in range(2)]
            o = (acc[0] - lam * acc[1]).T
            o = _rms(o, gsub_ref[...]) * (1.0 - lam_init)
            od_ref[0, rows(qb), h * LANES:(h + 1) * LANES] = o.astype(BF16)
        for pair in range(FOX_HEADS // 2):
            o = jnp.concatenate(
                [accf_ref[qb * FOX_HEADS + 2 * pair + e] * inv_l[DIFF_CHAINS + 2 * pair + e]
                 for e in range(2)], axis=0).T
            of_ref[0, rows(qb), pair * LANES:(pair + 1) * LANES] = o.astype(BF16)


def _attention(lam_vecs, g_subln, bias_near, qd, kd, vd, qf, kf, vf, decay, lam_init):
    batch, seq, _ = kd.shape
    const2 = lambda b, i: (0, 0)
    const4 = lambda b, i: (0, 0, 0, 0)
    kk = np.arange(TK)[:, None]
    qq = np.arange(TQ)[None, :]
    causal = jnp.asarray(np.where(kk <= qq, 0.0, -np.inf).astype(np.float32))
    lane = np.arange(LANES)
    pick = (lane[None] < DECAY_PARTS * FOX_HEADS) & (lane[None] % FOX_HEADS
                                                     == np.arange(FOX_HEADS)[:, None])
    pick = jnp.asarray(np.broadcast_to(pick[:, None, :], (FOX_HEADS, TQ, LANES)), BF16)
    qblk = pl.BlockSpec((1, Q_BLOCKS * TQ, BRANCH_WIDTH), lambda b, i: (b, i, 0))
    qsel = pl.BlockSpec((2, 1, Q_BLOCKS * TQ, BRANCH_WIDTH), lambda b, i: (0, b, i, 0))
    full = pl.BlockSpec((1, seq, BRANCH_WIDTH), lambda b, i: (b, 0, 0))
    out = jax.ShapeDtypeStruct((batch, seq, BRANCH_WIDTH), BF16)
    return pl.pallas_call(
        functools.partial(_attn_kernel, lam_init),
        grid=(batch, seq // (Q_BLOCKS * TQ)),
        in_specs=[
            pl.BlockSpec(lam_vecs.shape, const2),
            pl.BlockSpec(g_subln.shape, const2),
            pl.BlockSpec(bias_near.shape, const4),
            pl.BlockSpec(causal.shape, const2),
            pl.BlockSpec(pick.shape, lambda b, i: (0, 0, 0)),
            qsel, full, full, qsel, full, full,
            pl.BlockSpec((1, seq, LANES), lambda b, i: (b, 0, 0)),
        ],
        out_specs=[qblk, qblk],
        out_shape=[out, out],
        scratch_shapes=[
            pltpu.VMEM((Q_BLOCKS * CHAINS, 1, TQ), F32),
            pltpu.VMEM((Q_BLOCKS * CHAINS, 1, TQ), F32),
            pltpu.VMEM((Q_BLOCKS * DIFF_CHAINS, LANES, TQ), F32),
            pltpu.VMEM((Q_BLOCKS * FOX_HEADS, HEAD_DIM, TQ), F32),
        ],
        compiler_params=pltpu.CompilerParams(
            dimension_semantics=("parallel", "arbitrary"), vmem_limit_bytes=VMEM_LIMIT),
        name="attention",
    )(lam_vecs, g_subln, bias_near, causal, pick, qd, kd, vd, qf, kf, vf, decay)


def _merge_kernel(x_ref, od_ref, of_ref, g_ref, wg_ref, wpa_ref, wpb_ref, wo_ref, y_ref):
    x = x_ref[...]
    h = _rms(x, g_ref[...]).astype(BF16)
    a = jnp.dot(od_ref[...], wpa_ref[...], preferred_element_type=F32)
    b = jnp.dot(of_ref[...], wpb_ref[...], preferred_element_type=F32)
    ga = jax.nn.sigmoid(jnp.dot(h, wg_ref[:, :D_MODEL], preferred_element_type=F32))
    merged = ga * a
    gb = jax.nn.sigmoid(jnp.dot(h, wg_ref[:, D_MODEL:], preferred_element_type=F32))
    merged = (merged + gb * b).astype(BF16)
    y_ref[...] = x + jnp.dot(merged, wo_ref[...], preferred_element_type=F32)


def _merge(x2, od, of, g_mix, w_gate, w_pa, w_pb, w_o):
    n = x2.shape[0]
    tm = TM_PROJ
    const = lambda i: (0, 0)
    row = lambda i: (i, 0)
    return pl.pallas_call(
        _merge_kernel,
        grid=(n // tm,),
        in_specs=[
            pl.BlockSpec((tm, D_MODEL), row),
            pl.BlockSpec((tm, BRANCH_WIDTH), row),
            pl.BlockSpec((tm, BRANCH_WIDTH), row),
            pl.BlockSpec((1, D_MODEL), const),
            pl.BlockSpec(w_gate.shape, const),
            pl.BlockSpec(w_pa.shape, const),
            pl.BlockSpec(w_pb.shape, const),
            pl.BlockSpec(w_o.shape, const),
        ],
        out_specs=pl.BlockSpec((tm, D_MODEL), row),
        out_shape=jax.ShapeDtypeStruct((n, D_MODEL), F32),
        compiler_params=pltpu.CompilerParams(
            dimension_semantics=("parallel",), vmem_limit_bytes=VMEM_LIMIT),
        name="merge",
    )(x2, od, of, g_mix, w_gate, w_pa, w_pb, w_o)


FF_CHUNK = 1024


def _mlp_kernel(final_norm, x_ref, g_ref, w1_ref, w2_ref, gf_ref, y_ref):
    x = x_ref[...]
    h = _rms(x, g_ref[...]).astype(BF16)
    y = x
    for c in range(D_FF // FF_CHUNK):
        cols = slice(c * FF_CHUNK, (c + 1) * FF_CHUNK)
        u = jnp.maximum(jnp.dot(h, w1_ref[:, cols], preferred_element_type=F32), 0.0)
        y = y + jnp.dot((u * u).astype(BF16), w2_ref[cols, :], preferred_element_type=F32)
    y_ref[...] = _rms(y, gf_ref[...]) if final_norm else y


def _mlp(x2, g_mlp, w_1, w_2, g_final, final_norm):
    n = x2.shape[0]
    tm = TM_PROJ
    const = lambda i: (0, 0)
    row = lambda i: (i, 0)
    single = pl.Buffered(1)
    return pl.pallas_call(
        functools.partial(_mlp_kernel, final_norm),
        grid=(n // tm,),
        in_specs=[
            pl.BlockSpec((tm, D_MODEL), row),
            pl.BlockSpec((1, D_MODEL), const),
            pl.BlockSpec(w_1.shape, const, pipeline_mode=single),
            pl.BlockSpec(w_2.shape, const, pipeline_mode=single),
            pl.BlockSpec((1, D_MODEL), const),
        ],
        out_specs=pl.BlockSpec((tm, D_MODEL), row),
        out_shape=jax.ShapeDtypeStruct((n, D_MODEL), F32),
        compiler_params=pltpu.CompilerParams(
            dimension_semantics=("parallel",), vmem_limit_bytes=VMEM_LIMIT),
        name="mlp",
    )(x2, g_mlp, w_1, w_2, g_final)


def _layer(x, layer_idx, g_mix, w_in, b_f, lam_q1, lam_k1, lam_q2, lam_k2, g_subln,
           w_pa, w_pb, w_o, g_mlp, w_1, w_2, bias_near, g_final, final_norm):
    batch, seq, d = x.shape
    n = batch * seq
    x2 = x.reshape(n, d)
    qkv_cols = 6 * BRANCH_WIDTH
    pad = LANES - DECAY_PARTS * FOX_HEADS
    w_fl = jnp.pad(jnp.tile(w_in[:, qkv_cols:qkv_cols + FOX_HEADS], (1, DECAY_PARTS)),
                   ((0, 0), (0, pad))).astype(BF16)
    b_fl = jnp.pad(jnp.tile(b_f, DECAY_PARTS), (0, pad)).reshape(1, LANES)
    w_gate = w_in[:, qkv_cols + FOX_HEADS:].astype(BF16)
    g_mix2 = g_mix.reshape(1, d)

    qd, kd, vd, qf, kf, vf, decay = _in_proj(x2, g_mix2, w_in, w_fl, b_fl, seq)

    lam_vecs = jnp.stack([lam_q1, lam_k1, lam_q2, lam_k2]).astype(F32)
    shape3 = (batch, seq, BRANCH_WIDTH)
    shape4 = (2,) + shape3
    od, of = _attention(lam_vecs, g_subln.reshape(1, LANES), bias_near,
                        qd.reshape(shape4), kd.reshape(shape3), vd.reshape(shape3),
                        qf.reshape(shape4), kf.reshape(shape3), vf.reshape(shape3),
                        decay.reshape(batch, seq, LANES), _lambda_init(layer_idx))

    x1 = _merge(x2, od.reshape(n, BRANCH_WIDTH), of.reshape(n, BRANCH_WIDTH), g_mix2,
                w_gate, w_pa.astype(BF16), w_pb.astype(BF16), w_o.astype(BF16))
    y = _mlp(x1, g_mlp.reshape(1, d), w_1.astype(BF16), w_2.astype(BF16), g_final, final_norm)
    return y.reshape(batch, seq, d)


def kernel(x, g_mix, w_in, b_f, lam_q1, lam_k1, lam_q2, lam_k2, g_subln, w_pa, w_pb, w_o,
           g_mlp, w_1, w_2, rel_table, g_final):
    depth = g_mix.shape[0]
    bias_near = _bias_tiles(rel_table, x.shape[1])
    for l in range(depth):
        x = _layer(x, l, g_mix[l], w_in[l], b_f[l], lam_q1[l], lam_k1[l], lam_q2[l], lam_k2[l],
                   g_subln[l], w_pa[l], w_pb[l], w_o[l], g_mlp[l], w_1[l], w_2[l],
                   bias_near, g_final.reshape(1, -1), l == depth - 1)
    return x
```

```python
import functools
import math

import numpy as np
import jax
import jax.numpy as jnp
from jax import lax
from jax.experimental import pallas as pl
from jax.experimental.pallas import tpu as pltpu

D_MODEL = 1024
CHUNK = 64
HEAD_DIM = 64
DIFF_HEADS = 4
FOX_HEADS = 8
BRANCH_WIDTH = 512
D_FF = 4 * D_MODEL
REL_BUCKETS = 32
REL_MAX_DIST = 128
EPS = 1e-6
LANES = 128
MASKED_BUCKET = REL_BUCKETS

TQ = 256
TK = 256
TM_IN_PROJ = 512
TM_PROJ = 1024
VMEM_LIMIT = 56 * 1024 * 1024

LOG2E = math.log2(math.e)
Q_SCALE = HEAD_DIM ** -0.5 * LOG2E

F32 = jnp.float32
BF16 = jnp.bfloat16
NT_DIMS = (((1,), (1,)), ((), ()))


def _lambda_init(layer_idx):
    return 0.8 - 0.6 * math.exp(-0.3 * layer_idx)


def _rms(xf, g):
    return xf * lax.rsqrt(jnp.mean(xf * xf, axis=-1, keepdims=True) + EPS) * g


DECAY_PARTS = 3


def _in_proj_kernel(tiles_per_seq, x_ref, g_ref, w32_ref, wfl_ref, bf_ref,
                    qd_ref, kd_ref, vd_ref, qf_ref, kf_ref, vf_ref, dec_ref, carry_ref, w_ref):
    @pl.when(pl.program_id(0) == 0)
    def _():
        w_ref[...] = w32_ref[...].astype(BF16)

    h = _rms(x_ref[...], g_ref[...]).astype(BF16)

    @pl.when(pl.program_id(0) % tiles_per_seq == 0)
    def _():
        carry_ref[...] = jnp.zeros_like(carry_ref)

    z = jnp.dot(h, wfl_ref[...], preferred_element_type=F32) + bf_ref[...]
    acc = jnp.minimum(z, 0.0) - jnp.log1p(jnp.exp(-jnp.abs(z)))
    rows = acc.shape[0]
    row = lax.broadcasted_iota(jnp.int32, acc.shape, 0)
    d = 1
    while d < rows:
        acc = acc + jnp.where(row >= d, pltpu.roll(acc, d, axis=0), 0.0)
        d *= 2
    acc = acc + carry_ref[...]
    carry_ref[...] = acc[rows - 1:rows, :]
    neg = acc * -LOG2E
    hi = neg.astype(BF16).astype(F32)
    mid = (neg - hi).astype(BF16).astype(F32)
    lo = neg - hi - mid
    lane = lax.broadcasted_iota(jnp.int32, acc.shape, 1)
    piece = jnp.where(lane < FOX_HEADS, hi, jnp.where(lane < 2 * FOX_HEADS, mid, lo))
    dec_ref[...] = jnp.where(lane < DECAY_PARTS * FOX_HEADS, piece, 0.0).astype(BF16)

    outs = (qd_ref, kd_ref, vd_ref, qf_ref, kf_ref, vf_ref)
    for c, o_ref in enumerate(outs):
        w = w_ref[:, c * BRANCH_WIDTH:(c + 1) * BRANCH_WIDTH]
        o = jnp.dot(h, w, preferred_element_type=F32)
        if o_ref is qd_ref or o_ref is qf_ref:
            o = (o * Q_SCALE).astype(BF16)
            low = lax.broadcasted_iota(jnp.int32, o.shape, 1) % LANES < HEAD_DIM
            o_ref[0] = jnp.where(low, o, jnp.zeros_like(o))
            o_ref[1] = jnp.where(low, jnp.zeros_like(o), o)
        elif o_ref is vd_ref or o_ref is vf_ref:
            o_t = o.T.astype(BF16)
            for t in range(o_ref.shape[1]):
                o_ref[0, t] = o_t[:, t * TK:(t + 1) * TK]
        else:
            o_ref[...] = o.astype(BF16)


def _in_proj(x2, g_mix, w_in, w_fl, b_f, seq):
    n = x2.shape[0]
    tm = TM_IN_PROJ
    qkv_cols = 6 * BRANCH_WIDTH
    const = lambda i: (0, 0)
    row = lambda i: (i, 0)
    out_bf = jax.ShapeDtypeStruct((n, BRANCH_WIDTH), BF16)
    out_q = jax.ShapeDtypeStruct((2, n, BRANCH_WIDTH), BF16)
    spec_bf = pl.BlockSpec((tm, BRANCH_WIDTH), row)
    spec_q = pl.BlockSpec((2, tm, BRANCH_WIDTH), lambda i: (0, i, 0))
    tiles_per_seq = seq // tm
    out_v = jax.ShapeDtypeStruct((n // seq, seq // TK, BRANCH_WIDTH, TK), BF16)
    spec_v = pl.BlockSpec((1, tm // TK, BRANCH_WIDTH, TK),
                          lambda i: (i // tiles_per_seq, i % tiles_per_seq, 0, 0))
    return pl.pallas_call(
        functools.partial(_in_proj_kernel, seq // tm),
        grid=(n // tm,),
        in_specs=[
            pl.BlockSpec((tm, D_MODEL), row),
            pl.BlockSpec((1, D_MODEL), const),
            pl.BlockSpec((D_MODEL, qkv_cols), const, pipeline_mode=pl.Buffered(1)),
            pl.BlockSpec(w_fl.shape, const),
            pl.BlockSpec((1, LANES), const),
        ],
        out_specs=[spec_q, spec_bf, spec_v, spec_q, spec_bf, spec_v,
                   pl.BlockSpec((tm, LANES), row)],
        out_shape=[out_q, out_bf, out_v, out_q, out_bf, out_v,
                   jax.ShapeDtypeStruct((n, LANES), BF16)],
        scratch_shapes=[pltpu.VMEM((1, LANES), F32),
                        pltpu.VMEM((D_MODEL, qkv_cols), BF16)],
        compiler_params=pltpu.CompilerParams(
            dimension_semantics=("arbitrary",), vmem_limit_bytes=VMEM_LIMIT),
        name="in_proj",
    )(x2, g_mix, w_in, w_fl, b_f)


def _rel_bucket_np(rel):
    nb = REL_BUCKETS // 2
    ret = np.where(rel > 0, nb, 0)
    n = np.abs(rel)
    max_exact = nb // 2
    nf = np.maximum(n, 1).astype(np.float64)
    large = max_exact + (np.log(nf / max_exact) / math.log(REL_MAX_DIST / max_exact)
                         * (nb - max_exact)).astype(np.int32)
    large = np.minimum(large, nb - 1)
    return (ret + np.where(n < max_exact, n, large)).astype(np.int32)


def _bias_index_maps(seq):
    kk = np.arange(TK, dtype=np.int64)[:, None]
    qq = np.arange(TQ, dtype=np.int64)[None, :]
    diag = _rel_bucket_np(kk - qq)
    diag = np.where(kk // CHUNK <= qq // CHUNK, diag, MASKED_BUCKET).astype(np.int32)
    prev = _rel_bucket_np(kk - TK - qq)
    far = _rel_bucket_np(np.arange(-seq, -TK, dtype=np.int64))
    far_bucket = int(far[0])
    assert (far == far_bucket).all(), "keys two tiles back must share one bucket"
    return diag, prev, far_bucket


def _bias_kernel(far_bucket, tab_ref, idx_ref, out_ref):
    for h in range(DIFF_HEADS):
        far = tab_ref[far_bucket, h]
        for t in range(idx_ref.shape[0]):
            idx = idx_ref[t]
            acc = jnp.full(idx.shape, -jnp.inf, F32)
            for b in range(REL_BUCKETS):
                acc = jnp.where(idx == b, (tab_ref[b, h] - far) * LOG2E, acc)
            out_ref[t, h] = acc


def _bias_tiles(rel_table, seq):
    diag, prev, far_bucket = _bias_index_maps(seq)
    idx = np.stack([diag, prev])
    vmem = pl.BlockSpec(memory_space=pltpu.VMEM)
    return pl.pallas_call(
        functools.partial(_bias_kernel, far_bucket),
        in_specs=[pl.BlockSpec(memory_space=pltpu.SMEM), vmem],
        out_specs=vmem,
        out_shape=jax.ShapeDtypeStruct((idx.shape[0], DIFF_HEADS, TK, TQ), F32),
        name="bias_tiles",
    )(rel_table, jnp.asarray(idx))


DIFF_CHAINS = 2 * DIFF_HEADS
CHAINS = DIFF_CHAINS + FOX_HEADS
QK_AHEAD = 8
Q_BLOCKS = 2
SUM_ROWS = 16


def _attn_kernel(lam_init, lam_ref, gsub_ref, bnear_ref, causal_ref, pick_ref,
                 qd_ref, kd_ref, vd_ref, qf_ref, kf_ref, vf_ref, dec_ref,
                 od_ref, of_ref, m_ref, accd_ref, accf_ref):
    g = pl.program_id(1)
    lam_v = lam_ref[...]
    lam = (jnp.exp(jnp.sum(lam_v[0:1] * lam_v[1:2], axis=-1, keepdims=True))
           - jnp.exp(jnp.sum(lam_v[2:3] * lam_v[3:4], axis=-1, keepdims=True))
           + lam_init)

    def is_fox(c):
        return c >= DIFF_CHAINS

    def cols(c):
        blk = (c % DIFF_CHAINS) // 2
        return slice(blk * LANES, (blk + 1) * LANES)

    def rows(qb):
        return slice(qb * TQ, (qb + 1) * TQ)

    def accumulator(qb, c):
        if is_fox(c):
            return accf_ref.at[qb * FOX_HEADS + c - DIFF_CHAINS]
        return accd_ref.at[qb * DIFF_CHAINS + c]

    def normalised(qb, c):
        acc = accumulator(qb, c)[...]
        chans = acc.shape[0] - SUM_ROWS
        return acc[:chans] * (1.0 / acc[chans:chans + 1])

    def run(seq):
        tiles = {}

        def load(kind, tile, c):
            if kind == "v":
                chan = (slice((c - DIFF_CHAINS) * HEAD_DIM, (c - DIFF_CHAINS + 1) * HEAD_DIM)
                        if is_fox(c) else cols(c))
                key = (kind, id(tile), is_fox(c), chan.start)
                if key not in tiles:
                    v_t = (vf_ref if is_fox(c) else vd_ref)[0, tile, chan, :]
                    tiles[key] = jnp.concatenate([v_t, jnp.ones((SUM_ROWS, TK), BF16)], axis=0)
                return tiles[key]
            key = (kind, id(tile), is_fox(c), cols(c).start)
            if key not in tiles:
                at = pl.ds(pl.multiple_of(tile * TK, TK), TK)
                if is_fox(c):
                    tiles[key] = jnp.concatenate([kf_ref[0, at, cols(c)], dec_ref[0, at, :]],
                                                 axis=1)
                else:
                    tiles[key] = kd_ref[0, at, cols(c)]
            return tiles[key]

        def scores(tile, qb, c, add, first):
            if is_fox(c):
                q_t = jnp.concatenate([qf_ref[c % 2, 0, rows(qb), cols(c)],
                                       pick_ref[c - DIFF_CHAINS]], axis=1)
            else:
                q_t = qd_ref[c % 2, 0, rows(qb), cols(c)]
            s = lax.dot_general(load("k", tile, c), q_t, NT_DIMS, preferred_element_type=F32)
            return s if add is None else s + add()

        pending = {j: scores(*seq[j]) for j in range(min(QK_AHEAD, len(seq)))}
        for j, (tile, qb, c, _, first) in enumerate(seq):
            s = pending.pop(j)
            state = qb * CHAINS + c
            m_new = jnp.max(s, axis=0, keepdims=True)
            if not first:
                m_old = m_ref[state]
                m_new = jnp.maximum(m_old, m_new)
                alpha = jnp.exp2(m_old - m_new)
            p = jnp.exp2(s - m_new).astype(BF16)
            pv = jnp.dot(load("v", tile, c), p, preferred_element_type=F32)
            if j + QK_AHEAD < len(seq):
                pending[j + QK_AHEAD] = scores(*seq[j + QK_AHEAD])
            acc = accumulator(qb, c)
            m_ref[state] = m_new
            acc[...] = pv if first else alpha * acc[...] + pv

    def bias(kind, c):
        if kind == "diag":
            return (lambda: causal_ref[...]) if is_fox(c) else (lambda: bnear_ref[0, c // 2])
        if kind == "prev" and not is_fox(c):
            return lambda: bnear_ref[1, c // 2]
        return None

    def steps(tile, kinds, first=False):
        return [(tile, qb, c0 + e, bias(kind, c0 + e), first)
                for c0 in range(0, CHAINS, 2) for qb, kind in kinds.items() for e in range(2)]

    diag_a = 2 * g
    diag_b = diag_a + 1
    run(steps(diag_b, {1: "diag"}, first=True)
        + steps(diag_a, {0: "diag"}, first=True) + steps(diag_a, {1: "prev"}))

    def body(r, carry):
        hi = diag_a - 1 - 2 * r
        lo = hi - 1
        pl.when(r == 0)(lambda: run(steps(hi, {0: "prev", 1: "far"})
                                    + steps(lo, {0: "far", 1: "far"})))
        pl.when(r > 0)(lambda: run(steps(hi, {0: "far", 1: "far"})
                                   + steps(lo, {0: "far", 1: "far"})))
        return carry

    lax.fori_loop(0, g, body, 0)

    for qb in range(Q_BLOCKS):
        for h in range(DIFF_HEADS):
            o = (normalised(qb, 2 * h) - lam * normalised(qb, 2 * h + 1)).T
            o = _rms(o, gsub_ref[...]) * (1.0 - lam_init)
            od_ref[0, rows(qb), h * LANES:(h + 1) * LANES] = o.astype(BF16)
        for pair in range(FOX_HEADS // 2):
            o = jnp.concatenate([normalised(qb, DIFF_CHAINS + 2 * pair + e) for e in range(2)],
                                axis=0).T
            of_ref[0, rows(qb), pair * LANES:(pair + 1) * LANES] = o.astype(BF16)


def _attention(lam_vecs, g_subln, bias_near, qd, kd, vd, qf, kf, vf, decay, lam_init):
    batch, seq, _ = kd.shape
    const2 = lambda b, i: (0, 0)
    const4 = lambda b, i: (0, 0, 0, 0)
    kk = np.arange(TK)[:, None]
    qq = np.arange(TQ)[None, :]
    causal = jnp.asarray(np.where(kk <= qq, 0.0, -np.inf).astype(np.float32))
    lane = np.arange(LANES)
    pick = (lane[None] < DECAY_PARTS * FOX_HEADS) & (lane[None] % FOX_HEADS
                                                     == np.arange(FOX_HEADS)[:, None])
    pick = jnp.asarray(np.broadcast_to(pick[:, None, :], (FOX_HEADS, TQ, LANES)), BF16)
    qblk = pl.BlockSpec((1, Q_BLOCKS * TQ, BRANCH_WIDTH), lambda b, i: (b, i, 0))
    qsel = pl.BlockSpec((2, 1, Q_BLOCKS * TQ, BRANCH_WIDTH), lambda b, i: (0, b, i, 0))
    full = pl.BlockSpec((1, seq, BRANCH_WIDTH), lambda b, i: (b, 0, 0))
    full_v = pl.BlockSpec((1, seq // TK, BRANCH_WIDTH, TK), lambda b, i: (b, 0, 0, 0))
    out = jax.ShapeDtypeStruct((batch, seq, BRANCH_WIDTH), BF16)
    return pl.pallas_call(
        functools.partial(_attn_kernel, lam_init),
        grid=(batch, seq // (Q_BLOCKS * TQ)),
        in_specs=[
            pl.BlockSpec(lam_vecs.shape, const2),
            pl.BlockSpec(g_subln.shape, const2),
            pl.BlockSpec(bias_near.shape, const4),
            pl.BlockSpec(causal.shape, const2),
            pl.BlockSpec(pick.shape, lambda b, i: (0, 0, 0)),
            qsel, full, full_v, qsel, full, full_v,
            pl.BlockSpec((1, seq, LANES), lambda b, i: (b, 0, 0)),
        ],
        out_specs=[qblk, qblk],
        out_shape=[out, out],
        scratch_shapes=[
            pltpu.VMEM((Q_BLOCKS * CHAINS, 1, TQ), F32),
            pltpu.VMEM((Q_BLOCKS * DIFF_CHAINS, LANES + SUM_ROWS, TQ), F32),
            pltpu.VMEM((Q_BLOCKS * FOX_HEADS, HEAD_DIM + SUM_ROWS, TQ), F32),
        ],
        compiler_params=pltpu.CompilerParams(
            dimension_semantics=("parallel", "arbitrary"), vmem_limit_bytes=VMEM_LIMIT),
        name="attention",
    )(lam_vecs, g_subln, bias_near, causal, pick, qd, kd, vd, qf, kf, vf, decay)


def _merge_kernel(x_ref, od_ref, of_ref, g_ref, wg_ref, wpa_ref, wpb_ref, wo_ref, y_ref):
    x = x_ref[...]
    h = _rms(x, g_ref[...]).astype(BF16)
    a = jnp.dot(od_ref[...], wpa_ref[...], preferred_element_type=F32)
    b = jnp.dot(of_ref[...], wpb_ref[...], preferred_element_type=F32)
    ga = jax.nn.sigmoid(jnp.dot(h, wg_ref[:, :D_MODEL], preferred_element_type=F32))
    merged = ga * a
    gb = jax.nn.sigmoid(jnp.dot(h, wg_ref[:, D_MODEL:], preferred_element_type=F32))
    merged = (merged + gb * b).astype(BF16)
    y_ref[...] = x + jnp.dot(merged, wo_ref[...], preferred_element_type=F32)


def _merge(x2, od, of, g_mix, w_gate, w_pa, w_pb, w_o):
    n = x2.shape[0]
    tm = TM_PROJ
    const = lambda i: (0, 0)
    row = lambda i: (i, 0)
    return pl.pallas_call(
        _merge_kernel,
        grid=(n // tm,),
        in_specs=[
            pl.BlockSpec((tm, D_MODEL), row),
            pl.BlockSpec((tm, BRANCH_WIDTH), row),
            pl.BlockSpec((tm, BRANCH_WIDTH), row),
            pl.BlockSpec((1, D_MODEL), const),
            pl.BlockSpec(w_gate.shape, const),
            pl.BlockSpec(w_pa.shape, const),
            pl.BlockSpec(w_pb.shape, const),
            pl.BlockSpec(w_o.shape, const),
        ],
        out_specs=pl.BlockSpec((tm, D_MODEL), row),
        out_shape=jax.ShapeDtypeStruct((n, D_MODEL), F32),
        compiler_params=pltpu.CompilerParams(
            dimension_semantics=("parallel",), vmem_limit_bytes=VMEM_LIMIT),
        name="merge",
    )(x2, od, of, g_mix, w_gate, w_pa, w_pb, w_o)


FF_CHUNK = 1024


def _mlp_kernel(final_norm, x_ref, g_ref, w1_ref, w2_ref, gf_ref, y_ref):
    x = x_ref[...]
    h = _rms(x, g_ref[...]).astype(BF16)
    y = x
    for c in range(D_FF // FF_CHUNK):
        cols = slice(c * FF_CHUNK, (c + 1) * FF_CHUNK)
        u = jnp.maximum(jnp.dot(h, w1_ref[:, cols], preferred_element_type=F32), 0.0)
        y = y + jnp.dot((u * u).astype(BF16), w2_ref[cols, :], preferred_element_type=F32)
    y_ref[...] = _rms(y, gf_ref[...]) if final_norm else y


def _mlp(x2, g_mlp, w_1, w_2, g_final, final_norm):
    n = x2.shape[0]
    tm = TM_PROJ
    const = lambda i: (0, 0)
    row = lambda i: (i, 0)
    single = pl.Buffered(1)
    return pl.pallas_call(
        functools.partial(_mlp_kernel, final_norm),
        grid=(n // tm,),
        in_specs=[
            pl.BlockSpec((tm, D_MODEL), row),
            pl.BlockSpec((1, D_MODEL), const),
            pl.BlockSpec(w_1.shape, const, pipeline_mode=single),
            pl.BlockSpec(w_2.shape, const, pipeline_mode=single),
            pl.BlockSpec((1, D_MODEL), const),
        ],
        out_specs=pl.BlockSpec((tm, D_MODEL), row),
        out_shape=jax.ShapeDtypeStruct((n, D_MODEL), F32),
        compiler_params=pltpu.CompilerParams(
            dimension_semantics=("parallel",), vmem_limit_bytes=VMEM_LIMIT),
        name="mlp",
    )(x2, g_mlp, w_1, w_2, g_final)


def _layer(x, layer_idx, g_mix, w_in, b_f, lam_q1, lam_k1, lam_q2, lam_k2, g_subln,
           w_pa, w_pb, w_o, g_mlp, w_1, w_2, bias_near, g_final, final_norm):
    batch, seq, d = x.shape
    n = batch * seq
    x2 = x.reshape(n, d)
    qkv_cols = 6 * BRANCH_WIDTH
    pad = LANES - DECAY_PARTS * FOX_HEADS
    w_fl = jnp.pad(jnp.tile(w_in[:, qkv_cols:qkv_cols + FOX_HEADS], (1, DECAY_PARTS)),
                   ((0, 0), (0, pad))).astype(BF16)
    b_fl = jnp.pad(jnp.tile(b_f, DECAY_PARTS), (0, pad)).reshape(1, LANES)
    w_gate = w_in[:, qkv_cols + FOX_HEADS:].astype(BF16)
    g_mix2 = g_mix.reshape(1, d)

    qd, kd, vd, qf, kf, vf, decay = _in_proj(x2, g_mix2, w_in, w_fl, b_fl, seq)

    lam_vecs = jnp.stack([lam_q1, lam_k1, lam_q2, lam_k2]).astype(F32)
    shape3 = (batch, seq, BRANCH_WIDTH)
    shape4 = (2,) + shape3
    od, of = _attention(lam_vecs, g_subln.reshape(1, LANES), bias_near,
                        qd.reshape(shape4), kd.reshape(shape3), vd,
                        qf.reshape(shape4), kf.reshape(shape3), vf,
                        decay.reshape(batch, seq, LANES), _lambda_init(layer_idx))

    x1 = _merge(x2, od.reshape(n, BRANCH_WIDTH), of.reshape(n, BRANCH_WIDTH), g_mix2,
                w_gate, w_pa.astype(BF16), w_pb.astype(BF16), w_o.astype(BF16))
    y = _mlp(x1, g_mlp.reshape(1, d), w_1.astype(BF16), w_2.astype(BF16), g_final, final_norm)
    return y.reshape(batch, seq, d)


def kernel(x, g_mix, w_in, b_f, lam_q1, lam_k1, lam_q2, lam_k2, g_subln, w_pa, w_pb, w_o,
           g_mlp, w_1, w_2, rel_table, g_final):
    depth = g_mix.shape[0]
    bias_near = _bias_tiles(rel_table, x.shape[1])
    for l in range(depth):
        x = _layer(x, l, g_mix[l], w_in[l], b_f[l], lam_q1[l], lam_k1[l], lam_q2[l], lam_k2[l],
                   g_subln[l], w_pa[l], w_pb[l], w_o[l], g_mlp[l], w_1[l], w_2[l],
                   bias_near, g_final.reshape(1, -1), l == depth - 1)
    return x
```

```python
import functools
import math

import numpy as np
import jax
import jax.numpy as jnp
from jax import lax
from jax.experimental import pallas as pl
from jax.experimental.pallas import tpu as pltpu

D_MODEL = 1024
CHUNK = 64
HEAD_DIM = 64
DIFF_HEADS = 4
FOX_HEADS = 8
BRANCH_WIDTH = 512
D_FF = 4 * D_MODEL
REL_BUCKETS = 32
REL_MAX_DIST = 128
EPS = 1e-6
LANES = 128
MASKED_BUCKET = REL_BUCKETS

TQ = 256
TK = 256
TM_IN_PROJ = 512
TM_PROJ = 1024
VMEM_LIMIT = 56 * 1024 * 1024

LOG2E = math.log2(math.e)
Q_SCALE = HEAD_DIM ** -0.5 * LOG2E

F32 = jnp.float32
BF16 = jnp.bfloat16
NT_DIMS = (((1,), (1,)), ((), ()))


def _lambda_init(layer_idx):
    return 0.8 - 0.6 * math.exp(-0.3 * layer_idx)


def _rms(xf, g):
    return xf * lax.rsqrt(jnp.mean(xf * xf, axis=-1, keepdims=True) + EPS) * g


DECAY_PARTS = 3


def _in_proj_kernel(tiles_per_seq, x0_ref, xnext_ref, g_ref, w32_ref, wfl_ref, bf_ref,
                    qd_ref, kd_ref, vd_ref, qf_ref, kf_ref, vf_ref, dec_ref,
                    carry_ref, w_ref, h_ref):
    step = pl.program_id(0)

    @pl.when(step == 0)
    def _():
        w_ref[...] = w32_ref[...].astype(BF16)
        h_ref[0] = _rms(x0_ref[...], g_ref[...]).astype(BF16)

    h = h_ref[step % 2]

    @pl.when(step % tiles_per_seq == 0)
    def _():
        carry_ref[...] = jnp.zeros_like(carry_ref)

    z = jnp.dot(h, wfl_ref[...], preferred_element_type=F32) + bf_ref[...]
    acc = jnp.minimum(z, 0.0) - jnp.log1p(jnp.exp(-jnp.abs(z)))
    rows = acc.shape[0]
    row = lax.broadcasted_iota(jnp.int32, acc.shape, 0)
    d = 1
    while d < rows:
        acc = acc + jnp.where(row >= d, pltpu.roll(acc, d, axis=0), 0.0)
        d *= 2
    acc = acc + carry_ref[...]
    carry_ref[...] = acc[rows - 1:rows, :]
    neg = acc * -LOG2E
    hi = neg.astype(BF16).astype(F32)
    mid = (neg - hi).astype(BF16).astype(F32)
    lo = neg - hi - mid
    lane = lax.broadcasted_iota(jnp.int32, acc.shape, 1)
    piece = jnp.where(lane < FOX_HEADS, hi, jnp.where(lane < 2 * FOX_HEADS, mid, lo))
    dec_ref[...] = jnp.where(lane < DECAY_PARTS * FOX_HEADS, piece, 0.0).astype(BF16)

    outs = (qd_ref, kd_ref, vd_ref, qf_ref, kf_ref, vf_ref)
    for c in (2, 5, 0, 3, 1, 4):
        o_ref = outs[c]
        w = w_ref[:, c * BRANCH_WIDTH:(c + 1) * BRANCH_WIDTH]
        o = jnp.dot(h, w, preferred_element_type=F32)
        if o_ref is qd_ref or o_ref is qf_ref:
            o = (o * Q_SCALE).astype(BF16)
            low = lax.broadcasted_iota(jnp.int32, o.shape, 1) % LANES < HEAD_DIM
            o_ref[0] = jnp.where(low, o, jnp.zeros_like(o))
            o_ref[1] = jnp.where(low, jnp.zeros_like(o), o)
        elif o_ref is vd_ref or o_ref is vf_ref:
            o_t = o.T.astype(BF16)
            for t in range(o_ref.shape[1]):
                o_ref[0, t] = o_t[:, t * TK:(t + 1) * TK]
        else:
            o_ref[...] = o.astype(BF16)

    h_ref[(step + 1) % 2] = _rms(xnext_ref[...], g_ref[...]).astype(BF16)


def _in_proj(x2, g_mix, w_in, w_fl, b_f, seq):
    n = x2.shape[0]
    tm = TM_IN_PROJ
    qkv_cols = 6 * BRANCH_WIDTH
    const = lambda i: (0, 0)
    row = lambda i: (i, 0)
    out_bf = jax.ShapeDtypeStruct((n, BRANCH_WIDTH), BF16)
    out_q = jax.ShapeDtypeStruct((2, n, BRANCH_WIDTH), BF16)
    spec_bf = pl.BlockSpec((tm, BRANCH_WIDTH), row)
    spec_q = pl.BlockSpec((2, tm, BRANCH_WIDTH), lambda i: (0, i, 0))
    tiles_per_seq = seq // tm
    out_v = jax.ShapeDtypeStruct((n // seq, seq // TK, BRANCH_WIDTH, TK), BF16)
    spec_v = pl.BlockSpec((1, tm // TK, BRANCH_WIDTH, TK),
                          lambda i: (i // tiles_per_seq, i % tiles_per_seq, 0, 0))
    return pl.pallas_call(
        functools.partial(_in_proj_kernel, seq // tm),
        grid=(n // tm,),
        in_specs=[
            pl.BlockSpec((tm, D_MODEL), const),
            pl.BlockSpec((tm, D_MODEL), lambda i: (jnp.minimum(i + 1, n // tm - 1), 0)),
            pl.BlockSpec((1, D_MODEL), const),
            pl.BlockSpec((D_MODEL, qkv_cols), const, pipeline_mode=pl.Buffered(1)),
            pl.BlockSpec(w_fl.shape, const),
            pl.BlockSpec((1, LANES), const),
        ],
        out_specs=[spec_q, spec_bf, spec_v, spec_q, spec_bf, spec_v,
                   pl.BlockSpec((tm, LANES), row)],
        out_shape=[out_q, out_bf, out_v, out_q, out_bf, out_v,
                   jax.ShapeDtypeStruct((n, LANES), BF16)],
        scratch_shapes=[pltpu.VMEM((1, LANES), F32),
                        pltpu.VMEM((D_MODEL, qkv_cols), BF16),
                        pltpu.VMEM((2, tm, D_MODEL), BF16)],
        compiler_params=pltpu.CompilerParams(
            dimension_semantics=("arbitrary",), vmem_limit_bytes=VMEM_LIMIT),
        name="in_proj",
    )(x2, x2, g_mix, w_in, w_fl, b_f)


def _rel_bucket_np(rel):
    nb = REL_BUCKETS // 2
    ret = np.where(rel > 0, nb, 0)
    n = np.abs(rel)
    max_exact = nb // 2
    nf = np.maximum(n, 1).astype(np.float64)
    large = max_exact + (np.log(nf / max_exact) / math.log(REL_MAX_DIST / max_exact)
                         * (nb - max_exact)).astype(np.int32)
    large = np.minimum(large, nb - 1)
    return (ret + np.where(n < max_exact, n, large)).astype(np.int32)


def _bias_index_maps(seq):
    kk = np.arange(TK, dtype=np.int64)[:, None]
    qq = np.arange(TQ, dtype=np.int64)[None, :]
    diag = _rel_bucket_np(kk - qq)
    diag = np.where(kk // CHUNK <= qq // CHUNK, diag, MASKED_BUCKET).astype(np.int32)
    prev = _rel_bucket_np(kk - TK - qq)
    far = _rel_bucket_np(np.arange(-seq, -TK, dtype=np.int64))
    far_bucket = int(far[0])
    assert (far == far_bucket).all(), "keys two tiles back must share one bucket"
    return diag, prev, far_bucket


def _bias_kernel(far_bucket, tab_ref, idx_ref, out_ref):
    for h in range(DIFF_HEADS):
        far = tab_ref[far_bucket, h]
        for t in range(idx_ref.shape[0]):
            idx = idx_ref[t]
            acc = jnp.full(idx.shape, -jnp.inf, F32)
            for b in range(REL_BUCKETS):
                acc = jnp.where(idx == b, (tab_ref[b, h] - far) * LOG2E, acc)
            out_ref[t, h] = acc


def _bias_tiles(rel_table, seq):
    diag, prev, far_bucket = _bias_index_maps(seq)
    idx = np.stack([diag, prev])
    vmem = pl.BlockSpec(memory_space=pltpu.VMEM)
    return pl.pallas_call(
        functools.partial(_bias_kernel, far_bucket),
        in_specs=[pl.BlockSpec(memory_space=pltpu.SMEM), vmem],
        out_specs=vmem,
        out_shape=jax.ShapeDtypeStruct((idx.shape[0], DIFF_HEADS, TK, TQ), F32),
        name="bias_tiles",
    )(rel_table, jnp.asarray(idx))


DIFF_CHAINS = 2 * DIFF_HEADS
CHAINS = DIFF_CHAINS + FOX_HEADS
QK_AHEAD = 8
Q_BLOCKS = 2
SUM_ROWS = 16


def _attn_kernel(lam_init, lam_ref, gsub_ref, bnear_ref, causal_ref, pick_ref,
                 qd_ref, kd_ref, vd_ref, qf_ref, kf_ref, vf_ref, dec_ref,
                 od_ref, of_ref, m_ref, accd_ref, accf_ref):
    g = pl.program_id(1)
    lam_v = lam_ref[...]
    lam = (jnp.exp(jnp.sum(lam_v[0:1] * lam_v[1:2], axis=-1, keepdims=True))
           - jnp.exp(jnp.sum(lam_v[2:3] * lam_v[3:4], axis=-1, keepdims=True))
           + lam_init)

    def is_fox(c):
        return c >= DIFF_CHAINS

    def cols(c):
        blk = (c % DIFF_CHAINS) // 2
        return slice(blk * LANES, (blk + 1) * LANES)

    def rows(qb):
        return slice(qb * TQ, (qb + 1) * TQ)

    def accumulator(qb, c):
        if is_fox(c):
            return accf_ref.at[qb * FOX_HEADS + c - DIFF_CHAINS]
        return accd_ref.at[qb * DIFF_CHAINS + c]

    def normalised(qb, c):
        acc = accumulator(qb, c)[...]
        chans = acc.shape[0] - SUM_ROWS
        return acc[:chans] * (1.0 / acc[chans:chans + 1])

    def run(seq):
        tiles = {}

        def load(kind, tile, c):
            if kind == "v":
                chan = (slice((c - DIFF_CHAINS) * HEAD_DIM, (c - DIFF_CHAINS + 1) * HEAD_DIM)
                        if is_fox(c) else cols(c))
                key = (kind, id(tile), is_fox(c), chan.start)
                if key not in tiles:
                    v_t = (vf_ref if is_fox(c) else vd_ref)[0, tile, chan, :]
                    tiles[key] = jnp.concatenate([v_t, jnp.ones((SUM_ROWS, TK), BF16)], axis=0)
                return tiles[key]
            key = (kind, id(tile), is_fox(c), cols(c).start)
            if key not in tiles:
                at = pl.ds(pl.multiple_of(tile * TK, TK), TK)
                if is_fox(c):
                    tiles[key] = jnp.concatenate([kf_ref[0, at, cols(c)], dec_ref[0, at, :]],
                                                 axis=1)
                else:
                    tiles[key] = kd_ref[0, at, cols(c)]
            return tiles[key]

        def scores(tile, qb, c, add, first):
            if is_fox(c):
                q_t = jnp.concatenate([qf_ref[c % 2, 0, rows(qb), cols(c)],
                                       pick_ref[c - DIFF_CHAINS]], axis=1)
            else:
                q_t = qd_ref[c % 2, 0, rows(qb), cols(c)]
            s = lax.dot_general(load("k", tile, c), q_t, NT_DIMS, preferred_element_type=F32)
            return s if add is None else s + add()

        pending = {j: scores(*seq[j]) for j in range(min(QK_AHEAD, len(seq)))}
        for j, (tile, qb, c, _, first) in enumerate(seq):
            s = pending.pop(j)
            state = qb * CHAINS + c
            m_new = jnp.max(s, axis=0, keepdims=True)
            if not first:
                m_old = m_ref[state]
                m_new = jnp.maximum(m_old, m_new)
                alpha = jnp.exp2(m_old - m_new)
            p = jnp.exp2(s - m_new).astype(BF16)
            pv = jnp.dot(load("v", tile, c), p, preferred_element_type=F32)
            if j + QK_AHEAD < len(seq):
                pending[j + QK_AHEAD] = scores(*seq[j + QK_AHEAD])
            acc = accumulator(qb, c)
            m_ref[state] = m_new
            acc[...] = pv if first else alpha * acc[...] + pv

    def bias(kind, c):
        if kind == "diag":
            return (lambda: causal_ref[...]) if is_fox(c) else (lambda: bnear_ref[0, c // 2])
        if kind == "prev" and not is_fox(c):
            return lambda: bnear_ref[1, c // 2]
        return None

    def steps(tile, kinds, first=False):
        order = [c0 + b for c0 in range(0, DIFF_CHAINS, 2) for b in (0, DIFF_CHAINS)]
        return [(tile, qb, c0 + e, bias(kind, c0 + e), first)
                for c0 in order for qb, kind in kinds.items() for e in range(2)]

    diag_a = 2 * g
    diag_b = diag_a + 1
    run(steps(diag_b, {1: "diag"}, first=True)
        + steps(diag_a, {0: "diag"}, first=True) + steps(diag_a, {1: "prev"}))

    def body(r, carry):
        hi = diag_a - 1 - 2 * r
        lo = hi - 1
        pl.when(r == 0)(lambda: run(steps(hi, {0: "prev", 1: "far"})
                                    + steps(lo, {0: "far", 1: "far"})))
        pl.when(r > 0)(lambda: run(steps(hi, {0: "far", 1: "far"})
                                   + steps(lo, {0: "far", 1: "far"})))
        return carry

    lax.fori_loop(0, g, body, 0)

    for qb in range(Q_BLOCKS):
        for h in range(DIFF_HEADS):
            o = (normalised(qb, 2 * h) - lam * normalised(qb, 2 * h + 1)).T
            o = _rms(o, gsub_ref[...]) * (1.0 - lam_init)
            od_ref[0, rows(qb), h * LANES:(h + 1) * LANES] = o.astype(BF16)
        for pair in range(FOX_HEADS // 2):
            o = jnp.concatenate([normalised(qb, DIFF_CHAINS + 2 * pair + e) for e in range(2)],
                                axis=0).T
            of_ref[0, rows(qb), pair * LANES:(pair + 1) * LANES] = o.astype(BF16)


def _attention(lam_vecs, g_subln, bias_near, qd, kd, vd, qf, kf, vf, decay, lam_init):
    batch, seq, _ = kd.shape
    const2 = lambda b, i: (0, 0)
    const4 = lambda b, i: (0, 0, 0, 0)
    kk = np.arange(TK)[:, None]
    qq = np.arange(TQ)[None, :]
    causal = jnp.asarray(np.where(kk <= qq, 0.0, -np.inf).astype(np.float32))
    lane = np.arange(LANES)
    pick = (lane[None] < DECAY_PARTS * FOX_HEADS) & (lane[None] % FOX_HEADS
                                                     == np.arange(FOX_HEADS)[:, None])
    pick = jnp.asarray(np.broadcast_to(pick[:, None, :], (FOX_HEADS, TQ, LANES)), BF16)
    qblk = pl.BlockSpec((1, Q_BLOCKS * TQ, BRANCH_WIDTH), lambda b, i: (b, i, 0))
    qsel = pl.BlockSpec((2, 1, Q_BLOCKS * TQ, BRANCH_WIDTH), lambda b, i: (0, b, i, 0))
    full = pl.BlockSpec((1, seq, BRANCH_WIDTH), lambda b, i: (b, 0, 0))
    full_v = pl.BlockSpec((1, seq // TK, BRANCH_WIDTH, TK), lambda b, i: (b, 0, 0, 0))
    out = jax.ShapeDtypeStruct((batch, seq, BRANCH_WIDTH), BF16)
    return pl.pallas_call(
        functools.partial(_attn_kernel, lam_init),
        grid=(batch, seq // (Q_BLOCKS * TQ)),
        in_specs=[
            pl.BlockSpec(lam_vecs.shape, const2),
            pl.BlockSpec(g_subln.shape, const2),
            pl.BlockSpec(bias_near.shape, const4),
            pl.BlockSpec(causal.shape, const2),
            pl.BlockSpec(pick.shape, lambda b, i: (0, 0, 0)),
            qsel, full, full_v, qsel, full, full_v,
            pl.BlockSpec((1, seq, LANES), lambda b, i: (b, 0, 0)),
        ],
        out_specs=[qblk, qblk],
        out_shape=[out, out],
        scratch_shapes=[
            pltpu.VMEM((Q_BLOCKS * CHAINS, 1, TQ), F32),
            pltpu.VMEM((Q_BLOCKS * DIFF_CHAINS, LANES + SUM_ROWS, TQ), F32),
            pltpu.VMEM((Q_BLOCKS * FOX_HEADS, HEAD_DIM + SUM_ROWS, TQ), F32),
        ],
        compiler_params=pltpu.CompilerParams(
            dimension_semantics=("parallel", "arbitrary"), vmem_limit_bytes=VMEM_LIMIT),
        name="attention",
    )(lam_vecs, g_subln, bias_near, causal, pick, qd, kd, vd, qf, kf, vf, decay)


def _merge_kernel(x_ref, od_ref, of_ref, g_ref, wg_ref, wpa_ref, wpb_ref, wo_ref, y_ref):
    x = x_ref[...]
    h = _rms(x, g_ref[...]).astype(BF16)
    a = jnp.dot(od_ref[...], wpa_ref[...], preferred_element_type=F32)
    b = jnp.dot(of_ref[...], wpb_ref[...], preferred_element_type=F32)
    ga = jax.nn.sigmoid(jnp.dot(h, wg_ref[:, :D_MODEL], preferred_element_type=F32))
    merged = ga * a
    gb = jax.nn.sigmoid(jnp.dot(h, wg_ref[:, D_MODEL:], preferred_element_type=F32))
    merged = (merged + gb * b).astype(BF16)
    y_ref[...] = x + jnp.dot(merged, wo_ref[...], preferred_element_type=F32)


def _merge(x2, od, of, g_mix, w_gate, w_pa, w_pb, w_o):
    n = x2.shape[0]
    tm = TM_PROJ
    const = lambda i: (0, 0)
    row = lambda i: (i, 0)
    return pl.pallas_call(
        _merge_kernel,
        grid=(n // tm,),
        in_specs=[
            pl.BlockSpec((tm, D_MODEL), row),
            pl.BlockSpec((tm, BRANCH_WIDTH), row),
            pl.BlockSpec((tm, BRANCH_WIDTH), row),
            pl.BlockSpec((1, D_MODEL), const),
            pl.BlockSpec(w_gate.shape, const),
            pl.BlockSpec(w_pa.shape, const),
            pl.BlockSpec(w_pb.shape, const),
            pl.BlockSpec(w_o.shape, const),
        ],
        out_specs=pl.BlockSpec((tm, D_MODEL), row),
        out_shape=jax.ShapeDtypeStruct((n, D_MODEL), F32),
        compiler_params=pltpu.CompilerParams(
            dimension_semantics=("parallel",), vmem_limit_bytes=VMEM_LIMIT),
        name="merge",
    )(x2, od, of, g_mix, w_gate, w_pa, w_pb, w_o)


FF_CHUNK = 1024


def _mlp_kernel(final_norm, x_ref, g_ref, w1_ref, w2_ref, gf_ref, y_ref):
    x = x_ref[...]
    h = _rms(x, g_ref[...]).astype(BF16)
    y = x
    for c in range(D_FF // FF_CHUNK):
        cols = slice(c * FF_CHUNK, (c + 1) * FF_CHUNK)
        u = jnp.maximum(jnp.dot(h, w1_ref[:, cols], preferred_element_type=F32), 0.0)
        y = y + jnp.dot((u * u).astype(BF16), w2_ref[cols, :], preferred_element_type=F32)
    y_ref[...] = _rms(y, gf_ref[...]) if final_norm else y


def _mlp(x2, g_mlp, w_1, w_2, g_final, final_norm):
    n = x2.shape[0]
    tm = TM_PROJ
    const = lambda i: (0, 0)
    row = lambda i: (i, 0)
    single = pl.Buffered(1)
    return pl.pallas_call(
        functools.partial(_mlp_kernel, final_norm),
        grid=(n // tm,),
        in_specs=[
            pl.BlockSpec((tm, D_MODEL), row),
            pl.BlockSpec((1, D_MODEL), const),
            pl.BlockSpec(w_1.shape, const, pipeline_mode=single),
            pl.BlockSpec(w_2.shape, const, pipeline_mode=single),
            pl.BlockSpec((1, D_MODEL), const),
        ],
        out_specs=pl.BlockSpec((tm, D_MODEL), row),
        out_shape=jax.ShapeDtypeStruct((n, D_MODEL), F32),
        compiler_params=pltpu.CompilerParams(
            dimension_semantics=("parallel",), vmem_limit_bytes=VMEM_LIMIT),
        name="mlp",
    )(x2, g_mlp, w_1, w_2, g_final)


def _layer(x, layer_idx, g_mix, w_in, b_f, lam_q1, lam_k1, lam_q2, lam_k2, g_subln,
           w_pa, w_pb, w_o, g_mlp, w_1, w_2, bias_near, g_final, final_norm):
    batch, seq, d = x.shape
    n = batch * seq
    x2 = x.reshape(n, d)
    qkv_cols = 6 * BRANCH_WIDTH
    pad = LANES - DECAY_PARTS * FOX_HEADS
    w_fl = jnp.pad(jnp.tile(w_in[:, qkv_cols:qkv_cols + FOX_HEADS], (1, DECAY_PARTS)),
                   ((0, 0), (0, pad))).astype(BF16)
    b_fl = jnp.pad(jnp.tile(b_f, DECAY_PARTS), (0, pad)).reshape(1, LANES)
    w_gate = w_in[:, qkv_cols + FOX_HEADS:].astype(BF16)
    g_mix2 = g_mix.reshape(1, d)

    qd, kd, vd, qf, kf, vf, decay = _in_proj(x2, g_mix2, w_in, w_fl, b_fl, seq)

    lam_vecs = jnp.stack([lam_q1, lam_k1, lam_q2, lam_k2]).astype(F32)
    shape3 = (batch, seq, BRANCH_WIDTH)
    shape4 = (2,) + shape3
    od, of = _attention(lam_vecs, g_subln.reshape(1, LANES), bias_near,
                        qd.reshape(shape4), kd.reshape(shape3), vd,
                        qf.reshape(shape4), kf.reshape(shape3), vf,
                        decay.reshape(batch, seq, LANES), _lambda_init(layer_idx))

    x1 = _merge(x2, od.reshape(n, BRANCH_WIDTH), of.reshape(n, BRANCH_WIDTH), g_mix2,
                w_gate, w_pa.astype(BF16), w_pb.astype(BF16), w_o.astype(BF16))
    y = _mlp(x1, g_mlp.reshape(1, d), w_1.astype(BF16), w_2.astype(BF16), g_final, final_norm)
    return y.reshape(batch, seq, d)


def kernel(x, g_mix, w_in, b_f, lam_q1, lam_k1, lam_q2, lam_k2, g_subln, w_pa, w_pb, w_o,
           g_mlp, w_1, w_2, rel_table, g_final):
    depth = g_mix.shape[0]
    bias_near = _bias_tiles(rel_table, x.shape[1])
    for l in range(depth):
        x = _layer(x, l, g_mix[l], w_in[l], b_f[l], lam_q1[l], lam_k1[l], lam_q2[l], lam_k2[l],
                   g_subln[l], w_pa[l], w_pb[l], w_o[l], g_mlp[l], w_1[l], w_2[l],
                   bias_near, g_final.reshape(1, -1), l == depth - 1)
    return x
```

```python
import functools
import math

import numpy as np
import jax
import jax.numpy as jnp
from jax import lax
from jax.experimental import pallas as pl
from jax.experimental.pallas import tpu as pltpu

D_MODEL = 1024
CHUNK = 64
HEAD_DIM = 64
DIFF_HEADS = 4
FOX_HEADS = 8
BRANCH_WIDTH = 512
D_FF = 4 * D_MODEL
REL_BUCKETS = 32
REL_MAX_DIST = 128
EPS = 1e-6
LANES = 128
MASKED_BUCKET = REL_BUCKETS

TQ = 256
TK = 256
TM_IN_PROJ = 512
TM_PROJ = 1024
V7X_VMEM_BYTES = 64 * 1024 * 1024
VMEM_LIMIT = V7X_VMEM_BYTES * 7 // 8

LOG2E = math.log2(math.e)
Q_SCALE = HEAD_DIM ** -0.5 * LOG2E

F32 = jnp.float32
BF16 = jnp.bfloat16
NT_DIMS = (((1,), (1,)), ((), ()))
TN_DIMS = (((0,), (0,)), ((), ()))


def _lambda_init(layer_idx):
    return 0.8 - 0.6 * math.exp(-0.3 * layer_idx)


def _rms(xf, g):
    return xf * lax.rsqrt(jnp.mean(xf * xf, axis=-1, keepdims=True) + EPS) * g


DECAY_PARTS = 3


def _in_proj_kernel(tiles_per_seq, x0_ref, xnext_ref, g_ref, w32_ref, wfl_ref, bf_ref,
                    qd_ref, kd_ref, vd_ref, qf_ref, kf_ref, vf_ref, dec_ref,
                    carry_ref, w_ref, h_ref):
    step = pl.program_id(0)

    @pl.when(step == 0)
    def _():
        w_ref[...] = w32_ref[...].astype(BF16)
        h_ref[0] = _rms(x0_ref[...], g_ref[...]).astype(BF16)

    h = h_ref[step % 2]

    @pl.when(step % tiles_per_seq == 0)
    def _():
        carry_ref[...] = jnp.zeros_like(carry_ref)

    z = jnp.dot(h, wfl_ref[...], preferred_element_type=F32) + bf_ref[...]
    acc = jnp.minimum(z, 0.0) - jnp.log1p(jnp.exp(-jnp.abs(z)))
    rows = acc.shape[0]
    row = lax.broadcasted_iota(jnp.int32, acc.shape, 0)
    d = 1
    while d < rows:
        acc = acc + jnp.where(row >= d, pltpu.roll(acc, d, axis=0), 0.0)
        d *= 2
    acc = acc + carry_ref[...]
    carry_ref[...] = acc[rows - 1:rows, :]
    neg = acc * -LOG2E
    hi = neg.astype(BF16).astype(F32)
    mid = (neg - hi).astype(BF16).astype(F32)
    lo = neg - hi - mid
    lane = lax.broadcasted_iota(jnp.int32, acc.shape, 1)
    piece = jnp.where(lane < FOX_HEADS, hi, jnp.where(lane < 2 * FOX_HEADS, mid, lo))
    dec_ref[...] = jnp.where(lane < DECAY_PARTS * FOX_HEADS, piece, 0.0).astype(BF16)

    outs = (qd_ref, kd_ref, vd_ref, qf_ref, kf_ref, vf_ref)
    for c in (2, 5, 0, 3, 1, 4):
        o_ref = outs[c]
        w = w_ref[:, c * BRANCH_WIDTH:(c + 1) * BRANCH_WIDTH]
        o = jnp.dot(h, w, preferred_element_type=F32)
        if o_ref is qd_ref or o_ref is qf_ref:
            o = (o * Q_SCALE).astype(BF16)
            low = lax.broadcasted_iota(jnp.int32, o.shape, 1) % LANES < HEAD_DIM
            o_ref[0] = jnp.where(low, o, jnp.zeros_like(o))
            o_ref[1] = jnp.where(low, jnp.zeros_like(o), o)
        elif o_ref is vd_ref or o_ref is vf_ref:
            o_t = o.T.astype(BF16)
            for t in range(o_ref.shape[1]):
                o_ref[0, t] = o_t[:, t * TK:(t + 1) * TK]
        else:
            o_ref[...] = o.astype(BF16)

    h_ref[(step + 1) % 2] = _rms(xnext_ref[...], g_ref[...]).astype(BF16)


def _in_proj(x2, g_mix, w_in, w_fl, b_f, seq):
    n = x2.shape[0]
    tm = TM_IN_PROJ
    qkv_cols = 6 * BRANCH_WIDTH
    const = lambda i: (0, 0)
    row = lambda i: (i, 0)
    out_bf = jax.ShapeDtypeStruct((n, BRANCH_WIDTH), BF16)
    out_q = jax.ShapeDtypeStruct((2, n, BRANCH_WIDTH), BF16)
    spec_bf = pl.BlockSpec((tm, BRANCH_WIDTH), row)
    spec_q = pl.BlockSpec((2, tm, BRANCH_WIDTH), lambda i: (0, i, 0))
    tiles_per_seq = seq // tm
    out_v = jax.ShapeDtypeStruct((n // seq, seq // TK, BRANCH_WIDTH, TK), BF16)
    spec_v = pl.BlockSpec((1, tm // TK, BRANCH_WIDTH, TK),
                          lambda i: (i // tiles_per_seq, i % tiles_per_seq, 0, 0))
    return pl.pallas_call(
        functools.partial(_in_proj_kernel, seq // tm),
        grid=(n // tm,),
        in_specs=[
            pl.BlockSpec((tm, D_MODEL), const),
            pl.BlockSpec((tm, D_MODEL), lambda i: (jnp.minimum(i + 1, n // tm - 1), 0)),
            pl.BlockSpec((1, D_MODEL), const),
            pl.BlockSpec((D_MODEL, qkv_cols), const, pipeline_mode=pl.Buffered(1)),
            pl.BlockSpec(w_fl.shape, const),
            pl.BlockSpec((1, LANES), const),
        ],
        out_specs=[spec_q, spec_bf, spec_v, spec_q, spec_bf, spec_v,
                   pl.BlockSpec((tm, LANES), row)],
        out_shape=[out_q, out_bf, out_v, out_q, out_bf, out_v,
                   jax.ShapeDtypeStruct((n, LANES), BF16)],
        scratch_shapes=[pltpu.VMEM((1, LANES), F32),
                        pltpu.VMEM((D_MODEL, qkv_cols), BF16),
                        pltpu.VMEM((2, tm, D_MODEL), BF16)],
        compiler_params=pltpu.CompilerParams(
            dimension_semantics=("arbitrary",), vmem_limit_bytes=VMEM_LIMIT),
        name="in_proj",
    )(x2, x2, g_mix, w_in, w_fl, b_f)


def _rel_bucket_np(rel):
    nb = REL_BUCKETS // 2
    ret = np.where(rel > 0, nb, 0)
    n = np.abs(rel)
    max_exact = nb // 2
    nf = np.maximum(n, 1).astype(np.float64)
    large = max_exact + (np.log(nf / max_exact) / math.log(REL_MAX_DIST / max_exact)
                         * (nb - max_exact)).astype(np.int32)
    large = np.minimum(large, nb - 1)
    return (ret + np.where(n < max_exact, n, large)).astype(np.int32)


def _bias_index_maps(seq):
    kk = np.arange(TK, dtype=np.int64)[:, None]
    qq = np.arange(TQ, dtype=np.int64)[None, :]
    diag = _rel_bucket_np(kk - qq)
    diag = np.where(kk // CHUNK <= qq // CHUNK, diag, MASKED_BUCKET).astype(np.int32)
    prev = _rel_bucket_np(kk - TK - qq)
    far = _rel_bucket_np(np.arange(-seq, -TK, dtype=np.int64))
    far_bucket = int(far[0])
    assert (far == far_bucket).all(), "keys two tiles back must share one bucket"
    return diag, prev, far_bucket


def _bias_kernel(far_bucket, tab_ref, idx_ref, out_ref):
    for h in range(DIFF_HEADS):
        far = tab_ref[far_bucket, h]
        for t in range(idx_ref.shape[0]):
            idx = idx_ref[t]
            acc = jnp.full(idx.shape, -jnp.inf, F32)
            for b in range(REL_BUCKETS):
                acc = jnp.where(idx == b, (tab_ref[b, h] - far) * LOG2E, acc)
            out_ref[t, h] = acc


def _bias_tiles(rel_table, seq):
    diag, prev, far_bucket = _bias_index_maps(seq)
    idx = np.stack([diag, prev])
    vmem = pl.BlockSpec(memory_space=pltpu.VMEM)
    return pl.pallas_call(
        functools.partial(_bias_kernel, far_bucket),
        in_specs=[pl.BlockSpec(memory_space=pltpu.SMEM), vmem],
        out_specs=vmem,
        out_shape=jax.ShapeDtypeStruct((idx.shape[0], DIFF_HEADS, TK, TQ), F32),
        name="bias_tiles",
    )(rel_table, jnp.asarray(idx))


DIFF_CHAINS = 2 * DIFF_HEADS
CHAINS = DIFF_CHAINS + FOX_HEADS
QK_AHEAD = 8
Q_BLOCKS = 4
SUM_ROWS = 16


def _attn_kernel(lam_init, lam_ref, gsub_ref, bnear_ref, causal_ref, pick_ref,
                 qd_ref, kd_ref, vd_ref, qf_ref, kf_ref, vf_ref, dec_ref,
                 od_ref, of_ref, m_ref, accd_ref, accf_ref):
    g = pl.program_id(1)
    lam_v = lam_ref[...]
    lam = (jnp.exp(jnp.sum(lam_v[0:1] * lam_v[1:2], axis=-1, keepdims=True))
           - jnp.exp(jnp.sum(lam_v[2:3] * lam_v[3:4], axis=-1, keepdims=True))
           + lam_init)

    def is_fox(c):
        return c >= DIFF_CHAINS

    def cols(c):
        blk = (c % DIFF_CHAINS) // 2
        return slice(blk * LANES, (blk + 1) * LANES)

    def rows(qb):
        return slice(qb * TQ, (qb + 1) * TQ)

    def accumulator(qb, c):
        if is_fox(c):
            return accf_ref.at[qb * FOX_HEADS + c - DIFF_CHAINS]
        return accd_ref.at[qb * DIFF_CHAINS + c]

    def normalised(qb, c):
        acc = accumulator(qb, c)[...]
        chans = acc.shape[0] - SUM_ROWS
        return acc[:chans] * (1.0 / acc[chans:chans + 1])

    def run(seq):
        tiles = {}

        def load(kind, tile, c):
            if kind == "v":
                chan = (slice((c - DIFF_CHAINS) * HEAD_DIM, (c - DIFF_CHAINS + 1) * HEAD_DIM)
                        if is_fox(c) else cols(c))
                key = (kind, id(tile), is_fox(c), chan.start)
                if key not in tiles:
                    v_t = (vf_ref if is_fox(c) else vd_ref)[0, tile, chan, :]
                    tiles[key] = jnp.concatenate([v_t, jnp.ones((SUM_ROWS, TK), BF16)], axis=0)
                return tiles[key]
            key = (kind, id(tile), is_fox(c), cols(c).start)
            if key not in tiles:
                at = pl.ds(pl.multiple_of(tile * TK, TK), TK)
                if is_fox(c):
                    tiles[key] = jnp.concatenate([kf_ref[0, at, cols(c)], dec_ref[0, at, :]],
                                                 axis=1)
                else:
                    tiles[key] = kd_ref[0, at, cols(c)]
            return tiles[key]

        def scores(tile, qb, c, add, first):
            if is_fox(c):
                q_t = jnp.concatenate([qf_ref[c % 2, 0, rows(qb), cols(c)],
                                       pick_ref[c - DIFF_CHAINS]], axis=1)
            else:
                q_t = qd_ref[c % 2, 0, rows(qb), cols(c)]
            s = lax.dot_general(load("k", tile, c), q_t, NT_DIMS, preferred_element_type=F32)
            return s if add is None else s + add()

        pending = {j: scores(*seq[j]) for j in range(min(QK_AHEAD, len(seq)))}
        for j, (tile, qb, c, _, first) in enumerate(seq):
            s = pending.pop(j)
            state = qb * CHAINS + c
            m_new = jnp.max(s, axis=0, keepdims=True)
            if not first:
                m_old = m_ref[state]
                m_new = jnp.maximum(m_old, m_new)
                alpha = jnp.exp2(m_old - m_new)
            p = jnp.exp2(s - m_new).astype(BF16)
            pv = jnp.dot(load("v", tile, c), p, preferred_element_type=F32)
            if j + QK_AHEAD < len(seq):
                pending[j + QK_AHEAD] = scores(*seq[j + QK_AHEAD])
            acc = accumulator(qb, c)
            m_ref[state] = m_new
            acc[...] = pv if first else alpha * acc[...] + pv

    def bias(kind, c):
        if kind == "diag":
            return (lambda: causal_ref[...]) if is_fox(c) else (lambda: bnear_ref[0, c // 2])
        if kind == "prev" and not is_fox(c):
            return lambda: bnear_ref[1, c // 2]
        return None

    def steps(tile, kinds):
        order = [c0 + b for c0 in range(0, DIFF_CHAINS, 2) for b in (0, DIFF_CHAINS)]
        return [(tile, qb, c0 + e, bias(kind, c0 + e), kind == "diag")
                for c0 in order for qb, kind in kinds.items() for e in range(2)]

    base = Q_BLOCKS * g
    head = []
    for u in reversed(range(Q_BLOCKS)):
        head += steps(base + u, {j: "diag" if j == u else "prev" if j == u + 1 else "far"
                                 for j in range(u, Q_BLOCKS)})
    run(head)

    def body(r, carry):
        hi = base - 1 - 2 * r
        lo = hi - 1
        far = {j: "far" for j in range(Q_BLOCKS)}
        pl.when(r == 0)(lambda: run(steps(hi, {**far, 0: "prev"}) + steps(lo, far)))
        pl.when(r > 0)(lambda: run(steps(hi, far) + steps(lo, far)))
        return carry

    lax.fori_loop(0, base // 2, body, 0)

    for qb in range(Q_BLOCKS):
        for h in range(DIFF_HEADS):
            o = normalised(qb, 2 * h) - lam * normalised(qb, 2 * h + 1)
            o = o * lax.rsqrt(jnp.mean(o * o, axis=0, keepdims=True) + EPS) * gsub_ref[...]
            od_ref[0, h * LANES:(h + 1) * LANES, rows(qb)] = o.astype(BF16)
        for h in range(FOX_HEADS):
            o = normalised(qb, DIFF_CHAINS + h)
            of_ref[0, h * HEAD_DIM:(h + 1) * HEAD_DIM, rows(qb)] = o.astype(BF16)


def _attention(lam_vecs, g_subln, bias_near, qd, kd, vd, qf, kf, vf, decay, lam_init):
    batch, seq, _ = kd.shape
    const2 = lambda b, i: (0, 0)
    const4 = lambda b, i: (0, 0, 0, 0)
    kk = np.arange(TK)[:, None]
    qq = np.arange(TQ)[None, :]
    causal = jnp.asarray(np.where(kk <= qq, 0.0, -np.inf).astype(np.float32))
    lane = np.arange(LANES)
    pick = (lane[None] < DECAY_PARTS * FOX_HEADS) & (lane[None] % FOX_HEADS
                                                     == np.arange(FOX_HEADS)[:, None])
    pick = jnp.asarray(np.broadcast_to(pick[:, None, :], (FOX_HEADS, TQ, LANES)), BF16)
    g_rows = jnp.broadcast_to((g_subln * (1.0 - lam_init)).reshape(LANES, 1), (LANES, TQ))
    oblk = pl.BlockSpec((1, BRANCH_WIDTH, Q_BLOCKS * TQ), lambda b, i: (b, 0, i))
    qsel = pl.BlockSpec((2, 1, Q_BLOCKS * TQ, BRANCH_WIDTH), lambda b, i: (0, b, i, 0))
    full = pl.BlockSpec((1, seq, BRANCH_WIDTH), lambda b, i: (b, 0, 0))
    full_v = pl.BlockSpec((1, seq // TK, BRANCH_WIDTH, TK), lambda b, i: (b, 0, 0, 0))
    out = jax.ShapeDtypeStruct((batch, BRANCH_WIDTH, seq), BF16)
    return pl.pallas_call(
        functools.partial(_attn_kernel, lam_init),
        grid=(batch, seq // (Q_BLOCKS * TQ)),
        in_specs=[
            pl.BlockSpec(lam_vecs.shape, const2),
            pl.BlockSpec(g_rows.shape, const2),
            pl.BlockSpec(bias_near.shape, const4),
            pl.BlockSpec(causal.shape, const2),
            pl.BlockSpec(pick.shape, lambda b, i: (0, 0, 0)),
            qsel, full, full_v, qsel, full, full_v,
            pl.BlockSpec((1, seq, LANES), lambda b, i: (b, 0, 0)),
        ],
        out_specs=[oblk, oblk],
        out_shape=[out, out],
        scratch_shapes=[
            pltpu.VMEM((Q_BLOCKS * CHAINS, 1, TQ), F32),
            pltpu.VMEM((Q_BLOCKS * DIFF_CHAINS, LANES + SUM_ROWS, TQ), F32),
            pltpu.VMEM((Q_BLOCKS * FOX_HEADS, HEAD_DIM + SUM_ROWS, TQ), F32),
        ],
        compiler_params=pltpu.CompilerParams(
            dimension_semantics=("parallel", "arbitrary"), vmem_limit_bytes=VMEM_LIMIT),
        name="attention",
    )(lam_vecs, g_rows, bias_near, causal, pick, qd, kd, vd, qf, kf, vf, decay)


def _merge_kernel(x_ref, od_ref, of_ref, g_ref, wg_ref, wpa_ref, wpb_ref, wo_ref, y_ref):
    x = x_ref[...]
    h = _rms(x, g_ref[...]).astype(BF16)
    a = lax.dot_general(od_ref[0], wpa_ref[...], TN_DIMS, preferred_element_type=F32)
    b = lax.dot_general(of_ref[0], wpb_ref[...], TN_DIMS, preferred_element_type=F32)
    ga = jax.nn.sigmoid(jnp.dot(h, wg_ref[:, :D_MODEL], preferred_element_type=F32))
    merged = ga * a
    gb = jax.nn.sigmoid(jnp.dot(h, wg_ref[:, D_MODEL:], preferred_element_type=F32))
    merged = (merged + gb * b).astype(BF16)
    y_ref[...] = x + jnp.dot(merged, wo_ref[...], preferred_element_type=F32)


def _merge(x2, od, of, g_mix, w_gate, w_pa, w_pb, w_o):
    n = x2.shape[0]
    tm = TM_PROJ
    const = lambda i: (0, 0)
    row = lambda i: (i, 0)
    tiles_per_seq = od.shape[2] // tm
    col = pl.BlockSpec((1, BRANCH_WIDTH, tm), lambda i: (i // tiles_per_seq, 0, i % tiles_per_seq))
    return pl.pallas_call(
        _merge_kernel,
        grid=(n // tm,),
        in_specs=[
            pl.BlockSpec((tm, D_MODEL), row),
            col,
            col,
            pl.BlockSpec((1, D_MODEL), const),
            pl.BlockSpec(w_gate.shape, const),
            pl.BlockSpec(w_pa.shape, const),
            pl.BlockSpec(w_pb.shape, const),
            pl.BlockSpec(w_o.shape, const),
        ],
        out_specs=pl.BlockSpec((tm, D_MODEL), row),
        out_shape=jax.ShapeDtypeStruct((n, D_MODEL), F32),
        compiler_params=pltpu.CompilerParams(
            dimension_semantics=("parallel",), vmem_limit_bytes=VMEM_LIMIT),
        name="merge",
    )(x2, od, of, g_mix, w_gate, w_pa, w_pb, w_o)


FF_CHUNK = 1024


def _mlp_kernel(final_norm, x_ref, g_ref, w1_ref, w2_ref, gf_ref, y_ref):
    x = x_ref[...]
    h = _rms(x, g_ref[...]).astype(BF16)
    y = x
    for c in range(D_FF // FF_CHUNK):
        cols = slice(c * FF_CHUNK, (c + 1) * FF_CHUNK)
        u = jnp.maximum(jnp.dot(h, w1_ref[:, cols], preferred_element_type=F32), 0.0)
        y = y + jnp.dot((u * u).astype(BF16), w2_ref[cols, :], preferred_element_type=F32)
    y_ref[...] = _rms(y, gf_ref[...]) if final_norm else y


def _mlp(x2, g_mlp, w_1, w_2, g_final, final_norm):
    n = x2.shape[0]
    tm = TM_PROJ
    const = lambda i: (0, 0)
    row = lambda i: (i, 0)
    single = pl.Buffered(1)
    return pl.pallas_call(
        functools.partial(_mlp_kernel, final_norm),
        grid=(n // tm,),
        in_specs=[
            pl.BlockSpec((tm, D_MODEL), row),
            pl.BlockSpec((1, D_MODEL), const),
            pl.BlockSpec(w_1.shape, const, pipeline_mode=single),
            pl.BlockSpec(w_2.shape, const, pipeline_mode=single),
            pl.BlockSpec((1, D_MODEL), const),
        ],
        out_specs=pl.BlockSpec((tm, D_MODEL), row),
        out_shape=jax.ShapeDtypeStruct((n, D_MODEL), F32),
        compiler_params=pltpu.CompilerParams(
            dimension_semantics=("parallel",), vmem_limit_bytes=VMEM_LIMIT),
        name="mlp",
    )(x2, g_mlp, w_1, w_2, g_final)


def _layer(x, layer_idx, g_mix, w_in, b_f, lam_q1, lam_k1, lam_q2, lam_k2, g_subln,
           w_pa, w_pb, w_o, g_mlp, w_1, w_2, bias_near, g_final, final_norm):
    batch, seq, d = x.shape
    n = batch * seq
    x2 = x.reshape(n, d)
    qkv_cols = 6 * BRANCH_WIDTH
    pad = LANES - DECAY_PARTS * FOX_HEADS
    w_tail = lax.optimization_barrier(w_in[:, qkv_cols:])
    w_fl = jnp.pad(jnp.tile(w_tail[:, :FOX_HEADS], (1, DECAY_PARTS)),
                   ((0, 0), (0, pad))).astype(BF16)
    b_fl = jnp.pad(jnp.tile(b_f, DECAY_PARTS), (0, pad)).reshape(1, LANES)
    w_gate = w_tail[:, FOX_HEADS:].astype(BF16)
    g_mix2 = g_mix.reshape(1, d)

    qd, kd, vd, qf, kf, vf, decay = _in_proj(x2, g_mix2, w_in, w_fl, b_fl, seq)

    lam_vecs = jnp.stack([lam_q1, lam_k1, lam_q2, lam_k2]).astype(F32)
    shape3 = (batch, seq, BRANCH_WIDTH)
    shape4 = (2,) + shape3
    od, of = _attention(lam_vecs, g_subln, bias_near,
                        qd.reshape(shape4), kd.reshape(shape3), vd,
                        qf.reshape(shape4), kf.reshape(shape3), vf,
                        decay.reshape(batch, seq, LANES), _lambda_init(layer_idx))

    x1 = _merge(x2, od, of, g_mix2,
                w_gate, w_pa.astype(BF16), w_pb.astype(BF16), w_o.astype(BF16))
    y = _mlp(x1, g_mlp.reshape(1, d), w_1.astype(BF16), w_2.astype(BF16), g_final, final_norm)
    return y.reshape(batch, seq, d)


def kernel(x, g_mix, w_in, b_f, lam_q1, lam_k1, lam_q2, lam_k2, g_subln, w_pa, w_pb, w_o,
           g_mlp, w_1, w_2, rel_table, g_final):
    depth = g_mix.shape[0]
    bias_near = _bias_tiles(rel_table, x.shape[1])
    for l in range(depth):
        x = _layer(x, l, g_mix[l], w_in[l], b_f[l], lam_q1[l], lam_k1[l], lam_q2[l], lam_k2[l],
                   g_subln[l], w_pa[l], w_pb[l], w_o[l], g_mlp[l], w_1[l], w_2[l],
                   bias_near, g_final.reshape(1, -1), l == depth - 1)
    return x
```

```python
import functools
import math

import numpy as np
import jax
import jax.numpy as jnp
from jax import lax
from jax.experimental import pallas as pl
from jax.experimental.pallas import tpu as pltpu

D_MODEL = 1024
CHUNK = 64
HEAD_DIM = 64
DIFF_HEADS = 4
FOX_HEADS = 8
BRANCH_WIDTH = 512
D_FF = 4 * D_MODEL
REL_BUCKETS = 32
REL_MAX_DIST = 128
EPS = 1e-6
LANES = 128
MASKED_BUCKET = REL_BUCKETS

TQ = 256
TK = 256
TM_IN_PROJ = 512
TM_PROJ = 1024
V7X_VMEM_BYTES = 64 * 1024 * 1024
VMEM_LIMIT = V7X_VMEM_BYTES * 7 // 8

LOG2E = math.log2(math.e)
Q_SCALE = HEAD_DIM ** -0.5 * LOG2E

F32 = jnp.float32
BF16 = jnp.bfloat16
NT_DIMS = (((1,), (1,)), ((), ()))
TN_DIMS = (((0,), (0,)), ((), ()))


def _lambda_init(layer_idx):
    return 0.8 - 0.6 * math.exp(-0.3 * layer_idx)


def _rms(xf, g):
    return xf * lax.rsqrt(jnp.mean(xf * xf, axis=-1, keepdims=True) + EPS) * g


DECAY_PARTS = 3


def _in_proj_kernel(tiles_per_seq, x0_ref, xnext_ref, g_ref, w32_ref, wfl_ref, bf_ref,
                    qd_ref, kd_ref, vd_ref, qf_ref, kf_ref, vf_ref, dec_ref,
                    carry_ref, w_ref, h_ref):
    step = pl.program_id(0)

    @pl.when(step == 0)
    def _():
        w_ref[...] = w32_ref[...].astype(BF16)
        h_ref[0] = _rms(x0_ref[...], g_ref[...]).astype(BF16)

    h = h_ref[step % 2]

    @pl.when(step % tiles_per_seq == 0)
    def _():
        carry_ref[...] = jnp.zeros_like(carry_ref)

    z = jnp.dot(h, wfl_ref[...], preferred_element_type=F32) + bf_ref[...]
    acc = jnp.minimum(z, 0.0) - jnp.log1p(jnp.exp(-jnp.abs(z)))
    rows = acc.shape[0]
    row = lax.broadcasted_iota(jnp.int32, acc.shape, 0)
    d = 1
    while d < rows:
        acc = acc + jnp.where(row >= d, pltpu.roll(acc, d, axis=0), 0.0)
        d *= 2
    acc = acc + carry_ref[...]
    carry_ref[...] = acc[rows - 1:rows, :]
    neg = acc * -LOG2E
    hi = neg.astype(BF16).astype(F32)
    mid = (neg - hi).astype(BF16).astype(F32)
    lo = neg - hi - mid
    lane = lax.broadcasted_iota(jnp.int32, acc.shape, 1)
    piece = jnp.where(lane < FOX_HEADS, hi, jnp.where(lane < 2 * FOX_HEADS, mid, lo))
    dec_ref[...] = jnp.where(lane < DECAY_PARTS * FOX_HEADS, piece, 0.0).astype(BF16)

    outs = (qd_ref, kd_ref, vd_ref, qf_ref, kf_ref, vf_ref)
    for c in (2, 5, 0, 3, 1, 4):
        o_ref = outs[c]
        w = w_ref[:, c * BRANCH_WIDTH:(c + 1) * BRANCH_WIDTH]
        o = jnp.dot(h, w, preferred_element_type=F32)
        if o_ref is qd_ref or o_ref is qf_ref:
            o = (o * Q_SCALE).astype(BF16)
            low = lax.broadcasted_iota(jnp.int32, o.shape, 1) % LANES < HEAD_DIM
            o_ref[0] = jnp.where(low, o, jnp.zeros_like(o))
            o_ref[1] = jnp.where(low, jnp.zeros_like(o), o)
        elif o_ref is vd_ref or o_ref is vf_ref:
            o_t = o.T.astype(BF16)
            for t in range(o_ref.shape[1]):
                o_ref[0, t] = o_t[:, t * TK:(t + 1) * TK]
        else:
            o_ref[...] = o.astype(BF16)

    h_ref[(step + 1) % 2] = _rms(xnext_ref[...], g_ref[...]).astype(BF16)


def _in_proj(x2, g_mix, w_in, w_fl, b_f, seq):
    n = x2.shape[0]
    tm = TM_IN_PROJ
    qkv_cols = 6 * BRANCH_WIDTH
    const = lambda i: (0, 0)
    row = lambda i: (i, 0)
    out_bf = jax.ShapeDtypeStruct((n, BRANCH_WIDTH), BF16)
    out_q = jax.ShapeDtypeStruct((2, n, BRANCH_WIDTH), BF16)
    spec_bf = pl.BlockSpec((tm, BRANCH_WIDTH), row)
    spec_q = pl.BlockSpec((2, tm, BRANCH_WIDTH), lambda i: (0, i, 0))
    tiles_per_seq = seq // tm
    out_v = jax.ShapeDtypeStruct((n // seq, seq // TK, BRANCH_WIDTH, TK), BF16)
    spec_v = pl.BlockSpec((1, tm // TK, BRANCH_WIDTH, TK),
                          lambda i: (i // tiles_per_seq, i % tiles_per_seq, 0, 0))
    return pl.pallas_call(
        functools.partial(_in_proj_kernel, seq // tm),
        grid=(n // tm,),
        in_specs=[
            pl.BlockSpec((tm, D_MODEL), const),
            pl.BlockSpec((tm, D_MODEL), lambda i: (jnp.minimum(i + 1, n // tm - 1), 0)),
            pl.BlockSpec((1, D_MODEL), const),
            pl.BlockSpec((D_MODEL, qkv_cols), const, pipeline_mode=pl.Buffered(1)),
            pl.BlockSpec(w_fl.shape, const),
            pl.BlockSpec((1, LANES), const),
        ],
        out_specs=[spec_q, spec_bf, spec_v, spec_q, spec_bf, spec_v,
                   pl.BlockSpec((tm, LANES), row)],
        out_shape=[out_q, out_bf, out_v, out_q, out_bf, out_v,
                   jax.ShapeDtypeStruct((n, LANES), BF16)],
        scratch_shapes=[pltpu.VMEM((1, LANES), F32),
                        pltpu.VMEM((D_MODEL, qkv_cols), BF16),
                        pltpu.VMEM((2, tm, D_MODEL), BF16)],
        compiler_params=pltpu.CompilerParams(
            dimension_semantics=("arbitrary",), vmem_limit_bytes=VMEM_LIMIT),
        name="in_proj",
    )(x2, x2, g_mix, w_in, w_fl, b_f)


def _rel_bucket_np(rel):
    nb = REL_BUCKETS // 2
    ret = np.where(rel > 0, nb, 0)
    n = np.abs(rel)
    max_exact = nb // 2
    nf = np.maximum(n, 1).astype(np.float64)
    large = max_exact + (np.log(nf / max_exact) / math.log(REL_MAX_DIST / max_exact)
                         * (nb - max_exact)).astype(np.int32)
    large = np.minimum(large, nb - 1)
    return (ret + np.where(n < max_exact, n, large)).astype(np.int32)


def _bias_index_maps(seq):
    kk = np.arange(TK, dtype=np.int64)[:, None]
    qq = np.arange(TQ, dtype=np.int64)[None, :]
    diag = _rel_bucket_np(kk - qq)
    diag = np.where(kk // CHUNK <= qq // CHUNK, diag, MASKED_BUCKET).astype(np.int32)
    prev = _rel_bucket_np(kk - TK - qq)
    far = _rel_bucket_np(np.arange(-seq, -TK, dtype=np.int64))
    far_bucket = int(far[0])
    assert (far == far_bucket).all(), "keys two tiles back must share one bucket"
    return diag, prev, far_bucket


def _bias_kernel(far_bucket, tab_ref, idx_ref, out_ref):
    for h in range(DIFF_HEADS):
        far = tab_ref[far_bucket, h]
        for t in range(idx_ref.shape[0]):
            idx = idx_ref[t]
            acc = jnp.full(idx.shape, -jnp.inf, F32)
            for b in range(REL_BUCKETS):
                acc = jnp.where(idx == b, (tab_ref[b, h] - far) * LOG2E, acc)
            out_ref[t, h] = acc


def _bias_tiles(rel_table, seq):
    diag, prev, far_bucket = _bias_index_maps(seq)
    idx = np.stack([diag, prev])
    vmem = pl.BlockSpec(memory_space=pltpu.VMEM)
    return pl.pallas_call(
        functools.partial(_bias_kernel, far_bucket),
        in_specs=[pl.BlockSpec(memory_space=pltpu.SMEM), vmem],
        out_specs=vmem,
        out_shape=jax.ShapeDtypeStruct((idx.shape[0], DIFF_HEADS, TK, TQ), F32),
        name="bias_tiles",
    )(rel_table, jnp.asarray(idx))


DIFF_CHAINS = 2 * DIFF_HEADS
CHAINS = DIFF_CHAINS + FOX_HEADS
QK_AHEAD = 8
Q_BLOCKS = 4
SUM_ROWS = 16


def _attn_kernel(lam_init, lam_ref, gsub_ref, bnear_ref, causal_ref, pick_ref,
                 qd_ref, kd_ref, vd_ref, qf_ref, kf_ref, vf_ref, dec_ref,
                 od_ref, of_ref, m_ref, accd_ref, accf_ref):
    g = pl.program_id(1)
    lam_v = lam_ref[...]
    lam = (jnp.exp(jnp.sum(lam_v[0:1] * lam_v[1:2], axis=-1, keepdims=True))
           - jnp.exp(jnp.sum(lam_v[2:3] * lam_v[3:4], axis=-1, keepdims=True))
           + lam_init)

    def is_fox(c):
        return c >= DIFF_CHAINS

    def cols(c):
        blk = (c % DIFF_CHAINS) // 2
        return slice(blk * LANES, (blk + 1) * LANES)

    def rows(qb):
        return slice(qb * TQ, (qb + 1) * TQ)

    def accumulator(qb, c):
        if is_fox(c):
            return accf_ref.at[qb * FOX_HEADS + c - DIFF_CHAINS]
        return accd_ref.at[qb * DIFF_CHAINS + c]

    def normalised(qb, c):
        acc = accumulator(qb, c)[...]
        chans = acc.shape[0] - SUM_ROWS
        return acc[:chans] * (1.0 / acc[chans:chans + 1])

    def run(seq):
        tiles = {}

        def load(kind, tile, c):
            if kind == "v":
                chan = (slice((c - DIFF_CHAINS) * HEAD_DIM, (c - DIFF_CHAINS + 1) * HEAD_DIM)
                        if is_fox(c) else cols(c))
                key = (kind, id(tile), is_fox(c), chan.start)
                if key not in tiles:
                    v_t = (vf_ref if is_fox(c) else vd_ref)[0, tile, chan, :]
                    tiles[key] = jnp.concatenate([v_t, jnp.ones((SUM_ROWS, TK), BF16)], axis=0)
                return tiles[key]
            key = (kind, id(tile), is_fox(c), cols(c).start)
            if key not in tiles:
                at = pl.ds(pl.multiple_of(tile * TK, TK), TK)
                if is_fox(c):
                    tiles[key] = jnp.concatenate([kf_ref[0, at, cols(c)], dec_ref[0, at, :]],
                                                 axis=1)
                else:
                    tiles[key] = kd_ref[0, at, cols(c)]
            return tiles[key]

        def scores(tile, qb, c, add, first):
            if is_fox(c):
                q_t = jnp.concatenate([qf_ref[c % 2, 0, rows(qb), cols(c)],
                                       pick_ref[c - DIFF_CHAINS]], axis=1)
            else:
                q_t = qd_ref[c % 2, 0, rows(qb), cols(c)]
            s = lax.dot_general(load("k", tile, c), q_t, NT_DIMS, preferred_element_type=F32)
            return s if add is None else s + add()

        pending = {j: scores(*seq[j]) for j in range(min(QK_AHEAD, len(seq)))}
        for j, (tile, qb, c, _, first) in enumerate(seq):
            s = pending.pop(j)
            state = qb * CHAINS + c
            m_new = jnp.max(s, axis=0, keepdims=True)
            if not first:
                m_old = m_ref[state]
                m_new = jnp.maximum(m_old, m_new)
                alpha = jnp.exp2(m_old - m_new)
            p = jnp.exp2(s - m_new).astype(BF16)
            pv = jnp.dot(load("v", tile, c), p, preferred_element_type=F32)
            if j + QK_AHEAD < len(seq):
                pending[j + QK_AHEAD] = scores(*seq[j + QK_AHEAD])
            acc = accumulator(qb, c)
            m_ref[state] = m_new
            acc[...] = pv if first else alpha * acc[...] + pv

    def bias(kind, c):
        if kind == "diag":
            return (lambda: causal_ref[...]) if is_fox(c) else (lambda: bnear_ref[0, c // 2])
        if kind == "prev" and not is_fox(c):
            return lambda: bnear_ref[1, c // 2]
        return None

    def steps(tile, kinds):
        order = [c0 + b for c0 in range(0, DIFF_CHAINS, 2) for b in (0, DIFF_CHAINS)]
        return [(tile, qb, c0 + e, bias(kind, c0 + e), kind == "diag")
                for c0 in order for qb, kind in kinds.items() for e in range(2)]

    base = Q_BLOCKS * g
    head = []
    for u in reversed(range(Q_BLOCKS)):
        head += steps(base + u, {j: "diag" if j == u else "prev" if j == u + 1 else "far"
                                 for j in range(u, Q_BLOCKS)})
    run(head)

    def body(r, carry):
        hi = base - 1 - 2 * r
        lo = hi - 1
        far = {j: "far" for j in range(Q_BLOCKS)}
        pl.when(r == 0)(lambda: run(steps(hi, {**far, 0: "prev"}) + steps(lo, far)))
        pl.when(r > 0)(lambda: run(steps(hi, far) + steps(lo, far)))
        return carry

    lax.fori_loop(0, base // 2, body, 0)

    for qb in range(Q_BLOCKS):
        for h in range(DIFF_HEADS):
            o = normalised(qb, 2 * h) - lam * normalised(qb, 2 * h + 1)
            o = o * lax.rsqrt(jnp.mean(o * o, axis=0, keepdims=True) + EPS) * gsub_ref[...]
            od_ref[0, h * LANES:(h + 1) * LANES, rows(qb)] = o.astype(BF16)
        for h in range(FOX_HEADS):
            o = normalised(qb, DIFF_CHAINS + h)
            of_ref[0, h * HEAD_DIM:(h + 1) * HEAD_DIM, rows(qb)] = o.astype(BF16)


def _attention(lam_vecs, g_subln, bias_near, qd, kd, vd, qf, kf, vf, decay, lam_init):
    batch, seq, _ = kd.shape
    const2 = lambda b, i: (0, 0)
    const4 = lambda b, i: (0, 0, 0, 0)
    kk = np.arange(TK)[:, None]
    qq = np.arange(TQ)[None, :]
    causal = jnp.asarray(np.where(kk <= qq, 0.0, -np.inf).astype(np.float32))
    lane = np.arange(LANES)
    pick = (lane[None] < DECAY_PARTS * FOX_HEADS) & (lane[None] % FOX_HEADS
                                                     == np.arange(FOX_HEADS)[:, None])
    pick = jnp.asarray(np.broadcast_to(pick[:, None, :], (FOX_HEADS, TQ, LANES)), BF16)
    g_rows = jnp.broadcast_to((g_subln * (1.0 - lam_init)).reshape(LANES, 1), (LANES, TQ))
    oblk = pl.BlockSpec((1, BRANCH_WIDTH, Q_BLOCKS * TQ), lambda b, i: (b, 0, i))
    qsel = pl.BlockSpec((2, 1, Q_BLOCKS * TQ, BRANCH_WIDTH), lambda b, i: (0, b, i, 0))
    full = pl.BlockSpec((1, seq, BRANCH_WIDTH), lambda b, i: (b, 0, 0))
    full_v = pl.BlockSpec((1, seq // TK, BRANCH_WIDTH, TK), lambda b, i: (b, 0, 0, 0))
    out = jax.ShapeDtypeStruct((batch, BRANCH_WIDTH, seq), BF16)
    return pl.pallas_call(
        functools.partial(_attn_kernel, lam_init),
        grid=(batch, seq // (Q_BLOCKS * TQ)),
        in_specs=[
            pl.BlockSpec(lam_vecs.shape, const2),
            pl.BlockSpec(g_rows.shape, const2),
            pl.BlockSpec(bias_near.shape, const4),
            pl.BlockSpec(causal.shape, const2),
            pl.BlockSpec(pick.shape, lambda b, i: (0, 0, 0)),
            qsel, full, full_v, qsel, full, full_v,
            pl.BlockSpec((1, seq, LANES), lambda b, i: (b, 0, 0)),
        ],
        out_specs=[oblk, oblk],
        out_shape=[out, out],
        scratch_shapes=[
            pltpu.VMEM((Q_BLOCKS * CHAINS, 1, TQ), F32),
            pltpu.VMEM((Q_BLOCKS * DIFF_CHAINS, LANES + SUM_ROWS, TQ), F32),
            pltpu.VMEM((Q_BLOCKS * FOX_HEADS, HEAD_DIM + SUM_ROWS, TQ), F32),
        ],
        compiler_params=pltpu.CompilerParams(
            dimension_semantics=("parallel", "arbitrary"), vmem_limit_bytes=VMEM_LIMIT),
        name="attention",
    )(lam_vecs, g_rows, bias_near, causal, pick, qd, kd, vd, qf, kf, vf, decay)


def _merge_kernel(x_ref, od_ref, of_ref, g_ref, wtail_ref, wpa_ref, wpb_ref, wo_ref, y_ref,
                  wg_ref):
    @pl.when(pl.program_id(0) == 0)
    def _():
        wg_ref[...] = wtail_ref[:, FOX_HEADS:].astype(BF16)

    x = x_ref[...]
    h = _rms(x, g_ref[...]).astype(BF16)
    a = lax.dot_general(od_ref[0], wpa_ref[...], TN_DIMS, preferred_element_type=F32)
    b = lax.dot_general(of_ref[0], wpb_ref[...], TN_DIMS, preferred_element_type=F32)
    ga = jax.nn.sigmoid(jnp.dot(h, wg_ref[:, :D_MODEL], preferred_element_type=F32))
    merged = ga * a
    gb = jax.nn.sigmoid(jnp.dot(h, wg_ref[:, D_MODEL:], preferred_element_type=F32))
    merged = (merged + gb * b).astype(BF16)
    y_ref[...] = x + jnp.dot(merged, wo_ref[...], preferred_element_type=F32)


def _merge(x2, od, of, g_mix, w_tail, w_pa, w_pb, w_o):
    n = x2.shape[0]
    single = pl.Buffered(1)
    tm = TM_PROJ
    const = lambda i: (0, 0)
    row = lambda i: (i, 0)
    tiles_per_seq = od.shape[2] // tm
    col = pl.BlockSpec((1, BRANCH_WIDTH, tm), lambda i: (i // tiles_per_seq, 0, i % tiles_per_seq))
    return pl.pallas_call(
        _merge_kernel,
        grid=(n // tm,),
        in_specs=[
            pl.BlockSpec((tm, D_MODEL), row),
            col,
            col,
            pl.BlockSpec((1, D_MODEL), const),
            pl.BlockSpec(w_tail.shape, const, pipeline_mode=single),
            pl.BlockSpec(w_pa.shape, const, pipeline_mode=single),
            pl.BlockSpec(w_pb.shape, const, pipeline_mode=single),
            pl.BlockSpec(w_o.shape, const, pipeline_mode=single),
        ],
        out_specs=pl.BlockSpec((tm, D_MODEL), row),
        out_shape=jax.ShapeDtypeStruct((n, D_MODEL), F32),
        scratch_shapes=[pltpu.VMEM((D_MODEL, w_tail.shape[1] - FOX_HEADS), BF16)],
        compiler_params=pltpu.CompilerParams(
            dimension_semantics=("arbitrary",), vmem_limit_bytes=VMEM_LIMIT),
        name="merge",
    )(x2, od, of, g_mix, w_tail, w_pa, w_pb, w_o)


FF_CHUNK = 1024


def _mlp_kernel(final_norm, x_ref, g_ref, w1_ref, w2_ref, gf_ref, y_ref):
    x = x_ref[...]
    h = _rms(x, g_ref[...]).astype(BF16)
    y = x
    for c in range(D_FF // FF_CHUNK):
        cols = slice(c * FF_CHUNK, (c + 1) * FF_CHUNK)
        u = jnp.maximum(jnp.dot(h, w1_ref[:, cols], preferred_element_type=F32), 0.0)
        y = y + jnp.dot((u * u).astype(BF16), w2_ref[cols, :], preferred_element_type=F32)
    y_ref[...] = _rms(y, gf_ref[...]) if final_norm else y


def _mlp(x2, g_mlp, w_1, w_2, g_final, final_norm):
    n = x2.shape[0]
    tm = TM_PROJ
    const = lambda i: (0, 0)
    row = lambda i: (i, 0)
    single = pl.Buffered(1)
    return pl.pallas_call(
        functools.partial(_mlp_kernel, final_norm),
        grid=(n // tm,),
        in_specs=[
            pl.BlockSpec((tm, D_MODEL), row),
            pl.BlockSpec((1, D_MODEL), const),
            pl.BlockSpec(w_1.shape, const, pipeline_mode=single),
            pl.BlockSpec(w_2.shape, const, pipeline_mode=single),
            pl.BlockSpec((1, D_MODEL), const),
        ],
        out_specs=pl.BlockSpec((tm, D_MODEL), row),
        out_shape=jax.ShapeDtypeStruct((n, D_MODEL), F32),
        compiler_params=pltpu.CompilerParams(
            dimension_semantics=("parallel",), vmem_limit_bytes=VMEM_LIMIT),
        name="mlp",
    )(x2, g_mlp, w_1, w_2, g_final)


def _layer(x, layer_idx, g_mix, w_in, b_f, lam_q1, lam_k1, lam_q2, lam_k2, g_subln,
           w_pa, w_pb, w_o, g_mlp, w_1, w_2, bias_near, g_final, final_norm):
    batch, seq, d = x.shape
    n = batch * seq
    x2 = x.reshape(n, d)
    qkv_cols = 6 * BRANCH_WIDTH
    pad = LANES - DECAY_PARTS * FOX_HEADS
    w_tail = w_in[:, qkv_cols:]
    w_fl = jnp.pad(jnp.tile(w_tail[:, :FOX_HEADS], (1, DECAY_PARTS)),
                   ((0, 0), (0, pad))).astype(BF16)
    b_fl = jnp.pad(jnp.tile(b_f, DECAY_PARTS), (0, pad)).reshape(1, LANES)
    g_mix2 = g_mix.reshape(1, d)

    qd, kd, vd, qf, kf, vf, decay = _in_proj(x2, g_mix2, w_in, w_fl, b_fl, seq)

    lam_vecs = jnp.stack([lam_q1, lam_k1, lam_q2, lam_k2]).astype(F32)
    shape3 = (batch, seq, BRANCH_WIDTH)
    shape4 = (2,) + shape3
    od, of = _attention(lam_vecs, g_subln, bias_near,
                        qd.reshape(shape4), kd.reshape(shape3), vd,
                        qf.reshape(shape4), kf.reshape(shape3), vf,
                        decay.reshape(batch, seq, LANES), _lambda_init(layer_idx))

    x1 = _merge(x2, od, of, g_mix2,
                w_tail, w_pa.astype(BF16), w_pb.astype(BF16), w_o.astype(BF16))
    y = _mlp(x1, g_mlp.reshape(1, d), w_1.astype(BF16), w_2.astype(BF16), g_final, final_norm)
    return y.reshape(batch, seq, d)


def kernel(x, g_mix, w_in, b_f, lam_q1, lam_k1, lam_q2, lam_k2, g_subln, w_pa, w_pb, w_o,
           g_mlp, w_1, w_2, rel_table, g_final):
    depth = g_mix.shape[0]
    bias_near = _bias_tiles(rel_table, x.shape[1])
    for l in range(depth):
        x = _layer(x, l, g_mix[l], w_in[l], b_f[l], lam_q1[l], lam_k1[l], lam_q2[l], lam_k2[l],
                   g_subln[l], w_pa[l], w_pb[l], w_o[l], g_mlp[l], w_1[l], w_2[l],
                   bias_near, g_final.reshape(1, -1), l == depth - 1)
    return x
```

```python
import functools
import math

import numpy as np
import jax
import jax.numpy as jnp
from jax import lax
from jax.experimental import pallas as pl
from jax.experimental.pallas import tpu as pltpu

D_MODEL = 1024
CHUNK = 64
HEAD_DIM = 64
DIFF_HEADS = 4
FOX_HEADS = 8
BRANCH_WIDTH = 512
D_FF = 4 * D_MODEL
REL_BUCKETS = 32
REL_MAX_DIST = 128
EPS = 1e-6
LANES = 128
MASKED_BUCKET = REL_BUCKETS

TQ = 256
TK = 256
TM_IN_PROJ = 512
TM_PROJ = 1024
V7X_VMEM_BYTES = 64 * 1024 * 1024
VMEM_LIMIT = V7X_VMEM_BYTES * 7 // 8

LOG2E = math.log2(math.e)
Q_SCALE = HEAD_DIM ** -0.5 * LOG2E

F32 = jnp.float32
BF16 = jnp.bfloat16
NT_DIMS = (((1,), (1,)), ((), ()))
TN_DIMS = (((0,), (0,)), ((), ()))


def _lambda_init(layer_idx):
    return 0.8 - 0.6 * math.exp(-0.3 * layer_idx)


def _rms(xf, g):
    return xf * lax.rsqrt(jnp.mean(xf * xf, axis=-1, keepdims=True) + EPS) * g


DECAY_PARTS = 3


def _in_proj_kernel(tiles_per_seq, x0_ref, xnext_ref, g_ref, w32_ref, wfl_ref, bf_ref,
                    qd_ref, kd_ref, vd_ref, qf_ref, kf_ref, vf_ref, dec_ref,
                    carry_ref, w_ref, h_ref):
    step = pl.program_id(0)

    @pl.when(step == 0)
    def _():
        w_ref[...] = w32_ref[...].astype(BF16)
        h_ref[0] = _rms(x0_ref[...], g_ref[...]).astype(BF16)

    h = h_ref[step % 2]

    @pl.when(step % tiles_per_seq == 0)
    def _():
        carry_ref[...] = jnp.zeros_like(carry_ref)

    z = jnp.dot(h, wfl_ref[...], preferred_element_type=F32) + bf_ref[...]
    acc = jnp.minimum(z, 0.0) - jnp.log1p(jnp.exp(-jnp.abs(z)))
    rows = acc.shape[0]
    row = lax.broadcasted_iota(jnp.int32, acc.shape, 0)
    d = 1
    while d < rows:
        acc = acc + jnp.where(row >= d, pltpu.roll(acc, d, axis=0), 0.0)
        d *= 2
    acc = acc + carry_ref[...]
    carry_ref[...] = acc[rows - 1:rows, :]
    neg = acc * -LOG2E
    hi = neg.astype(BF16).astype(F32)
    mid = (neg - hi).astype(BF16).astype(F32)
    lo = neg - hi - mid
    lane = lax.broadcasted_iota(jnp.int32, acc.shape, 1)
    piece = jnp.where(lane < FOX_HEADS, hi, jnp.where(lane < 2 * FOX_HEADS, mid, lo))
    dec_ref[...] = jnp.where(lane < DECAY_PARTS * FOX_HEADS, piece, 0.0).astype(BF16)

    outs = (qd_ref, kd_ref, vd_ref, qf_ref, kf_ref, vf_ref)
    for c in (2, 5, 0, 3, 1, 4):
        o_ref = outs[c]
        w = w_ref[:, c * BRANCH_WIDTH:(c + 1) * BRANCH_WIDTH]
        o = jnp.dot(h, w, preferred_element_type=F32)
        if o_ref is qd_ref or o_ref is qf_ref:
            o = (o * Q_SCALE).astype(BF16)
            low = lax.broadcasted_iota(jnp.int32, o.shape, 1) % LANES < HEAD_DIM
            o_ref[0] = jnp.where(low, o, jnp.zeros_like(o))
            o_ref[1] = jnp.where(low, jnp.zeros_like(o), o)
        elif o_ref is vd_ref or o_ref is vf_ref:
            o_t = o.T.astype(BF16)
            for t in range(o_ref.shape[1]):
                o_ref[0, t] = o_t[:, t * TK:(t + 1) * TK]
        else:
            o_ref[...] = o.astype(BF16)

    h_ref[(step + 1) % 2] = _rms(xnext_ref[...], g_ref[...]).astype(BF16)


def _in_proj(x2, g_mix, w_in, w_fl, b_f, seq):
    n = x2.shape[0]
    tm = TM_IN_PROJ
    qkv_cols = 6 * BRANCH_WIDTH
    const = lambda i: (0, 0)
    row = lambda i: (i, 0)
    out_bf = jax.ShapeDtypeStruct((n, BRANCH_WIDTH), BF16)
    out_q = jax.ShapeDtypeStruct((2, n, BRANCH_WIDTH), BF16)
    spec_bf = pl.BlockSpec((tm, BRANCH_WIDTH), row)
    spec_q = pl.BlockSpec((2, tm, BRANCH_WIDTH), lambda i: (0, i, 0))
    tiles_per_seq = seq // tm
    out_v = jax.ShapeDtypeStruct((n // seq, seq // TK, BRANCH_WIDTH, TK), BF16)
    spec_v = pl.BlockSpec((1, tm // TK, BRANCH_WIDTH, TK),
                          lambda i: (i // tiles_per_seq, i % tiles_per_seq, 0, 0))
    return pl.pallas_call(
        functools.partial(_in_proj_kernel, seq // tm),
        grid=(n // tm,),
        in_specs=[
            pl.BlockSpec((tm, D_MODEL), const),
            pl.BlockSpec((tm, D_MODEL), lambda i: (jnp.minimum(i + 1, n // tm - 1), 0)),
            pl.BlockSpec((1, D_MODEL), const),
            pl.BlockSpec((D_MODEL, qkv_cols), const, pipeline_mode=pl.Buffered(1)),
            pl.BlockSpec(w_fl.shape, const),
            pl.BlockSpec((1, LANES), const),
        ],
        out_specs=[spec_q, spec_bf, spec_v, spec_q, spec_bf, spec_v,
                   pl.BlockSpec((tm, LANES), row)],
        out_shape=[out_q, out_bf, out_v, out_q, out_bf, out_v,
                   jax.ShapeDtypeStruct((n, LANES), BF16)],
        scratch_shapes=[pltpu.VMEM((1, LANES), F32),
                        pltpu.VMEM((D_MODEL, qkv_cols), BF16),
                        pltpu.VMEM((2, tm, D_MODEL), BF16)],
        compiler_params=pltpu.CompilerParams(
            dimension_semantics=("arbitrary",), vmem_limit_bytes=VMEM_LIMIT),
        name="in_proj",
    )(x2, x2, g_mix, w_in, w_fl, b_f)


def _rel_bucket_np(rel):
    nb = REL_BUCKETS // 2
    ret = np.where(rel > 0, nb, 0)
    n = np.abs(rel)
    max_exact = nb // 2
    nf = np.maximum(n, 1).astype(np.float64)
    large = max_exact + (np.log(nf / max_exact) / math.log(REL_MAX_DIST / max_exact)
                         * (nb - max_exact)).astype(np.int32)
    large = np.minimum(large, nb - 1)
    return (ret + np.where(n < max_exact, n, large)).astype(np.int32)


def _bias_index_maps(seq):
    kk = np.arange(TK, dtype=np.int64)[:, None]
    qq = np.arange(TQ, dtype=np.int64)[None, :]
    diag = _rel_bucket_np(kk - qq)
    diag = np.where(kk // CHUNK <= qq // CHUNK, diag, MASKED_BUCKET).astype(np.int32)
    prev = _rel_bucket_np(kk - TK - qq)
    far = _rel_bucket_np(np.arange(-seq, -TK, dtype=np.int64))
    far_bucket = int(far[0])
    assert (far == far_bucket).all(), "keys two tiles back must share one bucket"
    return diag, prev, far_bucket


def _bias_kernel(far_bucket, tab_ref, idx_ref, out_ref):
    for h in range(DIFF_HEADS):
        far = tab_ref[far_bucket, h]
        for t in range(idx_ref.shape[0]):
            idx = idx_ref[t]
            acc = jnp.full(idx.shape, -jnp.inf, F32)
            for b in range(REL_BUCKETS):
                acc = jnp.where(idx == b, (tab_ref[b, h] - far) * LOG2E, acc)
            out_ref[t, h] = acc


def _bias_tiles(rel_table, seq):
    diag, prev, far_bucket = _bias_index_maps(seq)
    idx = np.stack([diag, prev])
    vmem = pl.BlockSpec(memory_space=pltpu.VMEM)
    return pl.pallas_call(
        functools.partial(_bias_kernel, far_bucket),
        in_specs=[pl.BlockSpec(memory_space=pltpu.SMEM), vmem],
        out_specs=vmem,
        out_shape=jax.ShapeDtypeStruct((idx.shape[0], DIFF_HEADS, TK, TQ), F32),
        name="bias_tiles",
    )(rel_table, jnp.asarray(idx))


DIFF_CHAINS = 2 * DIFF_HEADS
CHAINS = DIFF_CHAINS + FOX_HEADS
QK_AHEAD = 8
Q_BLOCKS = 4
SUM_ROWS = 16


def _attn_kernel(lam_init, lam_ref, gsub_ref, bnear_ref, causal_ref, pick_ref,
                 qd_ref, kd_ref, vd_ref, qf_ref, kf_ref, vf_ref, dec_ref,
                 od_ref, of_ref, m_ref, accd_ref, accf_ref):
    g = pl.program_id(1)
    lam_v = lam_ref[...]
    lam = (jnp.exp(jnp.sum(lam_v[0:1] * lam_v[1:2], axis=-1, keepdims=True))
           - jnp.exp(jnp.sum(lam_v[2:3] * lam_v[3:4], axis=-1, keepdims=True))
           + lam_init)

    def is_fox(c):
        return c >= DIFF_CHAINS

    def cols(c):
        blk = (c % DIFF_CHAINS) // 2
        return slice(blk * LANES, (blk + 1) * LANES)

    def rows(qb):
        return slice(qb * TQ, (qb + 1) * TQ)

    def accumulator(qb, c):
        if is_fox(c):
            return accf_ref.at[qb * FOX_HEADS + c - DIFF_CHAINS]
        return accd_ref.at[qb * DIFF_CHAINS + c]

    def normalised(qb, c):
        acc = accumulator(qb, c)[...]
        chans = acc.shape[0] - SUM_ROWS
        return acc[:chans] * (1.0 / acc[chans:chans + 1])

    def run(seq):
        tiles = {}

        def load(kind, tile, c):
            if kind == "v":
                chan = (slice((c - DIFF_CHAINS) * HEAD_DIM, (c - DIFF_CHAINS + 1) * HEAD_DIM)
                        if is_fox(c) else cols(c))
                key = (kind, id(tile), is_fox(c), chan.start)
                if key not in tiles:
                    v_t = (vf_ref if is_fox(c) else vd_ref)[0, tile, chan, :]
                    tiles[key] = jnp.concatenate([v_t, jnp.ones((SUM_ROWS, TK), BF16)], axis=0)
                return tiles[key]
            key = (kind, id(tile), is_fox(c), cols(c).start)
            if key not in tiles:
                at = pl.ds(pl.multiple_of(tile * TK, TK), TK)
                if is_fox(c):
                    tiles[key] = jnp.concatenate([kf_ref[0, at, cols(c)], dec_ref[0, at, :]],
                                                 axis=1)
                else:
                    tiles[key] = kd_ref[0, at, cols(c)]
            return tiles[key]

        def scores(tile, qb, c, add, first):
            if is_fox(c):
                q_t = jnp.concatenate([qf_ref[c % 2, 0, rows(qb), cols(c)],
                                       pick_ref[c - DIFF_CHAINS]], axis=1)
            else:
                q_t = qd_ref[c % 2, 0, rows(qb), cols(c)]
            s = lax.dot_general(load("k", tile, c), q_t, NT_DIMS, preferred_element_type=F32)
            return s if add is None else s + add()

        pending = {j: scores(*seq[j]) for j in range(min(QK_AHEAD, len(seq)))}
        for j, (tile, qb, c, _, first) in enumerate(seq):
            s = pending.pop(j)
            state = qb * CHAINS + c
            m_new = jnp.max(s, axis=0, keepdims=True)
            if not first:
                m_old = m_ref[state]
                m_new = jnp.maximum(m_old, m_new)
                alpha = jnp.exp2(m_old - m_new)
            p = jnp.exp2(s - m_new).astype(BF16)
            pv = jnp.dot(load("v", tile, c), p, preferred_element_type=F32)
            if j + QK_AHEAD < len(seq):
                pending[j + QK_AHEAD] = scores(*seq[j + QK_AHEAD])
            acc = accumulator(qb, c)
            m_ref[state] = m_new
            acc[...] = pv if first else alpha * acc[...] + pv

    def bias(kind, c):
        if kind == "diag":
            return (lambda: causal_ref[...]) if is_fox(c) else (lambda: bnear_ref[0, c // 2])
        if kind == "prev" and not is_fox(c):
            return lambda: bnear_ref[1, c // 2]
        return None

    def steps(tile, kinds):
        order = [c0 + b for c0 in range(0, DIFF_CHAINS, 2) for b in (0, DIFF_CHAINS)]
        return [(tile, qb, c0 + e, bias(kind, c0 + e), kind == "diag")
                for c0 in order for qb, kind in kinds.items() for e in range(2)]

    base = Q_BLOCKS * g
    head = []
    for u in reversed(range(Q_BLOCKS)):
        head += steps(base + u, {j: "diag" if j == u else "prev" if j == u + 1 else "far"
                                 for j in range(u, Q_BLOCKS)})
    run(head)

    def body(r, carry):
        hi = base - 1 - 2 * r
        lo = hi - 1
        far = {j: "far" for j in range(Q_BLOCKS)}
        pl.when(r == 0)(lambda: run(steps(hi, {**far, 0: "prev"}) + steps(lo, far)))
        pl.when(r > 0)(lambda: run(steps(hi, far) + steps(lo, far)))
        return carry

    lax.fori_loop(0, base // 2, body, 0)

    for qb in range(Q_BLOCKS):
        for h in range(DIFF_HEADS):
            o = normalised(qb, 2 * h) - lam * normalised(qb, 2 * h + 1)
            o = o * lax.rsqrt(jnp.mean(o * o, axis=0, keepdims=True) + EPS) * gsub_ref[...]
            od_ref[0, h * LANES:(h + 1) * LANES, rows(qb)] = o.astype(BF16)
        for h in range(FOX_HEADS):
            o = normalised(qb, DIFF_CHAINS + h)
            of_ref[0, h * HEAD_DIM:(h + 1) * HEAD_DIM, rows(qb)] = o.astype(BF16)


def _attention(lam_vecs, g_subln, bias_near, qd, kd, vd, qf, kf, vf, decay, lam_init):
    batch, seq, _ = kd.shape
    const2 = lambda b, i: (0, 0)
    const4 = lambda b, i: (0, 0, 0, 0)
    kk = np.arange(TK)[:, None]
    qq = np.arange(TQ)[None, :]
    causal = jnp.asarray(np.where(kk <= qq, 0.0, -np.inf).astype(np.float32))
    lane = np.arange(LANES)
    pick = (lane[None] < DECAY_PARTS * FOX_HEADS) & (lane[None] % FOX_HEADS
                                                     == np.arange(FOX_HEADS)[:, None])
    pick = jnp.asarray(np.broadcast_to(pick[:, None, :], (FOX_HEADS, TQ, LANES)), BF16)
    g_rows = jnp.broadcast_to((g_subln * (1.0 - lam_init)).reshape(LANES, 1), (LANES, TQ))
    oblk = pl.BlockSpec((1, BRANCH_WIDTH, Q_BLOCKS * TQ), lambda b, i: (b, 0, i))
    qsel = pl.BlockSpec((2, 1, Q_BLOCKS * TQ, BRANCH_WIDTH), lambda b, i: (0, b, i, 0))
    full = pl.BlockSpec((1, seq, BRANCH_WIDTH), lambda b, i: (b, 0, 0))
    full_v = pl.BlockSpec((1, seq // TK, BRANCH_WIDTH, TK), lambda b, i: (b, 0, 0, 0))
    out = jax.ShapeDtypeStruct((batch, BRANCH_WIDTH, seq), BF16)
    return pl.pallas_call(
        functools.partial(_attn_kernel, lam_init),
        grid=(batch, seq // (Q_BLOCKS * TQ)),
        in_specs=[
            pl.BlockSpec(lam_vecs.shape, const2),
            pl.BlockSpec(g_rows.shape, const2),
            pl.BlockSpec(bias_near.shape, const4),
            pl.BlockSpec(causal.shape, const2),
            pl.BlockSpec(pick.shape, lambda b, i: (0, 0, 0)),
            qsel, full, full_v, qsel, full, full_v,
            pl.BlockSpec((1, seq, LANES), lambda b, i: (b, 0, 0)),
        ],
        out_specs=[oblk, oblk],
        out_shape=[out, out],
        scratch_shapes=[
            pltpu.VMEM((Q_BLOCKS * CHAINS, 1, TQ), F32),
            pltpu.VMEM((Q_BLOCKS * DIFF_CHAINS, LANES + SUM_ROWS, TQ), F32),
            pltpu.VMEM((Q_BLOCKS * FOX_HEADS, HEAD_DIM + SUM_ROWS, TQ), F32),
        ],
        compiler_params=pltpu.CompilerParams(
            dimension_semantics=("parallel", "arbitrary"), vmem_limit_bytes=VMEM_LIMIT),
        name="attention",
    )(lam_vecs, g_rows, bias_near, causal, pick, qd, kd, vd, qf, kf, vf, decay)


def _merge_kernel(x_ref, od_ref, of_ref, g_ref, wtail_ref, wpa_ref, wpb_ref, wo_ref, y_ref,
                  wg_ref):
    @pl.when(pl.program_id(0) == 0)
    def _():
        wg_ref[...] = wtail_ref[:, FOX_HEADS:].astype(BF16)

    x = x_ref[...]
    h = _rms(x, g_ref[...]).astype(BF16)
    a = lax.dot_general(od_ref[0], wpa_ref[...], TN_DIMS, preferred_element_type=F32)
    b = lax.dot_general(of_ref[0], wpb_ref[...], TN_DIMS, preferred_element_type=F32)
    ga = jax.nn.sigmoid(jnp.dot(h, wg_ref[:, :D_MODEL], preferred_element_type=F32))
    merged = ga * a
    gb = jax.nn.sigmoid(jnp.dot(h, wg_ref[:, D_MODEL:], preferred_element_type=F32))
    merged = (merged + gb * b).astype(BF16)
    y_ref[...] = x + jnp.dot(merged, wo_ref[...], preferred_element_type=F32)


def _merge(x2, od, of, g_mix, w_tail, w_pa, w_pb, w_o):
    n = x2.shape[0]
    single = pl.Buffered(1)
    tm = TM_PROJ
    const = lambda i: (0, 0)
    row = lambda i: (i, 0)
    tiles_per_seq = od.shape[2] // tm
    col = pl.BlockSpec((1, BRANCH_WIDTH, tm), lambda i: (i // tiles_per_seq, 0, i % tiles_per_seq))
    return pl.pallas_call(
        _merge_kernel,
        grid=(n // tm,),
        in_specs=[
            pl.BlockSpec((tm, D_MODEL), row),
            col,
            col,
            pl.BlockSpec((1, D_MODEL), const),
            pl.BlockSpec(w_tail.shape, const, pipeline_mode=single),
            pl.BlockSpec(w_pa.shape, const, pipeline_mode=single),
            pl.BlockSpec(w_pb.shape, const, pipeline_mode=single),
            pl.BlockSpec(w_o.shape, const, pipeline_mode=single),
        ],
        out_specs=pl.BlockSpec((tm, D_MODEL), row),
        out_shape=jax.ShapeDtypeStruct((n, D_MODEL), F32),
        scratch_shapes=[pltpu.VMEM((D_MODEL, w_tail.shape[1] - FOX_HEADS), BF16)],
        compiler_params=pltpu.CompilerParams(
            dimension_semantics=("arbitrary",), vmem_limit_bytes=VMEM_LIMIT),
        name="merge",
    )(x2, od, of, g_mix, w_tail, w_pa, w_pb, w_o)


FF_CHUNK = 1024


def _mlp_kernel(final_norm, x_ref, g_ref, w1_ref, w2_ref, gf_ref, y_ref):
    x = x_ref[...]
    h = _rms(x, g_ref[...]).astype(BF16)
    y = x
    for c in range(D_FF // FF_CHUNK):
        cols = slice(c * FF_CHUNK, (c + 1) * FF_CHUNK)
        u = jnp.maximum(jnp.dot(h, w1_ref[:, cols], preferred_element_type=F32), 0.0)
        y = y + jnp.dot((u * u).astype(BF16), w2_ref[cols, :], preferred_element_type=F32)
    y_ref[...] = _rms(y, gf_ref[...]) if final_norm else y


def _mlp(x2, g_mlp, w_1, w_2, g_final, final_norm):
    n = x2.shape[0]
    tm = TM_PROJ
    const = lambda i: (0, 0)
    row = lambda i: (i, 0)
    single = pl.Buffered(1)
    return pl.pallas_call(
        functools.partial(_mlp_kernel, final_norm),
        grid=(n // tm,),
        in_specs=[
            pl.BlockSpec((tm, D_MODEL), row),
            pl.BlockSpec((1, D_MODEL), const),
            pl.BlockSpec(w_1.shape, const, pipeline_mode=single),
            pl.BlockSpec(w_2.shape, const, pipeline_mode=single),
            pl.BlockSpec((1, D_MODEL), const),
        ],
        out_specs=pl.BlockSpec((tm, D_MODEL), row),
        out_shape=jax.ShapeDtypeStruct((n, D_MODEL), F32),
        compiler_params=pltpu.CompilerParams(
            dimension_semantics=("parallel",), vmem_limit_bytes=VMEM_LIMIT),
        name="mlp",
    )(x2, g_mlp, w_1, w_2, g_final)


def _layer(x, layer_idx, g_mix, w_in, b_f, lam_q1, lam_k1, lam_q2, lam_k2, g_subln,
           w_pa, w_pb, w_o, g_mlp, w_1, w_2, bias_near, g_final, final_norm):
    batch, seq, d = x.shape
    n = batch * seq
    x2 = x.reshape(n, d)
    qkv_cols = 6 * BRANCH_WIDTH
    pad = LANES - DECAY_PARTS * FOX_HEADS
    w_tail = w_in[:, qkv_cols:]
    w_fl = jnp.pad(jnp.tile(w_tail[:, :FOX_HEADS], (1, DECAY_PARTS)),
                   ((0, 0), (0, pad))).astype(BF16)
    b_fl = jnp.pad(jnp.tile(b_f, DECAY_PARTS), (0, pad)).reshape(1, LANES)
    g_mix2 = g_mix.reshape(1, d)

    qd, kd, vd, qf, kf, vf, decay = _in_proj(x2, g_mix2, w_in[:, :qkv_cols], w_fl, b_fl, seq)

    lam_vecs = jnp.stack([lam_q1, lam_k1, lam_q2, lam_k2]).astype(F32)
    shape3 = (batch, seq, BRANCH_WIDTH)
    shape4 = (2,) + shape3
    od, of = _attention(lam_vecs, g_subln, bias_near,
                        qd.reshape(shape4), kd.reshape(shape3), vd,
                        qf.reshape(shape4), kf.reshape(shape3), vf,
                        decay.reshape(batch, seq, LANES), _lambda_init(layer_idx))

    x1 = _merge(x2, od, of, g_mix2,
                w_tail, w_pa.astype(BF16), w_pb.astype(BF16), w_o.astype(BF16))
    y = _mlp(x1, g_mlp.reshape(1, d), w_1.astype(BF16), w_2.astype(BF16), g_final, final_norm)
    return y.reshape(batch, seq, d)


def kernel(x, g_mix, w_in, b_f, lam_q1, lam_k1, lam_q2, lam_k2, g_subln, w_pa, w_pb, w_o,
           g_mlp, w_1, w_2, rel_table, g_final):
    depth = g_mix.shape[0]
    bias_near = _bias_tiles(rel_table, x.shape[1])
    for l in range(depth):
        x = _layer(x, l, g_mix[l], w_in[l], b_f[l], lam_q1[l], lam_k1[l], lam_q2[l], lam_k2[l],
                   g_subln[l], w_pa[l], w_pb[l], w_o[l], g_mlp[l], w_1[l], w_2[l],
                   bias_near, g_final.reshape(1, -1), l == depth - 1)
    return x
```

```python
import functools
import math

import numpy as np
import jax
import jax.numpy as jnp
from jax import lax
from jax.experimental import pallas as pl
from jax.experimental.pallas import tpu as pltpu

D_MODEL = 1024
CHUNK = 64
HEAD_DIM = 64
DIFF_HEADS = 4
FOX_HEADS = 8
BRANCH_WIDTH = 512
D_FF = 4 * D_MODEL
REL_BUCKETS = 32
REL_MAX_DIST = 128
EPS = 1e-6
LANES = 128
MASKED_BUCKET = REL_BUCKETS

TQ = 256
TK = 256
TM_IN_PROJ = 512
TM_PROJ = 1024
V7X_VMEM_BYTES = 64 * 1024 * 1024
VMEM_LIMIT = V7X_VMEM_BYTES * 7 // 8

LOG2E = math.log2(math.e)
Q_SCALE = HEAD_DIM ** -0.5 * LOG2E

F32 = jnp.float32
BF16 = jnp.bfloat16
NT_DIMS = (((1,), (1,)), ((), ()))
TN_DIMS = (((0,), (0,)), ((), ()))


def _lambda_init(layer_idx):
    return 0.8 - 0.6 * math.exp(-0.3 * layer_idx)


def _rms(xf, g):
    return xf * lax.rsqrt(jnp.mean(xf * xf, axis=-1, keepdims=True) + EPS) * g


DECAY_PARTS = 3


def _in_proj_kernel(tiles_per_seq, x0_ref, xnext_ref, g_ref, w32_ref, wfl_ref, bf_ref,
                    qd_ref, kd_ref, vd_ref, qf_ref, kf_ref, vf_ref, dec_ref,
                    carry_ref, w_ref, h_ref):
    step = pl.program_id(0)

    @pl.when(step == 0)
    def _():
        w_ref[...] = w32_ref[...].astype(BF16)
        h_ref[0] = _rms(x0_ref[...], g_ref[...]).astype(BF16)

    h = h_ref[step % 2]

    @pl.when(step % tiles_per_seq == 0)
    def _():
        carry_ref[...] = jnp.zeros_like(carry_ref)

    z = jnp.dot(h, wfl_ref[...], preferred_element_type=F32) + bf_ref[...]
    acc = jnp.minimum(z, 0.0) - jnp.log1p(jnp.exp(-jnp.abs(z)))
    rows = acc.shape[0]
    row = lax.broadcasted_iota(jnp.int32, acc.shape, 0)
    d = 1
    while d < rows:
        acc = acc + jnp.where(row >= d, pltpu.roll(acc, d, axis=0), 0.0)
        d *= 2
    acc = acc + carry_ref[...]
    carry_ref[...] = acc[rows - 1:rows, :]
    neg = acc * -LOG2E
    hi = neg.astype(BF16).astype(F32)
    mid = (neg - hi).astype(BF16).astype(F32)
    lo = neg - hi - mid
    lane = lax.broadcasted_iota(jnp.int32, acc.shape, 1)
    piece = jnp.where(lane < FOX_HEADS, hi, jnp.where(lane < 2 * FOX_HEADS, mid, lo))
    dec_ref[...] = jnp.where(lane < DECAY_PARTS * FOX_HEADS, piece, 0.0).astype(BF16)

    outs = (qd_ref, kd_ref, vd_ref, qf_ref, kf_ref, vf_ref)
    for c in (2, 5, 0, 3, 1, 4):
        o_ref = outs[c]
        w = w_ref[:, c * BRANCH_WIDTH:(c + 1) * BRANCH_WIDTH]
        o = jnp.dot(h, w, preferred_element_type=F32)
        if o_ref is qd_ref or o_ref is qf_ref:
            o = (o * Q_SCALE).astype(BF16)
            low = lax.broadcasted_iota(jnp.int32, o.shape, 1) % LANES < HEAD_DIM
            o_ref[0] = jnp.where(low, o, jnp.zeros_like(o))
            o_ref[1] = jnp.where(low, jnp.zeros_like(o), o)
        elif o_ref is vd_ref or o_ref is vf_ref:
            o_t = o.T.astype(BF16)
            for t in range(o_ref.shape[1]):
                o_ref[0, t] = o_t[:, t * TK:(t + 1) * TK]
        else:
            o_ref[...] = o.astype(BF16)

    h_ref[(step + 1) % 2] = _rms(xnext_ref[...], g_ref[...]).astype(BF16)


def _in_proj(x2, g_mix, w_in, w_fl, b_f, seq):
    n = x2.shape[0]
    tm = TM_IN_PROJ
    qkv_cols = 6 * BRANCH_WIDTH
    const = lambda i: (0, 0)
    row = lambda i: (i, 0)
    out_bf = jax.ShapeDtypeStruct((n, BRANCH_WIDTH), BF16)
    out_q = jax.ShapeDtypeStruct((2, n, BRANCH_WIDTH), BF16)
    spec_bf = pl.BlockSpec((tm, BRANCH_WIDTH), row)
    spec_q = pl.BlockSpec((2, tm, BRANCH_WIDTH), lambda i: (0, i, 0))
    tiles_per_seq = seq // tm
    out_v = jax.ShapeDtypeStruct((n // seq, seq // TK, BRANCH_WIDTH, TK), BF16)
    spec_v = pl.BlockSpec((1, tm // TK, BRANCH_WIDTH, TK),
                          lambda i: (i // tiles_per_seq, i % tiles_per_seq, 0, 0))
    return pl.pallas_call(
        functools.partial(_in_proj_kernel, seq // tm),
        grid=(n // tm,),
        in_specs=[
            pl.BlockSpec((tm, D_MODEL), const),
            pl.BlockSpec((tm, D_MODEL), lambda i: (jnp.minimum(i + 1, n // tm - 1), 0)),
            pl.BlockSpec((1, D_MODEL), const),
            pl.BlockSpec((D_MODEL, qkv_cols), const, pipeline_mode=pl.Buffered(1)),
            pl.BlockSpec(w_fl.shape, const),
            pl.BlockSpec((1, LANES), const),
        ],
        out_specs=[spec_q, spec_bf, spec_v, spec_q, spec_bf, spec_v,
                   pl.BlockSpec((tm, LANES), row)],
        out_shape=[out_q, out_bf, out_v, out_q, out_bf, out_v,
                   jax.ShapeDtypeStruct((n, LANES), BF16)],
        scratch_shapes=[pltpu.VMEM((1, LANES), F32),
                        pltpu.VMEM((D_MODEL, qkv_cols), BF16),
                        pltpu.VMEM((2, tm, D_MODEL), BF16)],
        compiler_params=pltpu.CompilerParams(
            dimension_semantics=("arbitrary",), vmem_limit_bytes=VMEM_LIMIT),
        name="in_proj",
    )(x2, x2, g_mix, w_in, w_fl, b_f)


def _rel_bucket_np(rel):
    nb = REL_BUCKETS // 2
    ret = np.where(rel > 0, nb, 0)
    n = np.abs(rel)
    max_exact = nb // 2
    nf = np.maximum(n, 1).astype(np.float64)
    large = max_exact + (np.log(nf / max_exact) / math.log(REL_MAX_DIST / max_exact)
                         * (nb - max_exact)).astype(np.int32)
    large = np.minimum(large, nb - 1)
    return (ret + np.where(n < max_exact, n, large)).astype(np.int32)


def _bias_index_maps(seq):
    kk = np.arange(TK, dtype=np.int64)[:, None]
    qq = np.arange(TQ, dtype=np.int64)[None, :]
    diag = _rel_bucket_np(kk - qq)
    diag = np.where(kk // CHUNK <= qq // CHUNK, diag, MASKED_BUCKET).astype(np.int32)
    prev = _rel_bucket_np(kk - TK - qq)
    far = _rel_bucket_np(np.arange(-seq, -TK, dtype=np.int64))
    far_bucket = int(far[0])
    assert (far == far_bucket).all(), "keys two tiles back must share one bucket"
    return diag, prev, far_bucket


def _bias_kernel(far_bucket, buckets, tab_ref, idx_ref, out_ref):
    for h in range(DIFF_HEADS):
        far = tab_ref[far_bucket, h]
        for t in range(idx_ref.shape[0]):
            idx = idx_ref[t]
            acc = jnp.full(idx.shape, -jnp.inf, F32)
            for b in buckets[t]:
                acc = jnp.where(idx == b, (tab_ref[b, h] - far) * LOG2E, acc)
            out_ref[t, h] = acc


def _bias_tiles(rel_table, seq):
    diag, prev, far_bucket = _bias_index_maps(seq)
    idx = np.stack([diag, prev])
    buckets = tuple(tuple(int(b) for b in np.unique(m) if b != MASKED_BUCKET) for m in idx)
    vmem = pl.BlockSpec(memory_space=pltpu.VMEM)
    return pl.pallas_call(
        functools.partial(_bias_kernel, far_bucket, buckets),
        in_specs=[pl.BlockSpec(memory_space=pltpu.SMEM), vmem],
        out_specs=vmem,
        out_shape=jax.ShapeDtypeStruct((idx.shape[0], DIFF_HEADS, TK, TQ), F32),
        name="bias_tiles",
    )(rel_table, jnp.asarray(idx))


DIFF_CHAINS = 2 * DIFF_HEADS
CHAINS = DIFF_CHAINS + FOX_HEADS
QK_AHEAD = 8
Q_BLOCKS = 4
SUM_ROWS = 16


def _attn_kernel(lam_init, lam_ref, gsub_ref, bnear_ref, causal_ref, pick_ref,
                 qd_ref, kd_ref, vd_ref, qf_ref, kf_ref, vf_ref, dec_ref,
                 od_ref, of_ref, m_ref, accd_ref, accf_ref):
    g = pl.program_id(1)
    lam_v = lam_ref[...]
    lam = (jnp.exp(jnp.sum(lam_v[0:1] * lam_v[1:2], axis=-1, keepdims=True))
           - jnp.exp(jnp.sum(lam_v[2:3] * lam_v[3:4], axis=-1, keepdims=True))
           + lam_init)

    def is_fox(c):
        return c >= DIFF_CHAINS

    def cols(c):
        blk = (c % DIFF_CHAINS) // 2
        return slice(blk * LANES, (blk + 1) * LANES)

    def rows(qb):
        return slice(qb * TQ, (qb + 1) * TQ)

    def accumulator(qb, c):
        if is_fox(c):
            return accf_ref.at[qb * FOX_HEADS + c - DIFF_CHAINS]
        return accd_ref.at[qb * DIFF_CHAINS + c]

    def normalised(qb, c):
        acc = accumulator(qb, c)[...]
        chans = acc.shape[0] - SUM_ROWS
        return acc[:chans] * (1.0 / acc[chans:chans + 1])

    def run(seq):
        tiles = {}

        def load(kind, tile, c):
            if kind == "v":
                chan = (slice((c - DIFF_CHAINS) * HEAD_DIM, (c - DIFF_CHAINS + 1) * HEAD_DIM)
                        if is_fox(c) else cols(c))
                key = (kind, id(tile), is_fox(c), chan.start)
                if key not in tiles:
                    v_t = (vf_ref if is_fox(c) else vd_ref)[0, tile, chan, :]
                    tiles[key] = jnp.concatenate([v_t, jnp.ones((SUM_ROWS, TK), BF16)], axis=0)
                return tiles[key]
            key = (kind, id(tile), is_fox(c), cols(c).start)
            if key not in tiles:
                at = pl.ds(pl.multiple_of(tile * TK, TK), TK)
                if is_fox(c):
                    tiles[key] = jnp.concatenate([kf_ref[0, at, cols(c)], dec_ref[0, at, :]],
                                                 axis=1)
                else:
                    tiles[key] = kd_ref[0, at, cols(c)]
            return tiles[key]

        def scores(tile, qb, c, add, first):
            if is_fox(c):
                q_t = jnp.concatenate([qf_ref[c % 2, 0, rows(qb), cols(c)],
                                       pick_ref[c - DIFF_CHAINS]], axis=1)
            else:
                q_t = qd_ref[c % 2, 0, rows(qb), cols(c)]
            s = lax.dot_general(load("k", tile, c), q_t, NT_DIMS, preferred_element_type=F32)
            return s if add is None else s + add()

        pending = {j: scores(*seq[j]) for j in range(min(QK_AHEAD, len(seq)))}
        for j, (tile, qb, c, _, first) in enumerate(seq):
            s = pending.pop(j)
            state = qb * CHAINS + c
            m_new = jnp.max(s, axis=0, keepdims=True)
            if not first:
                m_old = m_ref[state]
                m_new = jnp.maximum(m_old, m_new)
                alpha = jnp.exp2(m_old - m_new)
            p = jnp.exp2(s - m_new).astype(BF16)
            pv = jnp.dot(load("v", tile, c), p, preferred_element_type=F32)
            if j + QK_AHEAD < len(seq):
                pending[j + QK_AHEAD] = scores(*seq[j + QK_AHEAD])
            acc = accumulator(qb, c)
            m_ref[state] = m_new
            acc[...] = pv if first else alpha * acc[...] + pv

    def bias(kind, c):
        if kind == "diag":
            return (lambda: causal_ref[...]) if is_fox(c) else (lambda: bnear_ref[0, c // 2])
        if kind == "prev" and not is_fox(c):
            return lambda: bnear_ref[1, c // 2]
        return None

    def steps(tile, kinds):
        order = [c0 + b for c0 in range(0, DIFF_CHAINS, 2) for b in (0, DIFF_CHAINS)]
        return [(tile, qb, c0 + e, bias(kind, c0 + e), kind == "diag")
                for c0 in order for qb, kind in kinds.items() for e in range(2)]

    base = Q_BLOCKS * g
    head = []
    for u in reversed(range(Q_BLOCKS)):
        head += steps(base + u, {j: "diag" if j == u else "prev" if j == u + 1 else "far"
                                 for j in range(u, Q_BLOCKS)})
    run(head)

    def body(r, carry):
        hi = base - 1 - 2 * r
        lo = hi - 1
        far = {j: "far" for j in range(Q_BLOCKS)}
        pl.when(r == 0)(lambda: run(steps(hi, {**far, 0: "prev"}) + steps(lo, far)))
        pl.when(r > 0)(lambda: run(steps(hi, far) + steps(lo, far)))
        return carry

    lax.fori_loop(0, base // 2, body, 0)

    for qb in range(Q_BLOCKS):
        for h in range(DIFF_HEADS):
            o = normalised(qb, 2 * h) - lam * normalised(qb, 2 * h + 1)
            o = o * lax.rsqrt(jnp.mean(o * o, axis=0, keepdims=True) + EPS) * gsub_ref[...]
            od_ref[0, h * LANES:(h + 1) * LANES, rows(qb)] = o.astype(BF16)
        for h in range(FOX_HEADS):
            o = normalised(qb, DIFF_CHAINS + h)
            of_ref[0, h * HEAD_DIM:(h + 1) * HEAD_DIM, rows(qb)] = o.astype(BF16)


def _attention(lam_vecs, g_subln, bias_near, qd, kd, vd, qf, kf, vf, decay, lam_init):
    batch, seq, _ = kd.shape
    const2 = lambda b, i: (0, 0)
    const4 = lambda b, i: (0, 0, 0, 0)
    kk = np.arange(TK)[:, None]
    qq = np.arange(TQ)[None, :]
    causal = jnp.asarray(np.where(kk <= qq, 0.0, -np.inf).astype(np.float32))
    lane = np.arange(LANES)
    pick = (lane[None] < DECAY_PARTS * FOX_HEADS) & (lane[None] % FOX_HEADS
                                                     == np.arange(FOX_HEADS)[:, None])
    pick = jnp.asarray(np.broadcast_to(pick[:, None, :], (FOX_HEADS, TQ, LANES)), BF16)
    g_rows = jnp.broadcast_to((g_subln * (1.0 - lam_init)).reshape(LANES, 1), (LANES, TQ))
    oblk = pl.BlockSpec((1, BRANCH_WIDTH, Q_BLOCKS * TQ), lambda b, i: (b, 0, i))
    qsel = pl.BlockSpec((2, 1, Q_BLOCKS * TQ, BRANCH_WIDTH), lambda b, i: (0, b, i, 0))
    full = pl.BlockSpec((1, seq, BRANCH_WIDTH), lambda b, i: (b, 0, 0))
    full_v = pl.BlockSpec((1, seq // TK, BRANCH_WIDTH, TK), lambda b, i: (b, 0, 0, 0))
    out = jax.ShapeDtypeStruct((batch, BRANCH_WIDTH, seq), BF16)
    return pl.pallas_call(
        functools.partial(_attn_kernel, lam_init),
        grid=(batch, seq // (Q_BLOCKS * TQ)),
        in_specs=[
            pl.BlockSpec(lam_vecs.shape, const2),
            pl.BlockSpec(g_rows.shape, const2),
            pl.BlockSpec(bias_near.shape, const4),
            pl.BlockSpec(causal.shape, const2),
            pl.BlockSpec(pick.shape, lambda b, i: (0, 0, 0)),
            qsel, full, full_v, qsel, full, full_v,
            pl.BlockSpec((1, seq, LANES), lambda b, i: (b, 0, 0)),
        ],
        out_specs=[oblk, oblk],
        out_shape=[out, out],
        scratch_shapes=[
            pltpu.VMEM((Q_BLOCKS * CHAINS, 1, TQ), F32),
            pltpu.VMEM((Q_BLOCKS * DIFF_CHAINS, LANES + SUM_ROWS, TQ), F32),
            pltpu.VMEM((Q_BLOCKS * FOX_HEADS, HEAD_DIM + SUM_ROWS, TQ), F32),
        ],
        compiler_params=pltpu.CompilerParams(
            dimension_semantics=("parallel", "arbitrary"), vmem_limit_bytes=VMEM_LIMIT),
        name="attention",
    )(lam_vecs, g_rows, bias_near, causal, pick, qd, kd, vd, qf, kf, vf, decay)


def _merge_kernel(x_ref, od_ref, of_ref, g_ref, wg_ref, wpa_ref, wpb_ref, wo_ref, y_ref):
    x = x_ref[...]
    h = _rms(x, g_ref[...]).astype(BF16)
    a = lax.dot_general(od_ref[0], wpa_ref[...], TN_DIMS, preferred_element_type=F32)
    b = lax.dot_general(of_ref[0], wpb_ref[...], TN_DIMS, preferred_element_type=F32)
    ga = jax.nn.sigmoid(jnp.dot(h, wg_ref[:, :D_MODEL], preferred_element_type=F32))
    merged = ga * a
    gb = jax.nn.sigmoid(jnp.dot(h, wg_ref[:, D_MODEL:], preferred_element_type=F32))
    merged = (merged + gb * b).astype(BF16)
    y_ref[...] = x + jnp.dot(merged, wo_ref[...], preferred_element_type=F32)


def _merge(x2, od, of, g_mix, w_gate, w_pa, w_pb, w_o):
    n = x2.shape[0]
    tm = TM_PROJ
    const = lambda i: (0, 0)
    row = lambda i: (i, 0)
    tiles_per_seq = od.shape[2] // tm
    col = pl.BlockSpec((1, BRANCH_WIDTH, tm), lambda i: (i // tiles_per_seq, 0, i % tiles_per_seq))
    return pl.pallas_call(
        _merge_kernel,
        grid=(n // tm,),
        in_specs=[
            pl.BlockSpec((tm, D_MODEL), row),
            col,
            col,
            pl.BlockSpec((1, D_MODEL), const),
            pl.BlockSpec(w_gate.shape, const),
            pl.BlockSpec(w_pa.shape, const),
            pl.BlockSpec(w_pb.shape, const),
            pl.BlockSpec(w_o.shape, const),
        ],
        out_specs=pl.BlockSpec((tm, D_MODEL), row),
        out_shape=jax.ShapeDtypeStruct((n, D_MODEL), F32),
        compiler_params=pltpu.CompilerParams(
            dimension_semantics=("parallel",), vmem_limit_bytes=VMEM_LIMIT),
        name="merge",
    )(x2, od, of, g_mix, w_gate, w_pa, w_pb, w_o)


FF_CHUNK = 1024


def _mlp_kernel(final_norm, x_ref, g_ref, w1_ref, w2_ref, gf_ref, y_ref):
    x = x_ref[...]
    h = _rms(x, g_ref[...]).astype(BF16)
    y = x
    for c in range(D_FF // FF_CHUNK):
        cols = slice(c * FF_CHUNK, (c + 1) * FF_CHUNK)
        u = jnp.maximum(jnp.dot(h, w1_ref[:, cols], preferred_element_type=F32), 0.0)
        y = y + jnp.dot((u * u).astype(BF16), w2_ref[cols, :], preferred_element_type=F32)
    y_ref[...] = _rms(y, gf_ref[...]) if final_norm else y


def _mlp(x2, g_mlp, w_1, w_2, g_final, final_norm):
    n = x2.shape[0]
    tm = TM_PROJ
    const = lambda i: (0, 0)
    row = lambda i: (i, 0)
    single = pl.Buffered(1)
    return pl.pallas_call(
        functools.partial(_mlp_kernel, final_norm),
        grid=(n // tm,),
        in_specs=[
            pl.BlockSpec((tm, D_MODEL), row),
            pl.BlockSpec((1, D_MODEL), const),
            pl.BlockSpec(w_1.shape, const, pipeline_mode=single),
            pl.BlockSpec(w_2.shape, const, pipeline_mode=single),
            pl.BlockSpec((1, D_MODEL), const),
        ],
        out_specs=pl.BlockSpec((tm, D_MODEL), row),
        out_shape=jax.ShapeDtypeStruct((n, D_MODEL), F32),
        compiler_params=pltpu.CompilerParams(
            dimension_semantics=("parallel",), vmem_limit_bytes=VMEM_LIMIT),
        name="mlp",
    )(x2, g_mlp, w_1, w_2, g_final)


def _layer(x, layer_idx, g_mix, w_in, b_f, lam_q1, lam_k1, lam_q2, lam_k2, g_subln,
           w_pa, w_pb, w_o, g_mlp, w_1, w_2, bias_near, g_final, final_norm):
    batch, seq, d = x.shape
    n = batch * seq
    x2 = x.reshape(n, d)
    qkv_cols = 6 * BRANCH_WIDTH
    pad = LANES - DECAY_PARTS * FOX_HEADS
    w_tail = lax.optimization_barrier(w_in[:, qkv_cols:])
    w_fl = jnp.pad(jnp.tile(w_tail[:, :FOX_HEADS], (1, DECAY_PARTS)),
                   ((0, 0), (0, pad))).astype(BF16)
    b_fl = jnp.pad(jnp.tile(b_f, DECAY_PARTS), (0, pad)).reshape(1, LANES)
    w_gate = w_tail[:, FOX_HEADS:].astype(BF16)
    g_mix2 = g_mix.reshape(1, d)

    qd, kd, vd, qf, kf, vf, decay = _in_proj(x2, g_mix2, w_in, w_fl, b_fl, seq)

    lam_vecs = jnp.stack([lam_q1, lam_k1, lam_q2, lam_k2]).astype(F32)
    shape3 = (batch, seq, BRANCH_WIDTH)
    shape4 = (2,) + shape3
    od, of = _attention(lam_vecs, g_subln, bias_near,
                        qd.reshape(shape4), kd.reshape(shape3), vd,
                        qf.reshape(shape4), kf.reshape(shape3), vf,
                        decay.reshape(batch, seq, LANES), _lambda_init(layer_idx))

    x1 = _merge(x2, od, of, g_mix2,
                w_gate, w_pa.astype(BF16), w_pb.astype(BF16), w_o.astype(BF16))
    y = _mlp(x1, g_mlp.reshape(1, d), w_1.astype(BF16), w_2.astype(BF16), g_final, final_norm)
    return y.reshape(batch, seq, d)


def kernel(x, g_mix, w_in, b_f, lam_q1, lam_k1, lam_q2, lam_k2, g_subln, w_pa, w_pb, w_o,
           g_mlp, w_1, w_2, rel_table, g_final):
    depth = g_mix.shape[0]
    batch, seq, d = x.shape
    assert d == D_MODEL and w_in.shape[1:] == (D_MODEL, 6 * BRANCH_WIDTH + FOX_HEADS + 2 * D_MODEL)
    assert w_1.shape[1:] == (D_MODEL, D_FF) and rel_table.shape == (REL_BUCKETS, DIFF_HEADS)
    assert seq % (Q_BLOCKS * TQ) == 0 and seq % TM_IN_PROJ == 0 and seq % TM_PROJ == 0
    bias_near = _bias_tiles(rel_table, seq)
    for l in range(depth):
        x = _layer(x, l, g_mix[l], w_in[l], b_f[l], lam_q1[l], lam_k1[l], lam_q2[l], lam_k2[l],
                   g_subln[l], w_pa[l], w_pb[l], w_o[l], g_mlp[l], w_1[l], w_2[l],
                   bias_near, g_final.reshape(1, -1), l == depth - 1)
    return x
```

```python
import functools
import math

import numpy as np
import jax
import jax.numpy as jnp
from jax import lax
from jax.experimental import pallas as pl
from jax.experimental.pallas import tpu as pltpu

D_MODEL = 1024
CHUNK = 64
HEAD_DIM = 64
DIFF_HEADS = 4
FOX_HEADS = 8
BRANCH_WIDTH = 512
D_FF = 4 * D_MODEL
REL_BUCKETS = 32
REL_MAX_DIST = 128
EPS = 1e-6
LANES = 128
MASKED_BUCKET = REL_BUCKETS

TQ = 256
TK = 256
TM_IN_PROJ = 512
TM_PROJ = 1024
V7X_VMEM_BYTES = 64 * 1024 * 1024
VMEM_LIMIT = V7X_VMEM_BYTES * 7 // 8

LOG2E = math.log2(math.e)
Q_SCALE = HEAD_DIM ** -0.5 * LOG2E

F32 = jnp.float32
BF16 = jnp.bfloat16
NT_DIMS = (((1,), (1,)), ((), ()))
TN_DIMS = (((0,), (0,)), ((), ()))


def _lambda_init(layer_idx):
    return 0.8 - 0.6 * math.exp(-0.3 * layer_idx)


def _rms(xf, g):
    return xf * lax.rsqrt(jnp.mean(xf * xf, axis=-1, keepdims=True) + EPS) * g


DECAY_PARTS = 3


def _in_proj_kernel(tiles_per_seq, x0_ref, xnext_ref, g_ref, w32_ref, wfl_ref, bf_ref,
                    qd_ref, kd_ref, vd_ref, qf_ref, kf_ref, vf_ref, dec_ref,
                    carry_ref, w_ref, h_ref):
    step = pl.program_id(0)

    @pl.when(step == 0)
    def _():
        w_ref[...] = w32_ref[...].astype(BF16)
        h_ref[0] = _rms(x0_ref[...], g_ref[...]).astype(BF16)

    h = h_ref[step % 2]

    @pl.when(step % tiles_per_seq == 0)
    def _():
        carry_ref[...] = jnp.zeros_like(carry_ref)

    z = jnp.dot(h, wfl_ref[...], preferred_element_type=F32) + bf_ref[...]
    acc = jnp.minimum(z, 0.0) - jnp.log1p(jnp.exp(-jnp.abs(z)))
    rows = acc.shape[0]
    row = lax.broadcasted_iota(jnp.int32, acc.shape, 0)
    d = 1
    while d < rows:
        acc = acc + jnp.where(row >= d, pltpu.roll(acc, d, axis=0), 0.0)
        d *= 2
    acc = acc + carry_ref[...]
    carry_ref[...] = acc[rows - 1:rows, :]
    neg = acc * -LOG2E
    hi = neg.astype(BF16).astype(F32)
    mid = (neg - hi).astype(BF16).astype(F32)
    lo = neg - hi - mid
    lane = lax.broadcasted_iota(jnp.int32, acc.shape, 1)
    piece = jnp.where(lane < FOX_HEADS, hi, jnp.where(lane < 2 * FOX_HEADS, mid, lo))
    dec_ref[...] = jnp.where(lane < DECAY_PARTS * FOX_HEADS, piece, 0.0).astype(BF16)

    outs = (qd_ref, kd_ref, vd_ref, qf_ref, kf_ref, vf_ref)
    for c in (2, 5, 0, 3, 1, 4):
        o_ref = outs[c]
        w = w_ref[:, c * BRANCH_WIDTH:(c + 1) * BRANCH_WIDTH]
        o = jnp.dot(h, w, preferred_element_type=F32)
        if o_ref is qd_ref or o_ref is qf_ref:
            o = (o * Q_SCALE).astype(BF16)
            low = lax.broadcasted_iota(jnp.int32, o.shape, 1) % LANES < HEAD_DIM
            o_ref[0] = jnp.where(low, o, jnp.zeros_like(o))
            o_ref[1] = jnp.where(low, jnp.zeros_like(o), o)
        elif o_ref is vd_ref or o_ref is vf_ref:
            o_t = o.T.astype(BF16)
            for t in range(o_ref.shape[1]):
                o_ref[0, t] = o_t[:, t * TK:(t + 1) * TK]
        else:
            o_ref[...] = o.astype(BF16)

    h_ref[(step + 1) % 2] = _rms(xnext_ref[...], g_ref[...]).astype(BF16)


def _in_proj(x2, g_mix, w_in, w_fl, b_f, seq):
    n = x2.shape[0]
    tm = TM_IN_PROJ
    qkv_cols = 6 * BRANCH_WIDTH
    const = lambda i: (0, 0)
    row = lambda i: (i, 0)
    out_bf = jax.ShapeDtypeStruct((n, BRANCH_WIDTH), BF16)
    out_q = jax.ShapeDtypeStruct((2, n, BRANCH_WIDTH), BF16)
    spec_bf = pl.BlockSpec((tm, BRANCH_WIDTH), row)
    spec_q = pl.BlockSpec((2, tm, BRANCH_WIDTH), lambda i: (0, i, 0))
    tiles_per_seq = seq // tm
    out_v = jax.ShapeDtypeStruct((n // seq, seq // TK, BRANCH_WIDTH, TK), BF16)
    spec_v = pl.BlockSpec((1, tm // TK, BRANCH_WIDTH, TK),
                          lambda i: (i // tiles_per_seq, i % tiles_per_seq, 0, 0))
    return pl.pallas_call(
        functools.partial(_in_proj_kernel, seq // tm),
        grid=(n // tm,),
        in_specs=[
            pl.BlockSpec((tm, D_MODEL), const),
            pl.BlockSpec((tm, D_MODEL), lambda i: (jnp.minimum(i + 1, n // tm - 1), 0)),
            pl.BlockSpec((1, D_MODEL), const),
            pl.BlockSpec((D_MODEL, qkv_cols), const, pipeline_mode=pl.Buffered(1)),
            pl.BlockSpec(w_fl.shape, const),
            pl.BlockSpec((1, LANES), const),
        ],
        out_specs=[spec_q, spec_bf, spec_v, spec_q, spec_bf, spec_v,
                   pl.BlockSpec((tm, LANES), row)],
        out_shape=[out_q, out_bf, out_v, out_q, out_bf, out_v,
                   jax.ShapeDtypeStruct((n, LANES), BF16)],
        scratch_shapes=[pltpu.VMEM((1, LANES), F32),
                        pltpu.VMEM((D_MODEL, qkv_cols), BF16),
                        pltpu.VMEM((2, tm, D_MODEL), BF16)],
        compiler_params=pltpu.CompilerParams(
            dimension_semantics=("arbitrary",), vmem_limit_bytes=VMEM_LIMIT),
        name="in_proj",
    )(x2, x2, g_mix, w_in, w_fl, b_f)


def _rel_bucket_np(rel):
    nb = REL_BUCKETS // 2
    ret = np.where(rel > 0, nb, 0)
    n = np.abs(rel)
    max_exact = nb // 2
    nf = np.maximum(n, 1).astype(np.float64)
    large = max_exact + (np.log(nf / max_exact) / math.log(REL_MAX_DIST / max_exact)
                         * (nb - max_exact)).astype(np.int32)
    large = np.minimum(large, nb - 1)
    return (ret + np.where(n < max_exact, n, large)).astype(np.int32)


def _bias_index_maps(seq):
    kk = np.arange(TK, dtype=np.int64)[:, None]
    qq = np.arange(TQ, dtype=np.int64)[None, :]
    diag = _rel_bucket_np(kk - qq)
    diag = np.where(kk // CHUNK <= qq // CHUNK, diag, MASKED_BUCKET).astype(np.int32)
    prev = _rel_bucket_np(kk - TK - qq)
    far = _rel_bucket_np(np.arange(-seq, -TK, dtype=np.int64))
    far_bucket = int(far[0])
    assert (far == far_bucket).all(), "keys two tiles back must share one bucket"
    return diag, prev, far_bucket


def _bias_kernel(far_bucket, buckets, tab_ref, idx_ref, out_ref):
    for h in range(DIFF_HEADS):
        far = tab_ref[far_bucket, h]
        for t in range(idx_ref.shape[0]):
            idx = idx_ref[t]
            acc = jnp.full(idx.shape, -jnp.inf, F32)
            for b in buckets[t]:
                acc = jnp.where(idx == b, (tab_ref[b, h] - far) * LOG2E, acc)
            out_ref[t, h] = acc


def _bias_tiles(rel_table, seq):
    diag, prev, far_bucket = _bias_index_maps(seq)
    idx = np.stack([diag, prev])
    buckets = tuple(tuple(int(b) for b in np.unique(m) if b != MASKED_BUCKET) for m in idx)
    vmem = pl.BlockSpec(memory_space=pltpu.VMEM)
    return pl.pallas_call(
        functools.partial(_bias_kernel, far_bucket, buckets),
        in_specs=[pl.BlockSpec(memory_space=pltpu.SMEM), vmem],
        out_specs=vmem,
        out_shape=jax.ShapeDtypeStruct((idx.shape[0], DIFF_HEADS, TK, TQ), F32),
        name="bias_tiles",
    )(rel_table, jnp.asarray(idx))


DIFF_CHAINS = 2 * DIFF_HEADS
CHAINS = DIFF_CHAINS + FOX_HEADS
QK_AHEAD = 6
Q_BLOCKS = 4
SUM_ROWS = 16


def _attn_kernel(lam_init, lam_ref, gsub_ref, bnear_ref, causal_ref, pick_ref,
                 qd_ref, kd_ref, vd_ref, qf_ref, kf_ref, vf_ref, dec_ref,
                 od_ref, of_ref, m_ref, accd_ref, accf_ref):
    g = pl.program_id(1)
    lam_v = lam_ref[...]
    lam = (jnp.exp(jnp.sum(lam_v[0:1] * lam_v[1:2], axis=-1, keepdims=True))
           - jnp.exp(jnp.sum(lam_v[2:3] * lam_v[3:4], axis=-1, keepdims=True))
           + lam_init)

    def is_fox(c):
        return c >= DIFF_CHAINS

    def cols(c):
        blk = (c % DIFF_CHAINS) // 2
        return slice(blk * LANES, (blk + 1) * LANES)

    def rows(qb):
        return slice(qb * TQ, (qb + 1) * TQ)

    def accumulator(qb, c):
        if is_fox(c):
            return accf_ref.at[qb * FOX_HEADS + c - DIFF_CHAINS]
        return accd_ref.at[qb * DIFF_CHAINS + c]

    def normalised(qb, c):
        acc = accumulator(qb, c)[...]
        chans = acc.shape[0] - SUM_ROWS
        return acc[:chans] * (1.0 / acc[chans:chans + 1])

    def run(seq):
        tiles = {}

        def load(kind, tile, c):
            if kind == "v":
                chan = (slice((c - DIFF_CHAINS) * HEAD_DIM, (c - DIFF_CHAINS + 1) * HEAD_DIM)
                        if is_fox(c) else cols(c))
                key = (kind, id(tile), is_fox(c), chan.start)
                if key not in tiles:
                    v_t = (vf_ref if is_fox(c) else vd_ref)[0, tile, chan, :]
                    tiles[key] = jnp.concatenate([v_t, jnp.ones((SUM_ROWS, TK), BF16)], axis=0)
                return tiles[key]
            key = (kind, id(tile), is_fox(c), cols(c).start)
            if key not in tiles:
                at = pl.ds(pl.multiple_of(tile * TK, TK), TK)
                if is_fox(c):
                    tiles[key] = jnp.concatenate([kf_ref[0, at, cols(c)], dec_ref[0, at, :]],
                                                 axis=1)
                else:
                    tiles[key] = kd_ref[0, at, cols(c)]
            return tiles[key]

        def scores(tile, qb, c, add, first):
            if is_fox(c):
                q_t = jnp.concatenate([qf_ref[c % 2, 0, rows(qb), cols(c)],
                                       pick_ref[c - DIFF_CHAINS]], axis=1)
            else:
                q_t = qd_ref[c % 2, 0, rows(qb), cols(c)]
            s = lax.dot_general(load("k", tile, c), q_t, NT_DIMS, preferred_element_type=F32)
            return s if add is None else s + add()

        pending = {j: scores(*seq[j]) for j in range(min(QK_AHEAD, len(seq)))}
        for j, (tile, qb, c, _, first) in enumerate(seq):
            s = pending.pop(j)
            state = qb * CHAINS + c
            m_new = jnp.max(s, axis=0, keepdims=True)
            if not first:
                m_old = m_ref[state]
                m_new = jnp.maximum(m_old, m_new)
                alpha = jnp.exp2(m_old - m_new)
            p = jnp.exp2(s - m_new).astype(BF16)
            pv = jnp.dot(load("v", tile, c), p, preferred_element_type=F32)
            if j + QK_AHEAD < len(seq):
                pending[j + QK_AHEAD] = scores(*seq[j + QK_AHEAD])
            acc = accumulator(qb, c)
            m_ref[state] = m_new
            acc[...] = pv if first else alpha * acc[...] + pv

    def bias(kind, c):
        if kind == "diag":
            return (lambda: causal_ref[...]) if is_fox(c) else (lambda: bnear_ref[0, c // 2])
        if kind == "prev" and not is_fox(c):
            return lambda: bnear_ref[1, c // 2]
        return None

    def steps(tile, kinds):
        order = [c0 + b for c0 in range(0, DIFF_CHAINS, 2) for b in (0, DIFF_CHAINS)]
        return [(tile, qb, c0 + e, bias(kind, c0 + e), kind == "diag")
                for c0 in order for qb, kind in kinds.items() for e in range(2)]

    base = Q_BLOCKS * g
    head = []
    for u in reversed(range(Q_BLOCKS)):
        head += steps(base + u, {j: "diag" if j == u else "prev" if j == u + 1 else "far"
                                 for j in range(u, Q_BLOCKS)})
    run(head)

    def body(r, carry):
        hi = base - 1 - 2 * r
        lo = hi - 1
        far = {j: "far" for j in range(Q_BLOCKS)}
        pl.when(r == 0)(lambda: run(steps(hi, {**far, 0: "prev"}) + steps(lo, far)))
        pl.when(r > 0)(lambda: run(steps(hi, far) + steps(lo, far)))
        return carry

    lax.fori_loop(0, base // 2, body, 0)

    for qb in range(Q_BLOCKS):
        for h in range(DIFF_HEADS):
            o = normalised(qb, 2 * h) - lam * normalised(qb, 2 * h + 1)
            o = o * lax.rsqrt(jnp.mean(o * o, axis=0, keepdims=True) + EPS) * gsub_ref[...]
            od_ref[0, h * LANES:(h + 1) * LANES, rows(qb)] = o.astype(BF16)
        for h in range(FOX_HEADS):
            o = normalised(qb, DIFF_CHAINS + h)
            of_ref[0, h * HEAD_DIM:(h + 1) * HEAD_DIM, rows(qb)] = o.astype(BF16)


def _attention(lam_vecs, g_subln, bias_near, qd, kd, vd, qf, kf, vf, decay, lam_init):
    batch, seq, _ = kd.shape
    const2 = lambda b, i: (0, 0)
    const4 = lambda b, i: (0, 0, 0, 0)
    kk = np.arange(TK)[:, None]
    qq = np.arange(TQ)[None, :]
    causal = jnp.asarray(np.where(kk <= qq, 0.0, -np.inf).astype(np.float32))
    lane = np.arange(LANES)
    pick = (lane[None] < DECAY_PARTS * FOX_HEADS) & (lane[None] % FOX_HEADS
                                                     == np.arange(FOX_HEADS)[:, None])
    pick = jnp.asarray(np.broadcast_to(pick[:, None, :], (FOX_HEADS, TQ, LANES)), BF16)
    g_rows = jnp.broadcast_to((g_subln * (1.0 - lam_init)).reshape(LANES, 1), (LANES, TQ))
    oblk = pl.BlockSpec((1, BRANCH_WIDTH, Q_BLOCKS * TQ), lambda b, i: (b, 0, i))
    qsel = pl.BlockSpec((2, 1, Q_BLOCKS * TQ, BRANCH_WIDTH), lambda b, i: (0, b, i, 0))
    full = pl.BlockSpec((1, seq, BRANCH_WIDTH), lambda b, i: (b, 0, 0))
    full_v = pl.BlockSpec((1, seq // TK, BRANCH_WIDTH, TK), lambda b, i: (b, 0, 0, 0))
    out = jax.ShapeDtypeStruct((batch, BRANCH_WIDTH, seq), BF16)
    return pl.pallas_call(
        functools.partial(_attn_kernel, lam_init),
        grid=(batch, seq // (Q_BLOCKS * TQ)),
        in_specs=[
            pl.BlockSpec(lam_vecs.shape, const2),
            pl.BlockSpec(g_rows.shape, const2),
            pl.BlockSpec(bias_near.shape, const4),
            pl.BlockSpec(causal.shape, const2),
            pl.BlockSpec(pick.shape, lambda b, i: (0, 0, 0)),
            qsel, full, full_v, qsel, full, full_v,
            pl.BlockSpec((1, seq, LANES), lambda b, i: (b, 0, 0)),
        ],
        out_specs=[oblk, oblk],
        out_shape=[out, out],
        scratch_shapes=[
            pltpu.VMEM((Q_BLOCKS * CHAINS, 1, TQ), F32),
            pltpu.VMEM((Q_BLOCKS * DIFF_CHAINS, LANES + SUM_ROWS, TQ), F32),
            pltpu.VMEM((Q_BLOCKS * FOX_HEADS, HEAD_DIM + SUM_ROWS, TQ), F32),
        ],
        compiler_params=pltpu.CompilerParams(
            dimension_semantics=("parallel", "arbitrary"), vmem_limit_bytes=VMEM_LIMIT),
        name="attention",
    )(lam_vecs, g_rows, bias_near, causal, pick, qd, kd, vd, qf, kf, vf, decay)


def _merge_kernel(x_ref, od_ref, of_ref, g_ref, wg_ref, wpa_ref, wpb_ref, wo_ref, y_ref):
    x = x_ref[...]
    h = _rms(x, g_ref[...]).astype(BF16)
    a = lax.dot_general(od_ref[0], wpa_ref[...], TN_DIMS, preferred_element_type=F32)
    b = lax.dot_general(of_ref[0], wpb_ref[...], TN_DIMS, preferred_element_type=F32)
    ga = jax.nn.sigmoid(jnp.dot(h, wg_ref[:, :D_MODEL], preferred_element_type=F32))
    merged = ga * a
    gb = jax.nn.sigmoid(jnp.dot(h, wg_ref[:, D_MODEL:], preferred_element_type=F32))
    merged = (merged + gb * b).astype(BF16)
    y_ref[...] = x + jnp.dot(merged, wo_ref[...], preferred_element_type=F32)


def _merge(x2, od, of, g_mix, w_gate, w_pa, w_pb, w_o):
    n = x2.shape[0]
    tm = TM_PROJ
    const = lambda i: (0, 0)
    row = lambda i: (i, 0)
    tiles_per_seq = od.shape[2] // tm
    col = pl.BlockSpec((1, BRANCH_WIDTH, tm), lambda i: (i // tiles_per_seq, 0, i % tiles_per_seq))
    return pl.pallas_call(
        _merge_kernel,
        grid=(n // tm,),
        in_specs=[
            pl.BlockSpec((tm, D_MODEL), row),
            col,
            col,
            pl.BlockSpec((1, D_MODEL), const),
            pl.BlockSpec(w_gate.shape, const),
            pl.BlockSpec(w_pa.shape, const),
            pl.BlockSpec(w_pb.shape, const),
            pl.BlockSpec(w_o.shape, const),
        ],
        out_specs=pl.BlockSpec((tm, D_MODEL), row),
        out_shape=jax.ShapeDtypeStruct((n, D_MODEL), F32),
        compiler_params=pltpu.CompilerParams(
            dimension_semantics=("parallel",), vmem_limit_bytes=VMEM_LIMIT),
        name="merge",
    )(x2, od, of, g_mix, w_gate, w_pa, w_pb, w_o)


FF_CHUNK = 1024


def _mlp_kernel(final_norm, x_ref, g_ref, w1_ref, w2_ref, gf_ref, y_ref):
    x = x_ref[...]
    h = _rms(x, g_ref[...]).astype(BF16)
    y = x
    for c in range(D_FF // FF_CHUNK):
        cols = slice(c * FF_CHUNK, (c + 1) * FF_CHUNK)
        u = jnp.maximum(jnp.dot(h, w1_ref[:, cols], preferred_element_type=F32), 0.0)
        y = y + jnp.dot((u * u).astype(BF16), w2_ref[cols, :], preferred_element_type=F32)
    y_ref[...] = _rms(y, gf_ref[...]) if final_norm else y


def _mlp(x2, g_mlp, w_1, w_2, g_final, final_norm):
    n = x2.shape[0]
    tm = TM_PROJ
    const = lambda i: (0, 0)
    row = lambda i: (i, 0)
    single = pl.Buffered(1)
    return pl.pallas_call(
        functools.partial(_mlp_kernel, final_norm),
        grid=(n // tm,),
        in_specs=[
            pl.BlockSpec((tm, D_MODEL), row),
            pl.BlockSpec((1, D_MODEL), const),
            pl.BlockSpec(w_1.shape, const, pipeline_mode=single),
            pl.BlockSpec(w_2.shape, const, pipeline_mode=single),
            pl.BlockSpec((1, D_MODEL), const),
        ],
        out_specs=pl.BlockSpec((tm, D_MODEL), row),
        out_shape=jax.ShapeDtypeStruct((n, D_MODEL), F32),
        compiler_params=pltpu.CompilerParams(
            dimension_semantics=("parallel",), vmem_limit_bytes=VMEM_LIMIT),
        name="mlp",
    )(x2, g_mlp, w_1, w_2, g_final)


def _layer(x, layer_idx, g_mix, w_in, b_f, lam_q1, lam_k1, lam_q2, lam_k2, g_subln,
           w_pa, w_pb, w_o, g_mlp, w_1, w_2, bias_near, g_final, final_norm):
    batch, seq, d = x.shape
    n = batch * seq
    x2 = x.reshape(n, d)
    qkv_cols = 6 * BRANCH_WIDTH
    pad = LANES - DECAY_PARTS * FOX_HEADS
    w_tail = lax.optimization_barrier(w_in[:, qkv_cols:])
    w_fl = jnp.pad(jnp.tile(w_tail[:, :FOX_HEADS], (1, DECAY_PARTS)),
                   ((0, 0), (0, pad))).astype(BF16)
    b_fl = jnp.pad(jnp.tile(b_f, DECAY_PARTS), (0, pad)).reshape(1, LANES)
    w_gate = w_tail[:, FOX_HEADS:].astype(BF16)
    g_mix2 = g_mix.reshape(1, d)

    qd, kd, vd, qf, kf, vf, decay = _in_proj(x2, g_mix2, w_in, w_fl, b_fl, seq)

    lam_vecs = jnp.stack([lam_q1, lam_k1, lam_q2, lam_k2]).astype(F32)
    shape3 = (batch, seq, BRANCH_WIDTH)
    shape4 = (2,) + shape3
    od, of = _attention(lam_vecs, g_subln, bias_near,
                        qd.reshape(shape4), kd.reshape(shape3), vd,
                        qf.reshape(shape4), kf.reshape(shape3), vf,
                        decay.reshape(batch, seq, LANES), _lambda_init(layer_idx))

    x1 = _merge(x2, od, of, g_mix2,
                w_gate, w_pa.astype(BF16), w_pb.astype(BF16), w_o.astype(BF16))
    y = _mlp(x1, g_mlp.reshape(1, d), w_1.astype(BF16), w_2.astype(BF16), g_final, final_norm)
    return y.reshape(batch, seq, d)


def kernel(x, g_mix, w_in, b_f, lam_q1, lam_k1, lam_q2, lam_k2, g_subln, w_pa, w_pb, w_o,
           g_mlp, w_1, w_2, rel_table, g_final):
    depth = g_mix.shape[0]
    batch, seq, d = x.shape
    assert d == D_MODEL and w_in.shape[1:] == (D_MODEL, 6 * BRANCH_WIDTH + FOX_HEADS + 2 * D_MODEL)
    assert w_1.shape[1:] == (D_MODEL, D_FF) and rel_table.shape == (REL_BUCKETS, DIFF_HEADS)
    assert seq % (Q_BLOCKS * TQ) == 0 and seq % TM_IN_PROJ == 0 and seq % TM_PROJ == 0
    bias_near = _bias_tiles(rel_table, seq)
    for l in range(depth):
        x = _layer(x, l, g_mix[l], w_in[l], b_f[l], lam_q1[l], lam_k1[l], lam_q2[l], lam_k2[l],
                   g_subln[l], w_pa[l], w_pb[l], w_o[l], g_mlp[l], w_1[l], w_2[l],
                   bias_near, g_final.reshape(1, -1), l == depth - 1)
    return x
```

```python
import functools
import math

import numpy as np
import jax
import jax.numpy as jnp
from jax import lax
from jax.experimental import pallas as pl
from jax.experimental.pallas import tpu as pltpu

D_MODEL = 1024
CHUNK = 64
HEAD_DIM = 64
DIFF_HEADS = 4
FOX_HEADS = 8
BRANCH_WIDTH = 512
D_FF = 4 * D_MODEL
REL_BUCKETS = 32
REL_MAX_DIST = 128
EPS = 1e-6
LANES = 128
MASKED_BUCKET = REL_BUCKETS

TQ = 256
TK = 256
TM_IN_PROJ = 512
TM_PROJ = 1024
V7X_VMEM_BYTES = 64 * 1024 * 1024
VMEM_LIMIT = V7X_VMEM_BYTES * 7 // 8

LOG2E = math.log2(math.e)
Q_SCALE = HEAD_DIM ** -0.5 * LOG2E

F32 = jnp.float32
BF16 = jnp.bfloat16
NT_DIMS = (((1,), (1,)), ((), ()))
TN_DIMS = (((0,), (0,)), ((), ()))


def _lambda_init(layer_idx):
    return 0.8 - 0.6 * math.exp(-0.3 * layer_idx)


def _rms(xf, g):
    return xf * lax.rsqrt(jnp.mean(xf * xf, axis=-1, keepdims=True) + EPS) * g


DECAY_PARTS = 3


def _in_proj_kernel(tiles_per_seq, x0_ref, xnext_ref, g_ref, w32_ref, wfl_ref, bf_ref,
                    qd_ref, kd_ref, vd_ref, qf_ref, kf_ref, vf_ref, dec_ref,
                    carry_ref, w_ref, h_ref):
    step = pl.program_id(0)

    @pl.when(step == 0)
    def _():
        w_ref[...] = w32_ref[...].astype(BF16)
        h_ref[0] = _rms(x0_ref[...], g_ref[...]).astype(BF16)

    h = h_ref[step % 2]

    @pl.when(step % tiles_per_seq == 0)
    def _():
        carry_ref[...] = jnp.zeros_like(carry_ref)

    z = jnp.dot(h, wfl_ref[...], preferred_element_type=F32) + bf_ref[...]
    acc = jnp.minimum(z, 0.0) - jnp.log1p(jnp.exp(-jnp.abs(z)))
    rows = acc.shape[0]
    row = lax.broadcasted_iota(jnp.int32, acc.shape, 0)
    d = 1
    while d < rows:
        acc = acc + jnp.where(row >= d, pltpu.roll(acc, d, axis=0), 0.0)
        d *= 2
    acc = acc + carry_ref[...]
    carry_ref[...] = acc[rows - 1:rows, :]
    neg = acc * -LOG2E
    hi = neg.astype(BF16).astype(F32)
    mid = (neg - hi).astype(BF16).astype(F32)
    lo = neg - hi - mid
    lane = lax.broadcasted_iota(jnp.int32, acc.shape, 1)
    piece = jnp.where(lane < FOX_HEADS, hi, jnp.where(lane < 2 * FOX_HEADS, mid, lo))
    dec_ref[...] = jnp.where(lane < DECAY_PARTS * FOX_HEADS, piece, 0.0).astype(BF16)

    outs = (qd_ref, kd_ref, vd_ref, qf_ref, kf_ref, vf_ref)
    for c in (2, 5, 0, 3, 1, 4):
        o_ref = outs[c]
        w = w_ref[:, c * BRANCH_WIDTH:(c + 1) * BRANCH_WIDTH]
        o = jnp.dot(h, w, preferred_element_type=F32)
        if o_ref is qd_ref or o_ref is qf_ref:
            o = (o * Q_SCALE).astype(BF16)
            low = lax.broadcasted_iota(jnp.int32, o.shape, 1) % LANES < HEAD_DIM
            o_ref[0] = jnp.where(low, o, jnp.zeros_like(o))
            o_ref[1] = jnp.where(low, jnp.zeros_like(o), o)
        elif o_ref is vd_ref or o_ref is vf_ref:
            o_t = o.T.astype(BF16)
            for t in range(o_ref.shape[1]):
                o_ref[0, t] = o_t[:, t * TK:(t + 1) * TK]
        else:
            o_ref[...] = o.astype(BF16)

    h_ref[(step + 1) % 2] = _rms(xnext_ref[...], g_ref[...]).astype(BF16)


def _in_proj(x2, g_mix, w_in, w_fl, b_f, seq):
    n = x2.shape[0]
    tm = TM_IN_PROJ
    qkv_cols = 6 * BRANCH_WIDTH
    const = lambda i: (0, 0)
    row = lambda i: (i, 0)
    out_bf = jax.ShapeDtypeStruct((n, BRANCH_WIDTH), BF16)
    out_q = jax.ShapeDtypeStruct((2, n, BRANCH_WIDTH), BF16)
    spec_bf = pl.BlockSpec((tm, BRANCH_WIDTH), row)
    spec_q = pl.BlockSpec((2, tm, BRANCH_WIDTH), lambda i: (0, i, 0))
    tiles_per_seq = seq // tm
    out_v = jax.ShapeDtypeStruct((n // seq, seq // TK, BRANCH_WIDTH, TK), BF16)
    spec_v = pl.BlockSpec((1, tm // TK, BRANCH_WIDTH, TK),
                          lambda i: (i // tiles_per_seq, i % tiles_per_seq, 0, 0))
    return pl.pallas_call(
        functools.partial(_in_proj_kernel, seq // tm),
        grid=(n // tm,),
        in_specs=[
            pl.BlockSpec((tm, D_MODEL), const),
            pl.BlockSpec((tm, D_MODEL), lambda i: (jnp.minimum(i + 1, n // tm - 1), 0)),
            pl.BlockSpec((1, D_MODEL), const),
            pl.BlockSpec((D_MODEL, qkv_cols), const, pipeline_mode=pl.Buffered(1)),
            pl.BlockSpec(w_fl.shape, const),
            pl.BlockSpec((1, LANES), const),
        ],
        out_specs=[spec_q, spec_bf, spec_v, spec_q, spec_bf, spec_v,
                   pl.BlockSpec((tm, LANES), row)],
        out_shape=[out_q, out_bf, out_v, out_q, out_bf, out_v,
                   jax.ShapeDtypeStruct((n, LANES), BF16)],
        scratch_shapes=[pltpu.VMEM((1, LANES), F32),
                        pltpu.VMEM((D_MODEL, qkv_cols), BF16),
                        pltpu.VMEM((2, tm, D_MODEL), BF16)],
        compiler_params=pltpu.CompilerParams(
            dimension_semantics=("arbitrary",), vmem_limit_bytes=VMEM_LIMIT),
        name="in_proj",
    )(x2, x2, g_mix, w_in, w_fl, b_f)


def _rel_bucket_np(rel):
    nb = REL_BUCKETS // 2
    ret = np.where(rel > 0, nb, 0)
    n = np.abs(rel)
    max_exact = nb // 2
    nf = np.maximum(n, 1).astype(np.float64)
    large = max_exact + (np.log(nf / max_exact) / math.log(REL_MAX_DIST / max_exact)
                         * (nb - max_exact)).astype(np.int32)
    large = np.minimum(large, nb - 1)
    return (ret + np.where(n < max_exact, n, large)).astype(np.int32)


def _bias_index_maps(seq):
    kk = np.arange(TK, dtype=np.int64)[:, None]
    qq = np.arange(TQ, dtype=np.int64)[None, :]
    diag = _rel_bucket_np(kk - qq)
    diag = np.where(kk // CHUNK <= qq // CHUNK, diag, MASKED_BUCKET).astype(np.int32)
    prev = _rel_bucket_np(kk - TK - qq)
    far = _rel_bucket_np(np.arange(-seq, -TK, dtype=np.int64))
    far_bucket = int(far[0])
    assert (far == far_bucket).all(), "keys two tiles back must share one bucket"
    return diag, prev, far_bucket


def _bias_kernel(far_bucket, buckets, tab_ref, idx_ref, out_ref):
    for h in range(DIFF_HEADS):
        far = tab_ref[far_bucket, h]
        for t in range(idx_ref.shape[0]):
            idx = idx_ref[t]
            acc = jnp.full(idx.shape, -jnp.inf, F32)
            for b in buckets[t]:
                acc = jnp.where(idx == b, (tab_ref[b, h] - far) * LOG2E, acc)
            out_ref[t, h] = acc


def _bias_tiles(rel_table, seq):
    diag, prev, far_bucket = _bias_index_maps(seq)
    idx = np.stack([diag, prev])
    buckets = tuple(tuple(int(b) for b in np.unique(m) if b != MASKED_BUCKET) for m in idx)
    vmem = pl.BlockSpec(memory_space=pltpu.VMEM)
    return pl.pallas_call(
        functools.partial(_bias_kernel, far_bucket, buckets),
        in_specs=[pl.BlockSpec(memory_space=pltpu.SMEM), vmem],
        out_specs=vmem,
        out_shape=jax.ShapeDtypeStruct((idx.shape[0], DIFF_HEADS, TK, TQ), F32),
        name="bias_tiles",
    )(rel_table, jnp.asarray(idx))


DIFF_CHAINS = 2 * DIFF_HEADS
CHAINS = DIFF_CHAINS + FOX_HEADS
QK_AHEAD = 4
Q_BLOCKS = 4
SUM_ROWS = 16


def _attn_kernel(lam_init, lam_ref, gsub_ref, bnear_ref, causal_ref, pick_ref,
                 qd_ref, kd_ref, vd_ref, qf_ref, kf_ref, vf_ref, dec_ref,
                 od_ref, of_ref, m_ref, accd_ref, accf_ref):
    g = pl.program_id(1)
    lam_v = lam_ref[...]
    lam = (jnp.exp(jnp.sum(lam_v[0:1] * lam_v[1:2], axis=-1, keepdims=True))
           - jnp.exp(jnp.sum(lam_v[2:3] * lam_v[3:4], axis=-1, keepdims=True))
           + lam_init)

    def is_fox(c):
        return c >= DIFF_CHAINS

    def cols(c):
        blk = (c % DIFF_CHAINS) // 2
        return slice(blk * LANES, (blk + 1) * LANES)

    def rows(qb):
        return slice(qb * TQ, (qb + 1) * TQ)

    def accumulator(qb, c):
        if is_fox(c):
            return accf_ref.at[qb * FOX_HEADS + c - DIFF_CHAINS]
        return accd_ref.at[qb * DIFF_CHAINS + c]

    def normalised(qb, c):
        acc = accumulator(qb, c)[...]
        chans = acc.shape[0] - SUM_ROWS
        return acc[:chans] * (1.0 / acc[chans:chans + 1])

    def run(seq):
        tiles = {}

        def load(kind, tile, c):
            if kind == "v":
                chan = (slice((c - DIFF_CHAINS) * HEAD_DIM, (c - DIFF_CHAINS + 1) * HEAD_DIM)
                        if is_fox(c) else cols(c))
                key = (kind, id(tile), is_fox(c), chan.start)
                if key not in tiles:
                    v_t = (vf_ref if is_fox(c) else vd_ref)[0, tile, chan, :]
                    tiles[key] = jnp.concatenate([v_t, jnp.ones((SUM_ROWS, TK), BF16)], axis=0)
                return tiles[key]
            key = (kind, id(tile), is_fox(c), cols(c).start)
            if key not in tiles:
                at = pl.ds(pl.multiple_of(tile * TK, TK), TK)
                if is_fox(c):
                    tiles[key] = jnp.concatenate([kf_ref[0, at, cols(c)], dec_ref[0, at, :]],
                                                 axis=1)
                else:
                    tiles[key] = kd_ref[0, at, cols(c)]
            return tiles[key]

        def scores(tile, qb, c, add, first):
            if is_fox(c):
                q_t = jnp.concatenate([qf_ref[c % 2, 0, rows(qb), cols(c)],
                                       pick_ref[c - DIFF_CHAINS]], axis=1)
            else:
                q_t = qd_ref[c % 2, 0, rows(qb), cols(c)]
            s = lax.dot_general(load("k", tile, c), q_t, NT_DIMS, preferred_element_type=F32)
            return s if add is None else s + add()

        pending = {j: scores(*seq[j]) for j in range(min(QK_AHEAD, len(seq)))}
        for j, (tile, qb, c, _, first) in enumerate(seq):
            s = pending.pop(j)
            state = qb * CHAINS + c
            m_new = jnp.max(s, axis=0, keepdims=True)
            if not first:
                m_old = m_ref[state]
                m_new = jnp.maximum(m_old, m_new)
                alpha = jnp.exp2(m_old - m_new)
            p = jnp.exp2(s - m_new).astype(BF16)
            pv = jnp.dot(load("v", tile, c), p, preferred_element_type=F32)
            if j + QK_AHEAD < len(seq):
                pending[j + QK_AHEAD] = scores(*seq[j + QK_AHEAD])
            acc = accumulator(qb, c)
            m_ref[state] = m_new
            acc[...] = pv if first else alpha * acc[...] + pv

    def bias(kind, c):
        if kind == "diag":
            return (lambda: causal_ref[...]) if is_fox(c) else (lambda: bnear_ref[0, c // 2])
        if kind == "prev" and not is_fox(c):
            return lambda: bnear_ref[1, c // 2]
        return None

    def steps(tile, kinds):
        order = [c0 + b for c0 in range(0, DIFF_CHAINS, 2) for b in (0, DIFF_CHAINS)]
        return [(tile, qb, c0 + e, bias(kind, c0 + e), kind == "diag")
                for c0 in order for qb, kind in kinds.items() for e in range(2)]

    base = Q_BLOCKS * g
    head = []
    for u in reversed(range(Q_BLOCKS)):
        head += steps(base + u, {j: "diag" if j == u else "prev" if j == u + 1 else "far"
                                 for j in range(u, Q_BLOCKS)})
    run(head)

    def body(r, carry):
        hi = base - 1 - 2 * r
        lo = hi - 1
        far = {j: "far" for j in range(Q_BLOCKS)}
        pl.when(r == 0)(lambda: run(steps(hi, {**far, 0: "prev"}) + steps(lo, far)))
        pl.when(r > 0)(lambda: run(steps(hi, far) + steps(lo, far)))
        return carry

    lax.fori_loop(0, base // 2, body, 0)

    for qb in range(Q_BLOCKS):
        for h in range(DIFF_HEADS):
            o = normalised(qb, 2 * h) - lam * normalised(qb, 2 * h + 1)
            o = o * lax.rsqrt(jnp.mean(o * o, axis=0, keepdims=True) + EPS) * gsub_ref[...]
            od_ref[0, h * LANES:(h + 1) * LANES, rows(qb)] = o.astype(BF16)
        for h in range(FOX_HEADS):
            o = normalised(qb, DIFF_CHAINS + h)
            of_ref[0, h * HEAD_DIM:(h + 1) * HEAD_DIM, rows(qb)] = o.astype(BF16)


def _attention(lam_vecs, g_subln, bias_near, qd, kd, vd, qf, kf, vf, decay, lam_init):
    batch, seq, _ = kd.shape
    const2 = lambda b, i: (0, 0)
    const4 = lambda b, i: (0, 0, 0, 0)
    kk = np.arange(TK)[:, None]
    qq = np.arange(TQ)[None, :]
    causal = jnp.asarray(np.where(kk <= qq, 0.0, -np.inf).astype(np.float32))
    lane = np.arange(LANES)
    pick = (lane[None] < DECAY_PARTS * FOX_HEADS) & (lane[None] % FOX_HEADS
                                                     == np.arange(FOX_HEADS)[:, None])
    pick = jnp.asarray(np.broadcast_to(pick[:, None, :], (FOX_HEADS, TQ, LANES)), BF16)
    g_rows = jnp.broadcast_to((g_subln * (1.0 - lam_init)).reshape(LANES, 1), (LANES, TQ))
    oblk = pl.BlockSpec((1, BRANCH_WIDTH, Q_BLOCKS * TQ), lambda b, i: (b, 0, i))
    qsel = pl.BlockSpec((2, 1, Q_BLOCKS * TQ, BRANCH_WIDTH), lambda b, i: (0, b, i, 0))
    full = pl.BlockSpec((1, seq, BRANCH_WIDTH), lambda b, i: (b, 0, 0))
    full_v = pl.BlockSpec((1, seq // TK, BRANCH_WIDTH, TK), lambda b, i: (b, 0, 0, 0))
    out = jax.ShapeDtypeStruct((batch, BRANCH_WIDTH, seq), BF16)
    return pl.pallas_call(
        functools.partial(_attn_kernel, lam_init),
        grid=(batch, seq // (Q_BLOCKS * TQ)),
        in_specs=[
            pl.BlockSpec(lam_vecs.shape, const2),
            pl.BlockSpec(g_rows.shape, const2),
            pl.BlockSpec(bias_near.shape, const4),
            pl.BlockSpec(causal.shape, const2),
            pl.BlockSpec(pick.shape, lambda b, i: (0, 0, 0)),
            qsel, full, full_v, qsel, full, full_v,
            pl.BlockSpec((1, seq, LANES), lambda b, i: (b, 0, 0)),
        ],
        out_specs=[oblk, oblk],
        out_shape=[out, out],
        scratch_shapes=[
            pltpu.VMEM((Q_BLOCKS * CHAINS, 1, TQ), F32),
            pltpu.VMEM((Q_BLOCKS * DIFF_CHAINS, LANES + SUM_ROWS, TQ), F32),
            pltpu.VMEM((Q_BLOCKS * FOX_HEADS, HEAD_DIM + SUM_ROWS, TQ), F32),
        ],
        compiler_params=pltpu.CompilerParams(
            dimension_semantics=("parallel", "arbitrary"), vmem_limit_bytes=VMEM_LIMIT),
        name="attention",
    )(lam_vecs, g_rows, bias_near, causal, pick, qd, kd, vd, qf, kf, vf, decay)


def _merge_kernel(x_ref, od_ref, of_ref, g_ref, wg_ref, wpa_ref, wpb_ref, wo_ref, y_ref):
    x = x_ref[...]
    h = _rms(x, g_ref[...]).astype(BF16)
    a = lax.dot_general(od_ref[0], wpa_ref[...], TN_DIMS, preferred_element_type=F32)
    b = lax.dot_general(of_ref[0], wpb_ref[...], TN_DIMS, preferred_element_type=F32)
    ga = jax.nn.sigmoid(jnp.dot(h, wg_ref[:, :D_MODEL], preferred_element_type=F32))
    merged = ga * a
    gb = jax.nn.sigmoid(jnp.dot(h, wg_ref[:, D_MODEL:], preferred_element_type=F32))
    merged = (merged + gb * b).astype(BF16)
    y_ref[...] = x + jnp.dot(merged, wo_ref[...], preferred_element_type=F32)


def _merge(x2, od, of, g_mix, w_gate, w_pa, w_pb, w_o):
    n = x2.shape[0]
    tm = TM_PROJ
    const = lambda i: (0, 0)
    row = lambda i: (i, 0)
    tiles_per_seq = od.shape[2] // tm
    col = pl.BlockSpec((1, BRANCH_WIDTH, tm), lambda i: (i // tiles_per_seq, 0, i % tiles_per_seq))
    return pl.pallas_call(
        _merge_kernel,
        grid=(n // tm,),
        in_specs=[
            pl.BlockSpec((tm, D_MODEL), row),
            col,
            col,
            pl.BlockSpec((1, D_MODEL), const),
            pl.BlockSpec(w_gate.shape, const),
            pl.BlockSpec(w_pa.shape, const),
            pl.BlockSpec(w_pb.shape, const),
            pl.BlockSpec(w_o.shape, const),
        ],
        out_specs=pl.BlockSpec((tm, D_MODEL), row),
        out_shape=jax.ShapeDtypeStruct((n, D_MODEL), F32),
        compiler_params=pltpu.CompilerParams(
            dimension_semantics=("parallel",), vmem_limit_bytes=VMEM_LIMIT),
        name="merge",
    )(x2, od, of, g_mix, w_gate, w_pa, w_pb, w_o)


FF_CHUNK = 1024


def _mlp_kernel(final_norm, x_ref, g_ref, w1_ref, w2_ref, gf_ref, y_ref):
    x = x_ref[...]
    h = _rms(x, g_ref[...]).astype(BF16)
    y = x
    for c in range(D_FF // FF_CHUNK):
        cols = slice(c * FF_CHUNK, (c + 1) * FF_CHUNK)
        u = jnp.maximum(jnp.dot(h, w1_ref[:, cols], preferred_element_type=F32), 0.0)
        y = y + jnp.dot((u * u).astype(BF16), w2_ref[cols, :], preferred_element_type=F32)
    y_ref[...] = _rms(y, gf_ref[...]) if final_norm else y


def _mlp(x2, g_mlp, w_1, w_2, g_final, final_norm):
    n = x2.shape[0]
    tm = TM_PROJ
    const = lambda i: (0, 0)
    row = lambda i: (i, 0)
    single = pl.Buffered(1)
    return pl.pallas_call(
        functools.partial(_mlp_kernel, final_norm),
        grid=(n // tm,),
        in_specs=[
            pl.BlockSpec((tm, D_MODEL), row),
            pl.BlockSpec((1, D_MODEL), const),
            pl.BlockSpec(w_1.shape, const, pipeline_mode=single),
            pl.BlockSpec(w_2.shape, const, pipeline_mode=single),
            pl.BlockSpec((1, D_MODEL), const),
        ],
        out_specs=pl.BlockSpec((tm, D_MODEL), row),
        out_shape=jax.ShapeDtypeStruct((n, D_MODEL), F32),
        compiler_params=pltpu.CompilerParams(
            dimension_semantics=("parallel",), vmem_limit_bytes=VMEM_LIMIT),
        name="mlp",
    )(x2, g_mlp, w_1, w_2, g_final)


def _layer(x, layer_idx, g_mix, w_in, b_f, lam_q1, lam_k1, lam_q2, lam_k2, g_subln,
           w_pa, w_pb, w_o, g_mlp, w_1, w_2, bias_near, g_final, final_norm):
    batch, seq, d = x.shape
    n = batch * seq
    x2 = x.reshape(n, d)
    qkv_cols = 6 * BRANCH_WIDTH
    pad = LANES - DECAY_PARTS * FOX_HEADS
    w_tail = lax.optimization_barrier(w_in[:, qkv_cols:])
    w_fl = jnp.pad(jnp.tile(w_tail[:, :FOX_HEADS], (1, DECAY_PARTS)),
                   ((0, 0), (0, pad))).astype(BF16)
    b_fl = jnp.pad(jnp.tile(b_f, DECAY_PARTS), (0, pad)).reshape(1, LANES)
    w_gate = w_tail[:, FOX_HEADS:].astype(BF16)
    g_mix2 = g_mix.reshape(1, d)

    qd, kd, vd, qf, kf, vf, decay = _in_proj(x2, g_mix2, w_in, w_fl, b_fl, seq)

    lam_vecs = jnp.stack([lam_q1, lam_k1, lam_q2, lam_k2]).astype(F32)
    shape3 = (batch, seq, BRANCH_WIDTH)
    shape4 = (2,) + shape3
    od, of = _attention(lam_vecs, g_subln, bias_near,
                        qd.reshape(shape4), kd.reshape(shape3), vd,
                        qf.reshape(shape4), kf.reshape(shape3), vf,
                        decay.reshape(batch, seq, LANES), _lambda_init(layer_idx))

    x1 = _merge(x2, od, of, g_mix2,
                w_gate, w_pa.astype(BF16), w_pb.astype(BF16), w_o.astype(BF16))
    y = _mlp(x1, g_mlp.reshape(1, d), w_1.astype(BF16), w_2.astype(BF16), g_final, final_norm)
    return y.reshape(batch, seq, d)


def kernel(x, g_mix, w_in, b_f, lam_q1, lam_k1, lam_q2, lam_k2, g_subln, w_pa, w_pb, w_o,
           g_mlp, w_1, w_2, rel_table, g_final):
    depth = g_mix.shape[0]
    batch, seq, d = x.shape
    assert d == D_MODEL and w_in.shape[1:] == (D_MODEL, 6 * BRANCH_WIDTH + FOX_HEADS + 2 * D_MODEL)
    assert w_1.shape[1:] == (D_MODEL, D_FF) and rel_table.shape == (REL_BUCKETS, DIFF_HEADS)
    assert seq % (Q_BLOCKS * TQ) == 0 and seq % TM_IN_PROJ == 0 and seq % TM_PROJ == 0
    bias_near = _bias_tiles(rel_table, seq)
    for l in range(depth):
        x = _layer(x, l, g_mix[l], w_in[l], b_f[l], lam_q1[l], lam_k1[l], lam_q2[l], lam_k2[l],
                   g_subln[l], w_pa[l], w_pb[l], w_o[l], g_mlp[l], w_1[l], w_2[l],
                   bias_near, g_final.reshape(1, -1), l == depth - 1)
    return x
```

```python
import functools
import math

import numpy as np
import jax
import jax.numpy as jnp
from jax import lax
from jax.experimental import pallas as pl
from jax.experimental.pallas import tpu as pltpu

D_MODEL = 1024
CHUNK = 64
HEAD_DIM = 64
DIFF_HEADS = 4
FOX_HEADS = 8
BRANCH_WIDTH = 512
D_FF = 4 * D_MODEL
REL_BUCKETS = 32
REL_MAX_DIST = 128
EPS = 1e-6
LANES = 128
MASKED_BUCKET = REL_BUCKETS

TQ = 256
TK = 256
TM_IN_PROJ = 512
TM_PROJ = 1024
V7X_VMEM_BYTES = 64 * 1024 * 1024
VMEM_LIMIT = V7X_VMEM_BYTES * 7 // 8

LOG2E = math.log2(math.e)
Q_SCALE = HEAD_DIM ** -0.5 * LOG2E

F32 = jnp.float32
BF16 = jnp.bfloat16
NT_DIMS = (((1,), (1,)), ((), ()))
TN_DIMS = (((0,), (0,)), ((), ()))


def _lambda_init(layer_idx):
    return 0.8 - 0.6 * math.exp(-0.3 * layer_idx)


def _rms(xf, g):
    return xf * lax.rsqrt(jnp.mean(xf * xf, axis=-1, keepdims=True) + EPS) * g


DECAY_PARTS = 3


def _in_proj_kernel(tiles_per_seq, x0_ref, xnext_ref, g_ref, w32_ref, wfl_ref, bf_ref,
                    qd_ref, kd_ref, vd_ref, qf_ref, kf_ref, vf_ref, dec_ref,
                    carry_ref, w_ref, h_ref):
    step = pl.program_id(0)

    @pl.when(step == 0)
    def _():
        w_ref[...] = w32_ref[...].astype(BF16)
        h_ref[0] = _rms(x0_ref[...], g_ref[...]).astype(BF16)

    h = h_ref[step % 2]

    @pl.when(step % tiles_per_seq == 0)
    def _():
        carry_ref[...] = jnp.zeros_like(carry_ref)

    z = jnp.dot(h, wfl_ref[...], preferred_element_type=F32) + bf_ref[...]
    acc = jnp.minimum(z, 0.0) - jnp.log1p(jnp.exp(-jnp.abs(z)))
    rows = acc.shape[0]
    row = lax.broadcasted_iota(jnp.int32, acc.shape, 0)
    d = 1
    while d < rows:
        acc = acc + jnp.where(row >= d, pltpu.roll(acc, d, axis=0), 0.0)
        d *= 2
    acc = acc + carry_ref[...]
    carry_ref[...] = acc[rows - 1:rows, :]
    neg = acc * -LOG2E
    hi = neg.astype(BF16).astype(F32)
    mid = (neg - hi).astype(BF16).astype(F32)
    lo = neg - hi - mid
    lane = lax.broadcasted_iota(jnp.int32, acc.shape, 1)
    piece = jnp.where(lane < FOX_HEADS, hi, jnp.where(lane < 2 * FOX_HEADS, mid, lo))
    dec_ref[...] = jnp.where(lane < DECAY_PARTS * FOX_HEADS, piece, 0.0).astype(BF16)

    outs = (qd_ref, kd_ref, vd_ref, qf_ref, kf_ref, vf_ref)
    for c in (2, 5, 0, 3, 1, 4):
        o_ref = outs[c]
        w = w_ref[:, c * BRANCH_WIDTH:(c + 1) * BRANCH_WIDTH]
        o = jnp.dot(h, w, preferred_element_type=F32)
        if o_ref is qd_ref or o_ref is qf_ref:
            o = (o * Q_SCALE).astype(BF16)
            low = lax.broadcasted_iota(jnp.int32, o.shape, 1) % LANES < HEAD_DIM
            o_ref[0] = jnp.where(low, o, jnp.zeros_like(o))
            o_ref[1] = jnp.where(low, jnp.zeros_like(o), o)
        elif o_ref is vd_ref or o_ref is vf_ref:
            o_t = o.T.astype(BF16)
            for t in range(o_ref.shape[1]):
                o_ref[0, t] = o_t[:, t * TK:(t + 1) * TK]
        else:
            o_ref[...] = o.astype(BF16)

    h_ref[(step + 1) % 2] = _rms(xnext_ref[...], g_ref[...]).astype(BF16)


def _in_proj(x2, g_mix, w_in, w_fl, b_f, seq):
    n = x2.shape[0]
    tm = TM_IN_PROJ
    qkv_cols = 6 * BRANCH_WIDTH
    const = lambda i: (0, 0)
    row = lambda i: (i, 0)
    out_bf = jax.ShapeDtypeStruct((n, BRANCH_WIDTH), BF16)
    out_q = jax.ShapeDtypeStruct((2, n, BRANCH_WIDTH), BF16)
    spec_bf = pl.BlockSpec((tm, BRANCH_WIDTH), row)
    spec_q = pl.BlockSpec((2, tm, BRANCH_WIDTH), lambda i: (0, i, 0))
    tiles_per_seq = seq // tm
    out_v = jax.ShapeDtypeStruct((n // seq, seq // TK, BRANCH_WIDTH, TK), BF16)
    spec_v = pl.BlockSpec((1, tm // TK, BRANCH_WIDTH, TK),
                          lambda i: (i // tiles_per_seq, i % tiles_per_seq, 0, 0))
    return pl.pallas_call(
        functools.partial(_in_proj_kernel, seq // tm),
        grid=(n // tm,),
        in_specs=[
            pl.BlockSpec((tm, D_MODEL), const),
            pl.BlockSpec((tm, D_MODEL), lambda i: (jnp.minimum(i + 1, n // tm - 1), 0)),
            pl.BlockSpec((1, D_MODEL), const),
            pl.BlockSpec((D_MODEL, qkv_cols), const, pipeline_mode=pl.Buffered(1)),
            pl.BlockSpec(w_fl.shape, const),
            pl.BlockSpec((1, LANES), const),
        ],
        out_specs=[spec_q, spec_bf, spec_v, spec_q, spec_bf, spec_v,
                   pl.BlockSpec((tm, LANES), row)],
        out_shape=[out_q, out_bf, out_v, out_q, out_bf, out_v,
                   jax.ShapeDtypeStruct((n, LANES), BF16)],
        scratch_shapes=[pltpu.VMEM((1, LANES), F32),
                        pltpu.VMEM((D_MODEL, qkv_cols), BF16),
                        pltpu.VMEM((2, tm, D_MODEL), BF16)],
        compiler_params=pltpu.CompilerParams(
            dimension_semantics=("arbitrary",), vmem_limit_bytes=VMEM_LIMIT),
        name="in_proj",
    )(x2, x2, g_mix, w_in, w_fl, b_f)


def _rel_bucket_np(rel):
    nb = REL_BUCKETS // 2
    ret = np.where(rel > 0, nb, 0)
    n = np.abs(rel)
    max_exact = nb // 2
    nf = np.maximum(n, 1).astype(np.float64)
    large = max_exact + (np.log(nf / max_exact) / math.log(REL_MAX_DIST / max_exact)
                         * (nb - max_exact)).astype(np.int32)
    large = np.minimum(large, nb - 1)
    return (ret + np.where(n < max_exact, n, large)).astype(np.int32)


def _bias_index_maps(seq):
    kk = np.arange(TK, dtype=np.int64)[:, None]
    qq = np.arange(TQ, dtype=np.int64)[None, :]
    diag = _rel_bucket_np(kk - qq)
    diag = np.where(kk // CHUNK <= qq // CHUNK, diag, MASKED_BUCKET).astype(np.int32)
    prev = _rel_bucket_np(kk - TK - qq)
    far = _rel_bucket_np(np.arange(-seq, -TK, dtype=np.int64))
    far_bucket = int(far[0])
    assert (far == far_bucket).all(), "keys two tiles back must share one bucket"
    return diag, prev, far_bucket


def _bias_kernel(far_bucket, buckets, tab_ref, idx_ref, out_ref):
    for h in range(DIFF_HEADS):
        far = tab_ref[far_bucket, h]
        for t in range(idx_ref.shape[0]):
            idx = idx_ref[t]
            acc = jnp.full(idx.shape, -jnp.inf, F32)
            for b in buckets[t]:
                acc = jnp.where(idx == b, (tab_ref[b, h] - far) * LOG2E, acc)
            out_ref[t, h] = acc


def _bias_tiles(rel_table, seq):
    diag, prev, far_bucket = _bias_index_maps(seq)
    idx = np.stack([diag, prev])
    buckets = tuple(tuple(int(b) for b in np.unique(m) if b != MASKED_BUCKET) for m in idx)
    vmem = pl.BlockSpec(memory_space=pltpu.VMEM)
    return pl.pallas_call(
        functools.partial(_bias_kernel, far_bucket, buckets),
        in_specs=[pl.BlockSpec(memory_space=pltpu.SMEM), vmem],
        out_specs=vmem,
        out_shape=jax.ShapeDtypeStruct((idx.shape[0], DIFF_HEADS, TK, TQ), F32),
        name="bias_tiles",
    )(rel_table, jnp.asarray(idx))


DIFF_CHAINS = 2 * DIFF_HEADS
CHAINS = DIFF_CHAINS + FOX_HEADS
QK_AHEAD = 5
Q_BLOCKS = 4
SUM_ROWS = 16


def _attn_kernel(lam_init, lam_ref, gsub_ref, bnear_ref, causal_ref, pick_ref,
                 qd_ref, kd_ref, vd_ref, qf_ref, kf_ref, vf_ref, dec_ref,
                 od_ref, of_ref, m_ref, accd_ref, accf_ref):
    g = pl.program_id(1)
    lam_v = lam_ref[...]
    lam = (jnp.exp(jnp.sum(lam_v[0:1] * lam_v[1:2], axis=-1, keepdims=True))
           - jnp.exp(jnp.sum(lam_v[2:3] * lam_v[3:4], axis=-1, keepdims=True))
           + lam_init)

    def is_fox(c):
        return c >= DIFF_CHAINS

    def cols(c):
        blk = (c % DIFF_CHAINS) // 2
        return slice(blk * LANES, (blk + 1) * LANES)

    def rows(qb):
        return slice(qb * TQ, (qb + 1) * TQ)

    def accumulator(qb, c):
        if is_fox(c):
            return accf_ref.at[qb * FOX_HEADS + c - DIFF_CHAINS]
        return accd_ref.at[qb * DIFF_CHAINS + c]

    def normalised(qb, c):
        acc = accumulator(qb, c)[...]
        chans = acc.shape[0] - SUM_ROWS
        return acc[:chans] * (1.0 / acc[chans:chans + 1])

    def run(seq):
        tiles = {}

        def load(kind, tile, c):
            if kind == "v":
                chan = (slice((c - DIFF_CHAINS) * HEAD_DIM, (c - DIFF_CHAINS + 1) * HEAD_DIM)
                        if is_fox(c) else cols(c))
                key = (kind, id(tile), is_fox(c), chan.start)
                if key not in tiles:
                    v_t = (vf_ref if is_fox(c) else vd_ref)[0, tile, chan, :]
                    tiles[key] = jnp.concatenate([v_t, jnp.ones((SUM_ROWS, TK), BF16)], axis=0)
                return tiles[key]
            key = (kind, id(tile), is_fox(c), cols(c).start)
            if key not in tiles:
                at = pl.ds(pl.multiple_of(tile * TK, TK), TK)
                if is_fox(c):
                    tiles[key] = jnp.concatenate([kf_ref[0, at, cols(c)], dec_ref[0, at, :]],
                                                 axis=1)
                else:
                    tiles[key] = kd_ref[0, at, cols(c)]
            return tiles[key]

        def scores(tile, qb, c, add, first):
            if is_fox(c):
                q_t = jnp.concatenate([qf_ref[c % 2, 0, rows(qb), cols(c)],
                                       pick_ref[c - DIFF_CHAINS]], axis=1)
            else:
                q_t = qd_ref[c % 2, 0, rows(qb), cols(c)]
            s = lax.dot_general(load("k", tile, c), q_t, NT_DIMS, preferred_element_type=F32)
            return s if add is None else s + add()

        pending = {j: scores(*seq[j]) for j in range(min(QK_AHEAD, len(seq)))}
        for j, (tile, qb, c, _, first) in enumerate(seq):
            s = pending.pop(j)
            state = qb * CHAINS + c
            m_new = jnp.max(s, axis=0, keepdims=True)
            if not first:
                m_old = m_ref[state]
                m_new = jnp.maximum(m_old, m_new)
                alpha = jnp.exp2(m_old - m_new)
            p = jnp.exp2(s - m_new).astype(BF16)
            pv = jnp.dot(load("v", tile, c), p, preferred_element_type=F32)
            if j + QK_AHEAD < len(seq):
                pending[j + QK_AHEAD] = scores(*seq[j + QK_AHEAD])
            acc = accumulator(qb, c)
            m_ref[state] = m_new
            acc[...] = pv if first else alpha * acc[...] + pv

    def bias(kind, c):
        if kind == "diag":
            return (lambda: causal_ref[...]) if is_fox(c) else (lambda: bnear_ref[0, c // 2])
        if kind == "prev" and not is_fox(c):
            return lambda: bnear_ref[1, c // 2]
        return None

    def steps(tile, kinds):
        order = [c0 + b for c0 in range(0, DIFF_CHAINS, 2) for b in (0, DIFF_CHAINS)]
        return [(tile, qb, c0 + e, bias(kind, c0 + e), kind == "diag")
                for c0 in order for qb, kind in kinds.items() for e in range(2)]

    base = Q_BLOCKS * g
    head = []
    for u in reversed(range(Q_BLOCKS)):
        head += steps(base + u, {j: "diag" if j == u else "prev" if j == u + 1 else "far"
                                 for j in range(u, Q_BLOCKS)})
    run(head)

    def body(r, carry):
        hi = base - 1 - 2 * r
        lo = hi - 1
        far = {j: "far" for j in range(Q_BLOCKS)}
        pl.when(r == 0)(lambda: run(steps(hi, {**far, 0: "prev"}) + steps(lo, far)))
        pl.when(r > 0)(lambda: run(steps(hi, far) + steps(lo, far)))
        return carry

    lax.fori_loop(0, base // 2, body, 0)

    for qb in range(Q_BLOCKS):
        for h in range(DIFF_HEADS):
            o = normalised(qb, 2 * h) - lam * normalised(qb, 2 * h + 1)
            o = o * lax.rsqrt(jnp.mean(o * o, axis=0, keepdims=True) + EPS) * gsub_ref[...]
            od_ref[0, h * LANES:(h + 1) * LANES, rows(qb)] = o.astype(BF16)
        for h in range(FOX_HEADS):
            o = normalised(qb, DIFF_CHAINS + h)
            of_ref[0, h * HEAD_DIM:(h + 1) * HEAD_DIM, rows(qb)] = o.astype(BF16)


def _attention(lam_vecs, g_subln, bias_near, qd, kd, vd, qf, kf, vf, decay, lam_init):
    batch, seq, _ = kd.shape
    const2 = lambda b, i: (0, 0)
    const4 = lambda b, i: (0, 0, 0, 0)
    kk = np.arange(TK)[:, None]
    qq = np.arange(TQ)[None, :]
    causal = jnp.asarray(np.where(kk <= qq, 0.0, -np.inf).astype(np.float32))
    lane = np.arange(LANES)
    pick = (lane[None] < DECAY_PARTS * FOX_HEADS) & (lane[None] % FOX_HEADS
                                                     == np.arange(FOX_HEADS)[:, None])
    pick = jnp.asarray(np.broadcast_to(pick[:, None, :], (FOX_HEADS, TQ, LANES)), BF16)
    g_rows = jnp.broadcast_to((g_subln * (1.0 - lam_init)).reshape(LANES, 1), (LANES, TQ))
    oblk = pl.BlockSpec((1, BRANCH_WIDTH, Q_BLOCKS * TQ), lambda b, i: (b, 0, i))
    qsel = pl.BlockSpec((2, 1, Q_BLOCKS * TQ, BRANCH_WIDTH), lambda b, i: (0, b, i, 0))
    full = pl.BlockSpec((1, seq, BRANCH_WIDTH), lambda b, i: (b, 0, 0))
    full_v = pl.BlockSpec((1, seq // TK, BRANCH_WIDTH, TK), lambda b, i: (b, 0, 0, 0))
    out = jax.ShapeDtypeStruct((batch, BRANCH_WIDTH, seq), BF16)
    return pl.pallas_call(
        functools.partial(_attn_kernel, lam_init),
        grid=(batch, seq // (Q_BLOCKS * TQ)),
        in_specs=[
            pl.BlockSpec(lam_vecs.shape, const2),
            pl.BlockSpec(g_rows.shape, const2),
            pl.BlockSpec(bias_near.shape, const4),
            pl.BlockSpec(causal.shape, const2),
            pl.BlockSpec(pick.shape, lambda b, i: (0, 0, 0)),
            qsel, full, full_v, qsel, full, full_v,
            pl.BlockSpec((1, seq, LANES), lambda b, i: (b, 0, 0)),
        ],
        out_specs=[oblk, oblk],
        out_shape=[out, out],
        scratch_shapes=[
            pltpu.VMEM((Q_BLOCKS * CHAINS, 1, TQ), F32),
            pltpu.VMEM((Q_BLOCKS * DIFF_CHAINS, LANES + SUM_ROWS, TQ), F32),
            pltpu.VMEM((Q_BLOCKS * FOX_HEADS, HEAD_DIM + SUM_ROWS, TQ), F32),
        ],
        compiler_params=pltpu.CompilerParams(
            dimension_semantics=("parallel", "arbitrary"), vmem_limit_bytes=VMEM_LIMIT),
        name="attention",
    )(lam_vecs, g_rows, bias_near, causal, pick, qd, kd, vd, qf, kf, vf, decay)


def _merge_kernel(x_ref, od_ref, of_ref, g_ref, wg_ref, wpa_ref, wpb_ref, wo_ref, y_ref):
    x = x_ref[...]
    h = _rms(x, g_ref[...]).astype(BF16)
    a = lax.dot_general(od_ref[0], wpa_ref[...], TN_DIMS, preferred_element_type=F32)
    b = lax.dot_general(of_ref[0], wpb_ref[...], TN_DIMS, preferred_element_type=F32)
    ga = jax.nn.sigmoid(jnp.dot(h, wg_ref[:, :D_MODEL], preferred_element_type=F32))
    merged = ga * a
    gb = jax.nn.sigmoid(jnp.dot(h, wg_ref[:, D_MODEL:], preferred_element_type=F32))
    merged = (merged + gb * b).astype(BF16)
    y_ref[...] = x + jnp.dot(merged, wo_ref[...], preferred_element_type=F32)


def _merge(x2, od, of, g_mix, w_gate, w_pa, w_pb, w_o):
    n = x2.shape[0]
    tm = TM_PROJ
    const = lambda i: (0, 0)
    row = lambda i: (i, 0)
    tiles_per_seq = od.shape[2] // tm
    col = pl.BlockSpec((1, BRANCH_WIDTH, tm), lambda i: (i // tiles_per_seq, 0, i % tiles_per_seq))
    return pl.pallas_call(
        _merge_kernel,
        grid=(n // tm,),
        in_specs=[
            pl.BlockSpec((tm, D_MODEL), row),
            col,
            col,
            pl.BlockSpec((1, D_MODEL), const),
            pl.BlockSpec(w_gate.shape, const),
            pl.BlockSpec(w_pa.shape, const),
            pl.BlockSpec(w_pb.shape, const),
            pl.BlockSpec(w_o.shape, const),
        ],
        out_specs=pl.BlockSpec((tm, D_MODEL), row),
        out_shape=jax.ShapeDtypeStruct((n, D_MODEL), F32),
        compiler_params=pltpu.CompilerParams(
            dimension_semantics=("parallel",), vmem_limit_bytes=VMEM_LIMIT),
        name="merge",
    )(x2, od, of, g_mix, w_gate, w_pa, w_pb, w_o)


FF_CHUNK = 1024


def _mlp_kernel(final_norm, x_ref, g_ref, w1_ref, w2_ref, gf_ref, y_ref):
    x = x_ref[...]
    h = _rms(x, g_ref[...]).astype(BF16)
    y = x
    for c in range(D_FF // FF_CHUNK):
        cols = slice(c * FF_CHUNK, (c + 1) * FF_CHUNK)
        u = jnp.maximum(jnp.dot(h, w1_ref[:, cols], preferred_element_type=F32), 0.0)
        y = y + jnp.dot((u * u).astype(BF16), w2_ref[cols, :], preferred_element_type=F32)
    y_ref[...] = _rms(y, gf_ref[...]) if final_norm else y


def _mlp(x2, g_mlp, w_1, w_2, g_final, final_norm):
    n = x2.shape[0]
    tm = TM_PROJ
    const = lambda i: (0, 0)
    row = lambda i: (i, 0)
    single = pl.Buffered(1)
    return pl.pallas_call(
        functools.partial(_mlp_kernel, final_norm),
        grid=(n // tm,),
        in_specs=[
            pl.BlockSpec((tm, D_MODEL), row),
            pl.BlockSpec((1, D_MODEL), const),
            pl.BlockSpec(w_1.shape, const, pipeline_mode=single),
            pl.BlockSpec(w_2.shape, const, pipeline_mode=single),
            pl.BlockSpec((1, D_MODEL), const),
        ],
        out_specs=pl.BlockSpec((tm, D_MODEL), row),
        out_shape=jax.ShapeDtypeStruct((n, D_MODEL), F32),
        compiler_params=pltpu.CompilerParams(
            dimension_semantics=("parallel",), vmem_limit_bytes=VMEM_LIMIT),
        name="mlp",
    )(x2, g_mlp, w_1, w_2, g_final)


def _layer(x, layer_idx, g_mix, w_in, b_f, lam_q1, lam_k1, lam_q2, lam_k2, g_subln,
           w_pa, w_pb, w_o, g_mlp, w_1, w_2, bias_near, g_final, final_norm):
    batch, seq, d = x.shape
    n = batch * seq
    x2 = x.reshape(n, d)
    qkv_cols = 6 * BRANCH_WIDTH
    pad = LANES - DECAY_PARTS * FOX_HEADS
    w_tail = lax.optimization_barrier(w_in[:, qkv_cols:])
    w_fl = jnp.pad(jnp.tile(w_tail[:, :FOX_HEADS], (1, DECAY_PARTS)),
                   ((0, 0), (0, pad))).astype(BF16)
    b_fl = jnp.pad(jnp.tile(b_f, DECAY_PARTS), (0, pad)).reshape(1, LANES)
    w_gate = w_tail[:, FOX_HEADS:].astype(BF16)
    g_mix2 = g_mix.reshape(1, d)

    qd, kd, vd, qf, kf, vf, decay = _in_proj(x2, g_mix2, w_in, w_fl, b_fl, seq)

    lam_vecs = jnp.stack([lam_q1, lam_k1, lam_q2, lam_k2]).astype(F32)
    shape3 = (batch, seq, BRANCH_WIDTH)
    shape4 = (2,) + shape3
    od, of = _attention(lam_vecs, g_subln, bias_near,
                        qd.reshape(shape4), kd.reshape(shape3), vd,
                        qf.reshape(shape4), kf.reshape(shape3), vf,
                        decay.reshape(batch, seq, LANES), _lambda_init(layer_idx))

    x1 = _merge(x2, od, of, g_mix2,
                w_gate, w_pa.astype(BF16), w_pb.astype(BF16), w_o.astype(BF16))
    y = _mlp(x1, g_mlp.reshape(1, d), w_1.astype(BF16), w_2.astype(BF16), g_final, final_norm)
    return y.reshape(batch, seq, d)


def kernel(x, g_mix, w_in, b_f, lam_q1, lam_k1, lam_q2, lam_k2, g_subln, w_pa, w_pb, w_o,
           g_mlp, w_1, w_2, rel_table, g_final):
    depth = g_mix.shape[0]
    batch, seq, d = x.shape
    assert d == D_MODEL and w_in.shape[1:] == (D_MODEL, 6 * BRANCH_WIDTH + FOX_HEADS + 2 * D_MODEL)
    assert w_1.shape[1:] == (D_MODEL, D_FF) and rel_table.shape == (REL_BUCKETS, DIFF_HEADS)
    assert seq % (Q_BLOCKS * TQ) == 0 and seq % TM_IN_PROJ == 0 and seq % TM_PROJ == 0
    bias_near = _bias_tiles(rel_table, seq)
    for l in range(depth):
        x = _layer(x, l, g_mix[l], w_in[l], b_f[l], lam_q1[l], lam_k1[l], lam_q2[l], lam_k2[l],
                   g_subln[l], w_pa[l], w_pb[l], w_o[l], g_mlp[l], w_1[l], w_2[l],
                   bias_near, g_final.reshape(1, -1), l == depth - 1)
    return x
```

```python
import functools
import math

import numpy as np
import jax
import jax.numpy as jnp
from jax import lax
from jax.experimental import pallas as pl
from jax.experimental.pallas import tpu as pltpu

D_MODEL = 1024
CHUNK = 64
HEAD_DIM = 64
DIFF_HEADS = 4
FOX_HEADS = 8
BRANCH_WIDTH = 512
D_FF = 4 * D_MODEL
REL_BUCKETS = 32
REL_MAX_DIST = 128
EPS = 1e-6
LANES = 128
MASKED_BUCKET = REL_BUCKETS

TQ = 256
TK = 256
TM_IN_PROJ = 512
TM_PROJ = 1024
V7X_VMEM_BYTES = 64 * 1024 * 1024
VMEM_LIMIT = V7X_VMEM_BYTES * 7 // 8

LOG2E = math.log2(math.e)
Q_SCALE = HEAD_DIM ** -0.5 * LOG2E

F32 = jnp.float32
BF16 = jnp.bfloat16
NT_DIMS = (((1,), (1,)), ((), ()))
TN_DIMS = (((0,), (0,)), ((), ()))


def _lambda_init(layer_idx):
    return 0.8 - 0.6 * math.exp(-0.3 * layer_idx)


def _rms(xf, g):
    return xf * lax.rsqrt(jnp.mean(xf * xf, axis=-1, keepdims=True) + EPS) * g


DECAY_PARTS = 3


def _in_proj_kernel(tiles_per_seq, x0_ref, xnext_ref, g_ref, w32_ref, wfl_ref, bf_ref,
                    qd_ref, kd_ref, vd_ref, qf_ref, kf_ref, vf_ref, dec_ref,
                    carry_ref, w_ref, h_ref):
    step = pl.program_id(0)

    @pl.when(step == 0)
    def _():
        w_ref[...] = w32_ref[...].astype(BF16)
        h_ref[0] = _rms(x0_ref[...], g_ref[...]).astype(BF16)

    h = h_ref[step % 2]

    @pl.when(step % tiles_per_seq == 0)
    def _():
        carry_ref[...] = jnp.zeros_like(carry_ref)

    z = jnp.dot(h, wfl_ref[...], preferred_element_type=F32) + bf_ref[...]
    acc = jnp.minimum(z, 0.0) - jnp.log1p(jnp.exp(-jnp.abs(z)))
    rows = acc.shape[0]
    row = lax.broadcasted_iota(jnp.int32, acc.shape, 0)
    d = 1
    while d < rows:
        acc = acc + jnp.where(row >= d, pltpu.roll(acc, d, axis=0), 0.0)
        d *= 2
    acc = acc + carry_ref[...]
    carry_ref[...] = acc[rows - 1:rows, :]
    neg = acc * -LOG2E
    hi = neg.astype(BF16).astype(F32)
    mid = (neg - hi).astype(BF16).astype(F32)
    lo = neg - hi - mid
    lane = lax.broadcasted_iota(jnp.int32, acc.shape, 1)
    piece = jnp.where(lane < FOX_HEADS, hi, jnp.where(lane < 2 * FOX_HEADS, mid, lo))
    dec_ref[...] = jnp.where(lane < DECAY_PARTS * FOX_HEADS, piece, 0.0).astype(BF16)

    outs = (qd_ref, kd_ref, vd_ref, qf_ref, kf_ref, vf_ref)
    for c in (2, 5, 0, 3, 1, 4):
        o_ref = outs[c]
        w = w_ref[:, c * BRANCH_WIDTH:(c + 1) * BRANCH_WIDTH]
        o = jnp.dot(h, w, preferred_element_type=F32)
        if o_ref is qd_ref or o_ref is qf_ref:
            o = (o * Q_SCALE).astype(BF16)
            low = lax.broadcasted_iota(jnp.int32, o.shape, 1) % LANES < HEAD_DIM
            o_ref[0] = jnp.where(low, o, jnp.zeros_like(o))
            o_ref[1] = jnp.where(low, jnp.zeros_like(o), o)
        elif o_ref is vd_ref or o_ref is vf_ref:
            o_t = o.T.astype(BF16)
            for t in range(o_ref.shape[1]):
                o_ref[0, t] = o_t[:, t * TK:(t + 1) * TK]
        else:
            o_ref[...] = o.astype(BF16)

    h_ref[(step + 1) % 2] = _rms(xnext_ref[...], g_ref[...]).astype(BF16)


def _in_proj(x2, g_mix, w_in, w_fl, b_f, seq):
    n = x2.shape[0]
    tm = TM_IN_PROJ
    qkv_cols = 6 * BRANCH_WIDTH
    const = lambda i: (0, 0)
    row = lambda i: (i, 0)
    out_bf = jax.ShapeDtypeStruct((n, BRANCH_WIDTH), BF16)
    out_q = jax.ShapeDtypeStruct((2, n, BRANCH_WIDTH), BF16)
    spec_bf = pl.BlockSpec((tm, BRANCH_WIDTH), row)
    spec_q = pl.BlockSpec((2, tm, BRANCH_WIDTH), lambda i: (0, i, 0))
    tiles_per_seq = seq // tm
    out_v = jax.ShapeDtypeStruct((n // seq, seq // TK, BRANCH_WIDTH, TK), BF16)
    spec_v = pl.BlockSpec((1, tm // TK, BRANCH_WIDTH, TK),
                          lambda i: (i // tiles_per_seq, i % tiles_per_seq, 0, 0))
    return pl.pallas_call(
        functools.partial(_in_proj_kernel, seq // tm),
        grid=(n // tm,),
        in_specs=[
            pl.BlockSpec((tm, D_MODEL), const),
            pl.BlockSpec((tm, D_MODEL), lambda i: (jnp.minimum(i + 1, n // tm - 1), 0)),
            pl.BlockSpec((1, D_MODEL), const),
            pl.BlockSpec((D_MODEL, qkv_cols), const, pipeline_mode=pl.Buffered(1)),
            pl.BlockSpec(w_fl.shape, const),
            pl.BlockSpec((1, LANES), const),
        ],
        out_specs=[spec_q, spec_bf, spec_v, spec_q, spec_bf, spec_v,
                   pl.BlockSpec((tm, LANES), row)],
        out_shape=[out_q, out_bf, out_v, out_q, out_bf, out_v,
                   jax.ShapeDtypeStruct((n, LANES), BF16)],
        scratch_shapes=[pltpu.VMEM((1, LANES), F32),
                        pltpu.VMEM((D_MODEL, qkv_cols), BF16),
                        pltpu.VMEM((2, tm, D_MODEL), BF16)],
        compiler_params=pltpu.CompilerParams(
            dimension_semantics=("arbitrary",), vmem_limit_bytes=VMEM_LIMIT),
        name="in_proj",
    )(x2, x2, g_mix, w_in, w_fl, b_f)


def _rel_bucket_np(rel):
    nb = REL_BUCKETS // 2
    ret = np.where(rel > 0, nb, 0)
    n = np.abs(rel)
    max_exact = nb // 2
    nf = np.maximum(n, 1).astype(np.float64)
    large = max_exact + (np.log(nf / max_exact) / math.log(REL_MAX_DIST / max_exact)
                         * (nb - max_exact)).astype(np.int32)
    large = np.minimum(large, nb - 1)
    return (ret + np.where(n < max_exact, n, large)).astype(np.int32)


def _bias_index_maps(seq):
    kk = np.arange(TK, dtype=np.int64)[:, None]
    qq = np.arange(TQ, dtype=np.int64)[None, :]
    diag = _rel_bucket_np(kk - qq)
    diag = np.where(kk // CHUNK <= qq // CHUNK, diag, MASKED_BUCKET).astype(np.int32)
    prev = _rel_bucket_np(kk - TK - qq)
    far = _rel_bucket_np(np.arange(-seq, -TK, dtype=np.int64))
    far_bucket = int(far[0])
    assert (far == far_bucket).all(), "keys two tiles back must share one bucket"
    return diag, prev, far_bucket


def _bias_kernel(far_bucket, buckets, tab_ref, idx_ref, out_ref):
    for h in range(DIFF_HEADS):
        far = tab_ref[far_bucket, h]
        for t in range(idx_ref.shape[0]):
            idx = idx_ref[t]
            acc = jnp.full(idx.shape, -jnp.inf, F32)
            for b in buckets[t]:
                acc = jnp.where(idx == b, (tab_ref[b, h] - far) * LOG2E, acc)
            out_ref[t, h] = acc


def _bias_tiles(rel_table, seq):
    diag, prev, far_bucket = _bias_index_maps(seq)
    idx = np.stack([diag, prev])
    buckets = tuple(tuple(int(b) for b in np.unique(m) if b != MASKED_BUCKET) for m in idx)
    vmem = pl.BlockSpec(memory_space=pltpu.VMEM)
    return pl.pallas_call(
        functools.partial(_bias_kernel, far_bucket, buckets),
        in_specs=[pl.BlockSpec(memory_space=pltpu.SMEM), vmem],
        out_specs=vmem,
        out_shape=jax.ShapeDtypeStruct((idx.shape[0], DIFF_HEADS, TK, TQ), F32),
        name="bias_tiles",
    )(rel_table, jnp.asarray(idx))


DIFF_CHAINS = 2 * DIFF_HEADS
CHAINS = DIFF_CHAINS + FOX_HEADS
QK_AHEAD = 7
Q_BLOCKS = 4
SUM_ROWS = 16


def _attn_kernel(lam_init, lam_ref, gsub_ref, bnear_ref, causal_ref, pick_ref,
                 qd_ref, kd_ref, vd_ref, qf_ref, kf_ref, vf_ref, dec_ref,
                 od_ref, of_ref, m_ref, accd_ref, accf_ref):
    g = pl.program_id(1)
    lam_v = lam_ref[...]
    lam = (jnp.exp(jnp.sum(lam_v[0:1] * lam_v[1:2], axis=-1, keepdims=True))
           - jnp.exp(jnp.sum(lam_v[2:3] * lam_v[3:4], axis=-1, keepdims=True))
           + lam_init)

    def is_fox(c):
        return c >= DIFF_CHAINS

    def cols(c):
        blk = (c % DIFF_CHAINS) // 2
        return slice(blk * LANES, (blk + 1) * LANES)

    def rows(qb):
        return slice(qb * TQ, (qb + 1) * TQ)

    def accumulator(qb, c):
        if is_fox(c):
            return accf_ref.at[qb * FOX_HEADS + c - DIFF_CHAINS]
        return accd_ref.at[qb * DIFF_CHAINS + c]

    def normalised(qb, c):
        acc = accumulator(qb, c)[...]
        chans = acc.shape[0] - SUM_ROWS
        return acc[:chans] * (1.0 / acc[chans:chans + 1])

    def run(seq):
        tiles = {}

        def load(kind, tile, c):
            if kind == "v":
                chan = (slice((c - DIFF_CHAINS) * HEAD_DIM, (c - DIFF_CHAINS + 1) * HEAD_DIM)
                        if is_fox(c) else cols(c))
                key = (kind, id(tile), is_fox(c), chan.start)
                if key not in tiles:
                    v_t = (vf_ref if is_fox(c) else vd_ref)[0, tile, chan, :]
                    tiles[key] = jnp.concatenate([v_t, jnp.ones((SUM_ROWS, TK), BF16)], axis=0)
                return tiles[key]
            key = (kind, id(tile), is_fox(c), cols(c).start)
            if key not in tiles:
                at = pl.ds(pl.multiple_of(tile * TK, TK), TK)
                if is_fox(c):
                    tiles[key] = jnp.concatenate([kf_ref[0, at, cols(c)], dec_ref[0, at, :]],
                                                 axis=1)
                else:
                    tiles[key] = kd_ref[0, at, cols(c)]
            return tiles[key]

        def scores(tile, qb, c, add, first):
            if is_fox(c):
                q_t = jnp.concatenate([qf_ref[c % 2, 0, rows(qb), cols(c)],
                                       pick_ref[c - DIFF_CHAINS]], axis=1)
            else:
                q_t = qd_ref[c % 2, 0, rows(qb), cols(c)]
            s = lax.dot_general(load("k", tile, c), q_t, NT_DIMS, preferred_element_type=F32)
            return s if add is None else s + add()

        pending = {j: scores(*seq[j]) for j in range(min(QK_AHEAD, len(seq)))}
        for j, (tile, qb, c, _, first) in enumerate(seq):
            s = pending.pop(j)
            state = qb * CHAINS + c
            m_new = jnp.max(s, axis=0, keepdims=True)
            if not first:
                m_old = m_ref[state]
                m_new = jnp.maximum(m_old, m_new)
                alpha = jnp.exp2(m_old - m_new)
            p = jnp.exp2(s - m_new).astype(BF16)
            pv = jnp.dot(load("v", tile, c), p, preferred_element_type=F32)
            if j + QK_AHEAD < len(seq):
                pending[j + QK_AHEAD] = scores(*seq[j + QK_AHEAD])
            acc = accumulator(qb, c)
            m_ref[state] = m_new
            acc[...] = pv if first else alpha * acc[...] + pv

    def bias(kind, c):
        if kind == "diag":
            return (lambda: causal_ref[...]) if is_fox(c) else (lambda: bnear_ref[0, c // 2])
        if kind == "prev" and not is_fox(c):
            return lambda: bnear_ref[1, c // 2]
        return None

    def steps(tile, kinds):
        order = [c0 + b for c0 in range(0, DIFF_CHAINS, 2) for b in (0, DIFF_CHAINS)]
        return [(tile, qb, c0 + e, bias(kind, c0 + e), kind == "diag")
                for c0 in order for qb, kind in kinds.items() for e in range(2)]

    base = Q_BLOCKS * g
    head = []
    for u in reversed(range(Q_BLOCKS)):
        head += steps(base + u, {j: "diag" if j == u else "prev" if j == u + 1 else "far"
                                 for j in range(u, Q_BLOCKS)})
    run(head)

    def body(r, carry):
        hi = base - 1 - 2 * r
        lo = hi - 1
        far = {j: "far" for j in range(Q_BLOCKS)}
        pl.when(r == 0)(lambda: run(steps(hi, {**far, 0: "prev"}) + steps(lo, far)))
        pl.when(r > 0)(lambda: run(steps(hi, far) + steps(lo, far)))
        return carry

    lax.fori_loop(0, base // 2, body, 0)

    for qb in range(Q_BLOCKS):
        for h in range(DIFF_HEADS):
            o = normalised(qb, 2 * h) - lam * normalised(qb, 2 * h + 1)
            o = o * lax.rsqrt(jnp.mean(o * o, axis=0, keepdims=True) + EPS) * gsub_ref[...]
            od_ref[0, h * LANES:(h + 1) * LANES, rows(qb)] = o.astype(BF16)
        for h in range(FOX_HEADS):
            o = normalised(qb, DIFF_CHAINS + h)
            of_ref[0, h * HEAD_DIM:(h + 1) * HEAD_DIM, rows(qb)] = o.astype(BF16)


def _attention(lam_vecs, g_subln, bias_near, qd, kd, vd, qf, kf, vf, decay, lam_init):
    batch, seq, _ = kd.shape
    const2 = lambda b, i: (0, 0)
    const4 = lambda b, i: (0, 0, 0, 0)
    kk = np.arange(TK)[:, None]
    qq = np.arange(TQ)[None, :]
    causal = jnp.asarray(np.where(kk <= qq, 0.0, -np.inf).astype(np.float32))
    lane = np.arange(LANES)
    pick = (lane[None] < DECAY_PARTS * FOX_HEADS) & (lane[None] % FOX_HEADS
                                                     == np.arange(FOX_HEADS)[:, None])
    pick = jnp.asarray(np.broadcast_to(pick[:, None, :], (FOX_HEADS, TQ, LANES)), BF16)
    g_rows = jnp.broadcast_to((g_subln * (1.0 - lam_init)).reshape(LANES, 1), (LANES, TQ))
    oblk = pl.BlockSpec((1, BRANCH_WIDTH, Q_BLOCKS * TQ), lambda b, i: (b, 0, i))
    qsel = pl.BlockSpec((2, 1, Q_BLOCKS * TQ, BRANCH_WIDTH), lambda b, i: (0, b, i, 0))
    full = pl.BlockSpec((1, seq, BRANCH_WIDTH), lambda b, i: (b, 0, 0))
    full_v = pl.BlockSpec((1, seq // TK, BRANCH_WIDTH, TK), lambda b, i: (b, 0, 0, 0))
    out = jax.ShapeDtypeStruct((batch, BRANCH_WIDTH, seq), BF16)
    return pl.pallas_call(
        functools.partial(_attn_kernel, lam_init),
        grid=(batch, seq // (Q_BLOCKS * TQ)),
        in_specs=[
            pl.BlockSpec(lam_vecs.shape, const2),
            pl.BlockSpec(g_rows.shape, const2),
            pl.BlockSpec(bias_near.shape, const4),
            pl.BlockSpec(causal.shape, const2),
            pl.BlockSpec(pick.shape, lambda b, i: (0, 0, 0)),
            qsel, full, full_v, qsel, full, full_v,
            pl.BlockSpec((1, seq, LANES), lambda b, i: (b, 0, 0)),
        ],
        out_specs=[oblk, oblk],
        out_shape=[out, out],
        scratch_shapes=[
            pltpu.VMEM((Q_BLOCKS * CHAINS, 1, TQ), F32),
            pltpu.VMEM((Q_BLOCKS * DIFF_CHAINS, LANES + SUM_ROWS, TQ), F32),
            pltpu.VMEM((Q_BLOCKS * FOX_HEADS, HEAD_DIM + SUM_ROWS, TQ), F32),
        ],
        compiler_params=pltpu.CompilerParams(
            dimension_semantics=("parallel", "arbitrary"), vmem_limit_bytes=VMEM_LIMIT),
        name="attention",
    )(lam_vecs, g_rows, bias_near, causal, pick, qd, kd, vd, qf, kf, vf, decay)


def _merge_kernel(x_ref, od_ref, of_ref, g_ref, wg_ref, wpa_ref, wpb_ref, wo_ref, y_ref):
    x = x_ref[...]
    h = _rms(x, g_ref[...]).astype(BF16)
    a = lax.dot_general(od_ref[0], wpa_ref[...], TN_DIMS, preferred_element_type=F32)
    b = lax.dot_general(of_ref[0], wpb_ref[...], TN_DIMS, preferred_element_type=F32)
    ga = jax.nn.sigmoid(jnp.dot(h, wg_ref[:, :D_MODEL], preferred_element_type=F32))
    merged = ga * a
    gb = jax.nn.sigmoid(jnp.dot(h, wg_ref[:, D_MODEL:], preferred_element_type=F32))
    merged = (merged + gb * b).astype(BF16)
    y_ref[...] = x + jnp.dot(merged, wo_ref[...], preferred_element_type=F32)


def _merge(x2, od, of, g_mix, w_gate, w_pa, w_pb, w_o):
    n = x2.shape[0]
    tm = TM_PROJ
    const = lambda i: (0, 0)
    row = lambda i: (i, 0)
    tiles_per_seq = od.shape[2] // tm
    col = pl.BlockSpec((1, BRANCH_WIDTH, tm), lambda i: (i // tiles_per_seq, 0, i % tiles_per_seq))
    return pl.pallas_call(
        _merge_kernel,
        grid=(n // tm,),
        in_specs=[
            pl.BlockSpec((tm, D_MODEL), row),
            col,
            col,
            pl.BlockSpec((1, D_MODEL), const),
            pl.BlockSpec(w_gate.shape, const),
            pl.BlockSpec(w_pa.shape, const),
            pl.BlockSpec(w_pb.shape, const),
            pl.BlockSpec(w_o.shape, const),
        ],
        out_specs=pl.BlockSpec((tm, D_MODEL), row),
        out_shape=jax.ShapeDtypeStruct((n, D_MODEL), F32),
        compiler_params=pltpu.CompilerParams(
            dimension_semantics=("parallel",), vmem_limit_bytes=VMEM_LIMIT),
        name="merge",
    )(x2, od, of, g_mix, w_gate, w_pa, w_pb, w_o)


FF_CHUNK = 1024


def _mlp_kernel(final_norm, x_ref, g_ref, w1_ref, w2_ref, gf_ref, y_ref):
    x = x_ref[...]
    h = _rms(x, g_ref[...]).astype(BF16)
    y = x
    for c in range(D_FF // FF_CHUNK):
        cols = slice(c * FF_CHUNK, (c + 1) * FF_CHUNK)
        u = jnp.maximum(jnp.dot(h, w1_ref[:, cols], preferred_element_type=F32), 0.0)
        y = y + jnp.dot((u * u).astype(BF16), w2_ref[cols, :], preferred_element_type=F32)
    y_ref[...] = _rms(y, gf_ref[...]) if final_norm else y


def _mlp(x2, g_mlp, w_1, w_2, g_final, final_norm):
    n = x2.shape[0]
    tm = TM_PROJ
    const = lambda i: (0, 0)
    row = lambda i: (i, 0)
    single = pl.Buffered(1)
    return pl.pallas_call(
        functools.partial(_mlp_kernel, final_norm),
        grid=(n // tm,),
        in_specs=[
            pl.BlockSpec((tm, D_MODEL), row),
            pl.BlockSpec((1, D_MODEL), const),
            pl.BlockSpec(w_1.shape, const, pipeline_mode=single),
            pl.BlockSpec(w_2.shape, const, pipeline_mode=single),
            pl.BlockSpec((1, D_MODEL), const),
        ],
        out_specs=pl.BlockSpec((tm, D_MODEL), row),
        out_shape=jax.ShapeDtypeStruct((n, D_MODEL), F32),
        compiler_params=pltpu.CompilerParams(
            dimension_semantics=("parallel",), vmem_limit_bytes=VMEM_LIMIT),
        name="mlp",
    )(x2, g_mlp, w_1, w_2, g_final)


def _layer(x, layer_idx, g_mix, w_in, b_f, lam_q1, lam_k1, lam_q2, lam_k2, g_subln,
           w_pa, w_pb, w_o, g_mlp, w_1, w_2, bias_near, g_final, final_norm):
    batch, seq, d = x.shape
    n = batch * seq
    x2 = x.reshape(n, d)
    qkv_cols = 6 * BRANCH_WIDTH
    pad = LANES - DECAY_PARTS * FOX_HEADS
    w_tail = lax.optimization_barrier(w_in[:, qkv_cols:])
    w_fl = jnp.pad(jnp.tile(w_tail[:, :FOX_HEADS], (1, DECAY_PARTS)),
                   ((0, 0), (0, pad))).astype(BF16)
    b_fl = jnp.pad(jnp.tile(b_f, DECAY_PARTS), (0, pad)).reshape(1, LANES)
    w_gate = w_tail[:, FOX_HEADS:].astype(BF16)
    g_mix2 = g_mix.reshape(1, d)

    qd, kd, vd, qf, kf, vf, decay = _in_proj(x2, g_mix2, w_in, w_fl, b_fl, seq)

    lam_vecs = jnp.stack([lam_q1, lam_k1, lam_q2, lam_k2]).astype(F32)
    shape3 = (batch, seq, BRANCH_WIDTH)
    shape4 = (2,) + shape3
    od, of = _attention(lam_vecs, g_subln, bias_near,
                        qd.reshape(shape4), kd.reshape(shape3), vd,
                        qf.reshape(shape4), kf.reshape(shape3), vf,
                        decay.reshape(batch, seq, LANES), _lambda_init(layer_idx))

    x1 = _merge(x2, od, of, g_mix2,
                w_gate, w_pa.astype(BF16), w_pb.astype(BF16), w_o.astype(BF16))
    y = _mlp(x1, g_mlp.reshape(1, d), w_1.astype(BF16), w_2.astype(BF16), g_final, final_norm)
    return y.reshape(batch, seq, d)


def kernel(x, g_mix, w_in, b_f, lam_q1, lam_k1, lam_q2, lam_k2, g_subln, w_pa, w_pb, w_o,
           g_mlp, w_1, w_2, rel_table, g_final):
    depth = g_mix.shape[0]
    batch, seq, d = x.shape
    assert d == D_MODEL and w_in.shape[1:] == (D_MODEL, 6 * BRANCH_WIDTH + FOX_HEADS + 2 * D_MODEL)
    assert w_1.shape[1:] == (D_MODEL, D_FF) and rel_table.shape == (REL_BUCKETS, DIFF_HEADS)
    assert seq % (Q_BLOCKS * TQ) == 0 and seq % TM_IN_PROJ == 0 and seq % TM_PROJ == 0
    bias_near = _bias_tiles(rel_table, seq)
    for l in range(depth):
        x = _layer(x, l, g_mix[l], w_in[l], b_f[l], lam_q1[l], lam_k1[l], lam_q2[l], lam_k2[l],
                   g_subln[l], w_pa[l], w_pb[l], w_o[l], g_mlp[l], w_1[l], w_2[l],
                   bias_near, g_final.reshape(1, -1), l == depth - 1)
    return x
```

```python
import functools
import math

import numpy as np
import jax
import jax.numpy as jnp
from jax import lax
from jax.experimental import pallas as pl
from jax.experimental.pallas import tpu as pltpu

D_MODEL = 1024
CHUNK = 64
HEAD_DIM = 64
DIFF_HEADS = 4
FOX_HEADS = 8
BRANCH_WIDTH = 512
D_FF = 4 * D_MODEL
REL_BUCKETS = 32
REL_MAX_DIST = 128
EPS = 1e-6
LANES = 128
MASKED_BUCKET = REL_BUCKETS

TQ = 256
TK = 256
TM_IN_PROJ = 512
TM_PROJ = 1024
V7X_VMEM_BYTES = 64 * 1024 * 1024
VMEM_LIMIT = V7X_VMEM_BYTES * 7 // 8

LOG2E = math.log2(math.e)
Q_SCALE = HEAD_DIM ** -0.5 * LOG2E

F32 = jnp.float32
BF16 = jnp.bfloat16
NT_DIMS = (((1,), (1,)), ((), ()))
TN_DIMS = (((0,), (0,)), ((), ()))


def _lambda_init(layer_idx):
    return 0.8 - 0.6 * math.exp(-0.3 * layer_idx)


def _rms(xf, g):
    return xf * lax.rsqrt(jnp.mean(xf * xf, axis=-1, keepdims=True) + EPS) * g


DECAY_PARTS = 3


def _in_proj_kernel(tiles_per_seq, x0_ref, xnext_ref, g_ref, w32_ref, wfl_ref, bf_ref,
                    qd_ref, kd_ref, vd_ref, qf_ref, kf_ref, vf_ref, dec_ref,
                    carry_ref, w_ref, h_ref):
    step = pl.program_id(0)

    @pl.when(step == 0)
    def _():
        w_ref[...] = w32_ref[...].astype(BF16)
        h_ref[0] = _rms(x0_ref[...], g_ref[...]).astype(BF16)

    h = h_ref[step % 2]

    @pl.when(step % tiles_per_seq == 0)
    def _():
        carry_ref[...] = jnp.zeros_like(carry_ref)

    z = jnp.dot(h, wfl_ref[...], preferred_element_type=F32) + bf_ref[...]
    acc = jnp.minimum(z, 0.0) - jnp.log1p(jnp.exp(-jnp.abs(z)))
    rows = acc.shape[0]
    row = lax.broadcasted_iota(jnp.int32, acc.shape, 0)
    d = 1
    while d < rows:
        acc = acc + jnp.where(row >= d, pltpu.roll(acc, d, axis=0), 0.0)
        d *= 2
    acc = acc + carry_ref[...]
    carry_ref[...] = acc[rows - 1:rows, :]
    neg = acc * -LOG2E
    hi = neg.astype(BF16).astype(F32)
    mid = (neg - hi).astype(BF16).astype(F32)
    lo = neg - hi - mid
    lane = lax.broadcasted_iota(jnp.int32, acc.shape, 1)
    piece = jnp.where(lane < FOX_HEADS, hi, jnp.where(lane < 2 * FOX_HEADS, mid, lo))
    dec_ref[...] = jnp.where(lane < DECAY_PARTS * FOX_HEADS, piece, 0.0).astype(BF16)

    outs = (qd_ref, kd_ref, vd_ref, qf_ref, kf_ref, vf_ref)
    for c in (2, 5, 0, 3, 1, 4):
        o_ref = outs[c]
        w = w_ref[:, c * BRANCH_WIDTH:(c + 1) * BRANCH_WIDTH]
        o = jnp.dot(h, w, preferred_element_type=F32)
        if o_ref is qd_ref or o_ref is qf_ref:
            o = (o * Q_SCALE).astype(BF16)
            low = lax.broadcasted_iota(jnp.int32, o.shape, 1) % LANES < HEAD_DIM
            o_ref[0] = jnp.where(low, o, jnp.zeros_like(o))
            o_ref[1] = jnp.where(low, jnp.zeros_like(o), o)
        elif o_ref is vd_ref or o_ref is vf_ref:
            o_t = o.T.astype(BF16)
            for t in range(o_ref.shape[1]):
                o_ref[0, t] = o_t[:, t * TK:(t + 1) * TK]
        else:
            o_ref[...] = o.astype(BF16)

    h_ref[(step + 1) % 2] = _rms(xnext_ref[...], g_ref[...]).astype(BF16)


def _in_proj(x2, g_mix, w_in, w_fl, b_f, seq):
    n = x2.shape[0]
    tm = TM_IN_PROJ
    qkv_cols = 6 * BRANCH_WIDTH
    const = lambda i: (0, 0)
    row = lambda i: (i, 0)
    out_bf = jax.ShapeDtypeStruct((n, BRANCH_WIDTH), BF16)
    out_q = jax.ShapeDtypeStruct((2, n, BRANCH_WIDTH), BF16)
    spec_bf = pl.BlockSpec((tm, BRANCH_WIDTH), row)
    spec_q = pl.BlockSpec((2, tm, BRANCH_WIDTH), lambda i: (0, i, 0))
    tiles_per_seq = seq // tm
    out_v = jax.ShapeDtypeStruct((n // seq, seq // TK, BRANCH_WIDTH, TK), BF16)
    spec_v = pl.BlockSpec((1, tm // TK, BRANCH_WIDTH, TK),
                          lambda i: (i // tiles_per_seq, i % tiles_per_seq, 0, 0))
    return pl.pallas_call(
        functools.partial(_in_proj_kernel, seq // tm),
        grid=(n // tm,),
        in_specs=[
            pl.BlockSpec((tm, D_MODEL), const),
            pl.BlockSpec((tm, D_MODEL), lambda i: (jnp.minimum(i + 1, n // tm - 1), 0)),
            pl.BlockSpec((1, D_MODEL), const),
            pl.BlockSpec((D_MODEL, qkv_cols), const, pipeline_mode=pl.Buffered(1)),
            pl.BlockSpec(w_fl.shape, const),
            pl.BlockSpec((1, LANES), const),
        ],
        out_specs=[spec_q, spec_bf, spec_v, spec_q, spec_bf, spec_v,
                   pl.BlockSpec((tm, LANES), row)],
        out_shape=[out_q, out_bf, out_v, out_q, out_bf, out_v,
                   jax.ShapeDtypeStruct((n, LANES), BF16)],
        scratch_shapes=[pltpu.VMEM((1, LANES), F32),
                        pltpu.VMEM((D_MODEL, qkv_cols), BF16),
                        pltpu.VMEM((2, tm, D_MODEL), BF16)],
        compiler_params=pltpu.CompilerParams(
            dimension_semantics=("arbitrary",), vmem_limit_bytes=VMEM_LIMIT),
        name="in_proj",
    )(x2, x2, g_mix, w_in, w_fl, b_f)


def _rel_bucket_np(rel):
    nb = REL_BUCKETS // 2
    ret = np.where(rel > 0, nb, 0)
    n = np.abs(rel)
    max_exact = nb // 2
    nf = np.maximum(n, 1).astype(np.float64)
    large = max_exact + (np.log(nf / max_exact) / math.log(REL_MAX_DIST / max_exact)
                         * (nb - max_exact)).astype(np.int32)
    large = np.minimum(large, nb - 1)
    return (ret + np.where(n < max_exact, n, large)).astype(np.int32)


def _bias_index_maps(seq):
    kk = np.arange(TK, dtype=np.int64)[:, None]
    qq = np.arange(TQ, dtype=np.int64)[None, :]
    diag = _rel_bucket_np(kk - qq)
    diag = np.where(kk // CHUNK <= qq // CHUNK, diag, MASKED_BUCKET).astype(np.int32)
    prev = _rel_bucket_np(kk - TK - qq)
    far = _rel_bucket_np(np.arange(-seq, -TK, dtype=np.int64))
    far_bucket = int(far[0])
    assert (far == far_bucket).all(), "keys two tiles back must share one bucket"
    return diag, prev, far_bucket


def _bias_kernel(far_bucket, buckets, tab_ref, idx_ref, out_ref):
    for h in range(DIFF_HEADS):
        far = tab_ref[far_bucket, h]
        for t in range(idx_ref.shape[0]):
            idx = idx_ref[t]
            acc = jnp.full(idx.shape, -jnp.inf, F32)
            for b in buckets[t]:
                acc = jnp.where(idx == b, (tab_ref[b, h] - far) * LOG2E, acc)
            out_ref[t, h] = acc


def _bias_tiles(rel_table, seq):
    diag, prev, far_bucket = _bias_index_maps(seq)
    idx = np.stack([diag, prev])
    buckets = tuple(tuple(int(b) for b in np.unique(m) if b != MASKED_BUCKET) for m in idx)
    vmem = pl.BlockSpec(memory_space=pltpu.VMEM)
    return pl.pallas_call(
        functools.partial(_bias_kernel, far_bucket, buckets),
        in_specs=[pl.BlockSpec(memory_space=pltpu.SMEM), vmem],
        out_specs=vmem,
        out_shape=jax.ShapeDtypeStruct((idx.shape[0], DIFF_HEADS, TK, TQ), F32),
        name="bias_tiles",
    )(rel_table, jnp.asarray(idx))


DIFF_CHAINS = 2 * DIFF_HEADS
CHAINS = DIFF_CHAINS + FOX_HEADS
QK_AHEAD = 6
Q_BLOCKS = 4
SUM_ROWS = 16


def _attn_kernel(lam_init, lam_ref, gsub_ref, bnear_ref, causal_ref, pick_ref,
                 qd_ref, kd_ref, vd_ref, qf_ref, kf_ref, vf_ref, dec_ref,
                 od_ref, of_ref, m_ref, accd_ref, accf_ref):
    g = pl.program_id(1)
    lam_v = lam_ref[...]
    lam = (jnp.exp(jnp.sum(lam_v[0:1] * lam_v[1:2], axis=-1, keepdims=True))
           - jnp.exp(jnp.sum(lam_v[2:3] * lam_v[3:4], axis=-1, keepdims=True))
           + lam_init)

    def is_fox(c):
        return c >= DIFF_CHAINS

    def cols(c):
        blk = (c % DIFF_CHAINS) // 2
        return slice(blk * LANES, (blk + 1) * LANES)

    def rows(qb):
        return slice(qb * TQ, (qb + 1) * TQ)

    def accumulator(qb, c):
        if is_fox(c):
            return accf_ref.at[qb * FOX_HEADS + c - DIFF_CHAINS]
        return accd_ref.at[qb * DIFF_CHAINS + c]

    def normalised(qb, c):
        acc = accumulator(qb, c)[...]
        chans = acc.shape[0] - SUM_ROWS
        return acc[:chans] * (1.0 / acc[chans:chans + 1])

    def run(seq):
        tiles = {}

        def load(kind, tile, c):
            if kind == "v":
                chan = (slice((c - DIFF_CHAINS) * HEAD_DIM, (c - DIFF_CHAINS + 1) * HEAD_DIM)
                        if is_fox(c) else cols(c))
                key = (kind, id(tile), is_fox(c), chan.start)
                if key not in tiles:
                    v_t = (vf_ref if is_fox(c) else vd_ref)[0, tile, chan, :]
                    tiles[key] = jnp.concatenate([v_t, jnp.ones((SUM_ROWS, TK), BF16)], axis=0)
                return tiles[key]
            key = (kind, id(tile), is_fox(c), cols(c).start)
            if key not in tiles:
                at = pl.ds(pl.multiple_of(tile * TK, TK), TK)
                if is_fox(c):
                    tiles[key] = jnp.concatenate([kf_ref[0, at, cols(c)], dec_ref[0, at, :]],
                                                 axis=1)
                else:
                    tiles[key] = kd_ref[0, at, cols(c)]
            return tiles[key]

        def scores(tile, qb, c, add, first):
            if is_fox(c):
                q_t = jnp.concatenate([qf_ref[c % 2, 0, rows(qb), cols(c)],
                                       pick_ref[c - DIFF_CHAINS]], axis=1)
            else:
                q_t = qd_ref[c % 2, 0, rows(qb), cols(c)]
            s = lax.dot_general(load("k", tile, c), q_t, NT_DIMS, preferred_element_type=F32)
            return s if add is None else s + add()

        pending = {j: scores(*seq[j]) for j in range(min(QK_AHEAD, len(seq)))}
        for j, (tile, qb, c, _, first) in enumerate(seq):
            s = pending.pop(j)
            state = qb * CHAINS + c
            m_new = jnp.max(s, axis=0, keepdims=True)
            if not first:
                m_old = m_ref[state]
                m_new = jnp.maximum(m_old, m_new)
                alpha = jnp.exp2(m_old - m_new)
            p = jnp.exp2(s - m_new).astype(BF16)
            pv = jnp.dot(load("v", tile, c), p, preferred_element_type=F32)
            if j + QK_AHEAD < len(seq):
                pending[j + QK_AHEAD] = scores(*seq[j + QK_AHEAD])
            acc = accumulator(qb, c)
            m_ref[state] = m_new
            acc[...] = pv if first else alpha * acc[...] + pv

    def bias(kind, c):
        if kind == "diag":
            return (lambda: causal_ref[...]) if is_fox(c) else (lambda: bnear_ref[0, c // 2])
        if kind == "prev" and not is_fox(c):
            return lambda: bnear_ref[1, c // 2]
        return None

    def steps(tile, kinds):
        order = list(range(0, CHAINS, 2))
        return [(tile, qb, c0 + e, bias(kind, c0 + e), kind == "diag")
                for c0 in order for qb, kind in kinds.items() for e in range(2)]

    base = Q_BLOCKS * g
    head = []
    for u in reversed(range(Q_BLOCKS)):
        head += steps(base + u, {j: "diag" if j == u else "prev" if j == u + 1 else "far"
                                 for j in range(u, Q_BLOCKS)})
    run(head)

    def body(r, carry):
        hi = base - 1 - 2 * r
        lo = hi - 1
        far = {j: "far" for j in range(Q_BLOCKS)}
        pl.when(r == 0)(lambda: run(steps(hi, {**far, 0: "prev"}) + steps(lo, far)))
        pl.when(r > 0)(lambda: run(steps(hi, far) + steps(lo, far)))
        return carry

    lax.fori_loop(0, base // 2, body, 0)

    for qb in range(Q_BLOCKS):
        for h in range(DIFF_HEADS):
            o = normalised(qb, 2 * h) - lam * normalised(qb, 2 * h + 1)
            o = o * lax.rsqrt(jnp.mean(o * o, axis=0, keepdims=True) + EPS) * gsub_ref[...]
            od_ref[0, h * LANES:(h + 1) * LANES, rows(qb)] = o.astype(BF16)
        for h in range(FOX_HEADS):
            o = normalised(qb, DIFF_CHAINS + h)
            of_ref[0, h * HEAD_DIM:(h + 1) * HEAD_DIM, rows(qb)] = o.astype(BF16)


def _attention(lam_vecs, g_subln, bias_near, qd, kd, vd, qf, kf, vf, decay, lam_init):
    batch, seq, _ = kd.shape
    const2 = lambda b, i: (0, 0)
    const4 = lambda b, i: (0, 0, 0, 0)
    kk = np.arange(TK)[:, None]
    qq = np.arange(TQ)[None, :]
    causal = jnp.asarray(np.where(kk <= qq, 0.0, -np.inf).astype(np.float32))
    lane = np.arange(LANES)
    pick = (lane[None] < DECAY_PARTS * FOX_HEADS) & (lane[None] % FOX_HEADS
                                                     == np.arange(FOX_HEADS)[:, None])
    pick = jnp.asarray(np.broadcast_to(pick[:, None, :], (FOX_HEADS, TQ, LANES)), BF16)
    g_rows = jnp.broadcast_to((g_subln * (1.0 - lam_init)).reshape(LANES, 1), (LANES, TQ))
    oblk = pl.BlockSpec((1, BRANCH_WIDTH, Q_BLOCKS * TQ), lambda b, i: (b, 0, i))
    qsel = pl.BlockSpec((2, 1, Q_BLOCKS * TQ, BRANCH_WIDTH), lambda b, i: (0, b, i, 0))
    full = pl.BlockSpec((1, seq, BRANCH_WIDTH), lambda b, i: (b, 0, 0))
    full_v = pl.BlockSpec((1, seq // TK, BRANCH_WIDTH, TK), lambda b, i: (b, 0, 0, 0))
    out = jax.ShapeDtypeStruct((batch, BRANCH_WIDTH, seq), BF16)
    return pl.pallas_call(
        functools.partial(_attn_kernel, lam_init),
        grid=(batch, seq // (Q_BLOCKS * TQ)),
        in_specs=[
            pl.BlockSpec(lam_vecs.shape, const2),
            pl.BlockSpec(g_rows.shape, const2),
            pl.BlockSpec(bias_near.shape, const4),
            pl.BlockSpec(causal.shape, const2),
            pl.BlockSpec(pick.shape, lambda b, i: (0, 0, 0)),
            qsel, full, full_v, qsel, full, full_v,
            pl.BlockSpec((1, seq, LANES), lambda b, i: (b, 0, 0)),
        ],
        out_specs=[oblk, oblk],
        out_shape=[out, out],
        scratch_shapes=[
            pltpu.VMEM((Q_BLOCKS * CHAINS, 1, TQ), F32),
            pltpu.VMEM((Q_BLOCKS * DIFF_CHAINS, LANES + SUM_ROWS, TQ), F32),
            pltpu.VMEM((Q_BLOCKS * FOX_HEADS, HEAD_DIM + SUM_ROWS, TQ), F32),
        ],
        compiler_params=pltpu.CompilerParams(
            dimension_semantics=("parallel", "arbitrary"), vmem_limit_bytes=VMEM_LIMIT),
        name="attention",
    )(lam_vecs, g_rows, bias_near, causal, pick, qd, kd, vd, qf, kf, vf, decay)


def _merge_kernel(x_ref, od_ref, of_ref, g_ref, wg_ref, wpa_ref, wpb_ref, wo_ref, y_ref):
    x = x_ref[...]
    h = _rms(x, g_ref[...]).astype(BF16)
    a = lax.dot_general(od_ref[0], wpa_ref[...], TN_DIMS, preferred_element_type=F32)
    b = lax.dot_general(of_ref[0], wpb_ref[...], TN_DIMS, preferred_element_type=F32)
    ga = jax.nn.sigmoid(jnp.dot(h, wg_ref[:, :D_MODEL], preferred_element_type=F32))
    merged = ga * a
    gb = jax.nn.sigmoid(jnp.dot(h, wg_ref[:, D_MODEL:], preferred_element_type=F32))
    merged = (merged + gb * b).astype(BF16)
    y_ref[...] = x + jnp.dot(merged, wo_ref[...], preferred_element_type=F32)


def _merge(x2, od, of, g_mix, w_gate, w_pa, w_pb, w_o):
    n = x2.shape[0]
    tm = TM_PROJ
    const = lambda i: (0, 0)
    row = lambda i: (i, 0)
    tiles_per_seq = od.shape[2] // tm
    col = pl.BlockSpec((1, BRANCH_WIDTH, tm), lambda i: (i // tiles_per_seq, 0, i % tiles_per_seq))
    return pl.pallas_call(
        _merge_kernel,
        grid=(n // tm,),
        in_specs=[
            pl.BlockSpec((tm, D_MODEL), row),
            col,
            col,
            pl.BlockSpec((1, D_MODEL), const),
            pl.BlockSpec(w_gate.shape, const),
            pl.BlockSpec(w_pa.shape, const),
            pl.BlockSpec(w_pb.shape, const),
            pl.BlockSpec(w_o.shape, const),
        ],
        out_specs=pl.BlockSpec((tm, D_MODEL), row),
        out_shape=jax.ShapeDtypeStruct((n, D_MODEL), F32),
        compiler_params=pltpu.CompilerParams(
            dimension_semantics=("parallel",), vmem_limit_bytes=VMEM_LIMIT),
        name="merge",
    )(x2, od, of, g_mix, w_gate, w_pa, w_pb, w_o)


FF_CHUNK = 1024


def _mlp_kernel(final_norm, x_ref, g_ref, w1_ref, w2_ref, gf_ref, y_ref):
    x = x_ref[...]
    h = _rms(x, g_ref[...]).astype(BF16)
    y = x
    for c in range(D_FF // FF_CHUNK):
        cols = slice(c * FF_CHUNK, (c + 1) * FF_CHUNK)
        u = jnp.maximum(jnp.dot(h, w1_ref[:, cols], preferred_element_type=F32), 0.0)
        y = y + jnp.dot((u * u).astype(BF16), w2_ref[cols, :], preferred_element_type=F32)
    y_ref[...] = _rms(y, gf_ref[...]) if final_norm else y


def _mlp(x2, g_mlp, w_1, w_2, g_final, final_norm):
    n = x2.shape[0]
    tm = TM_PROJ
    const = lambda i: (0, 0)
    row = lambda i: (i, 0)
    single = pl.Buffered(1)
    return pl.pallas_call(
        functools.partial(_mlp_kernel, final_norm),
        grid=(n // tm,),
        in_specs=[
            pl.BlockSpec((tm, D_MODEL), row),
            pl.BlockSpec((1, D_MODEL), const),
            pl.BlockSpec(w_1.shape, const, pipeline_mode=single),
            pl.BlockSpec(w_2.shape, const, pipeline_mode=single),
            pl.BlockSpec((1, D_MODEL), const),
        ],
        out_specs=pl.BlockSpec((tm, D_MODEL), row),
        out_shape=jax.ShapeDtypeStruct((n, D_MODEL), F32),
        compiler_params=pltpu.CompilerParams(
            dimension_semantics=("parallel",), vmem_limit_bytes=VMEM_LIMIT),
        name="mlp",
    )(x2, g_mlp, w_1, w_2, g_final)


def _layer(x, layer_idx, g_mix, w_in, b_f, lam_q1, lam_k1, lam_q2, lam_k2, g_subln,
           w_pa, w_pb, w_o, g_mlp, w_1, w_2, bias_near, g_final, final_norm):
    batch, seq, d = x.shape
    n = batch * seq
    x2 = x.reshape(n, d)
    qkv_cols = 6 * BRANCH_WIDTH
    pad = LANES - DECAY_PARTS * FOX_HEADS
    w_tail = lax.optimization_barrier(w_in[:, qkv_cols:])
    w_fl = jnp.pad(jnp.tile(w_tail[:, :FOX_HEADS], (1, DECAY_PARTS)),
                   ((0, 0), (0, pad))).astype(BF16)
    b_fl = jnp.pad(jnp.tile(b_f, DECAY_PARTS), (0, pad)).reshape(1, LANES)
    w_gate = w_tail[:, FOX_HEADS:].astype(BF16)
    g_mix2 = g_mix.reshape(1, d)

    qd, kd, vd, qf, kf, vf, decay = _in_proj(x2, g_mix2, w_in, w_fl, b_fl, seq)

    lam_vecs = jnp.stack([lam_q1, lam_k1, lam_q2, lam_k2]).astype(F32)
    shape3 = (batch, seq, BRANCH_WIDTH)
    shape4 = (2,) + shape3
    od, of = _attention(lam_vecs, g_subln, bias_near,
                        qd.reshape(shape4), kd.reshape(shape3), vd,
                        qf.reshape(shape4), kf.reshape(shape3), vf,
                        decay.reshape(batch, seq, LANES), _lambda_init(layer_idx))

    x1 = _merge(x2, od, of, g_mix2,
                w_gate, w_pa.astype(BF16), w_pb.astype(BF16), w_o.astype(BF16))
    y = _mlp(x1, g_mlp.reshape(1, d), w_1.astype(BF16), w_2.astype(BF16), g_final, final_norm)
    return y.reshape(batch, seq, d)


def kernel(x, g_mix, w_in, b_f, lam_q1, lam_k1, lam_q2, lam_k2, g_subln, w_pa, w_pb, w_o,
           g_mlp, w_1, w_2, rel_table, g_final):
    depth = g_mix.shape[0]
    batch, seq, d = x.shape
    assert d == D_MODEL and w_in.shape[1:] == (D_MODEL, 6 * BRANCH_WIDTH + FOX_HEADS + 2 * D_MODEL)
    assert w_1.shape[1:] == (D_MODEL, D_FF) and rel_table.shape == (REL_BUCKETS, DIFF_HEADS)
    assert seq % (Q_BLOCKS * TQ) == 0 and seq % TM_IN_PROJ == 0 and seq % TM_PROJ == 0
    bias_near = _bias_tiles(rel_table, seq)
    for l in range(depth):
        x = _layer(x, l, g_mix[l], w_in[l], b_f[l], lam_q1[l], lam_k1[l], lam_q2[l], lam_k2[l],
                   g_subln[l], w_pa[l], w_pb[l], w_o[l], g_mlp[l], w_1[l], w_2[l],
                   bias_near, g_final.reshape(1, -1), l == depth - 1)
    return x
```

```python
import functools
import math

import numpy as np
import jax
import jax.numpy as jnp
from jax import lax
from jax.experimental import pallas as pl
from jax.experimental.pallas import tpu as pltpu

D_MODEL = 1024
CHUNK = 64
HEAD_DIM = 64
DIFF_HEADS = 4
FOX_HEADS = 8
BRANCH_WIDTH = 512
D_FF = 4 * D_MODEL
REL_BUCKETS = 32
REL_MAX_DIST = 128
EPS = 1e-6
LANES = 128
MASKED_BUCKET = REL_BUCKETS

TQ = 256
TK = 256
TM_IN_PROJ = 512
TM_PROJ = 1024
V7X_VMEM_BYTES = 64 * 1024 * 1024
VMEM_LIMIT = V7X_VMEM_BYTES * 7 // 8

LOG2E = math.log2(math.e)
Q_SCALE = HEAD_DIM ** -0.5 * LOG2E

F32 = jnp.float32
BF16 = jnp.bfloat16
NT_DIMS = (((1,), (1,)), ((), ()))
TN_DIMS = (((0,), (0,)), ((), ()))


def _lambda_init(layer_idx):
    return 0.8 - 0.6 * math.exp(-0.3 * layer_idx)


def _rms(xf, g):
    return xf * lax.rsqrt(jnp.mean(xf * xf, axis=-1, keepdims=True) + EPS) * g


DECAY_PARTS = 3


def _in_proj_kernel(tiles_per_seq, x0_ref, xnext_ref, g_ref, w32_ref, wfl_ref, bf_ref,
                    qd_ref, kd_ref, vd_ref, qf_ref, kf_ref, vf_ref, dec_ref,
                    carry_ref, w_ref, h_ref):
    step = pl.program_id(0)

    @pl.when(step == 0)
    def _():
        w_ref[...] = w32_ref[...].astype(BF16)
        h_ref[0] = _rms(x0_ref[...], g_ref[...]).astype(BF16)

    h = h_ref[step % 2]

    @pl.when(step % tiles_per_seq == 0)
    def _():
        carry_ref[...] = jnp.zeros_like(carry_ref)

    z = jnp.dot(h, wfl_ref[...], preferred_element_type=F32) + bf_ref[...]
    acc = jnp.minimum(z, 0.0) - jnp.log1p(jnp.exp(-jnp.abs(z)))
    rows = acc.shape[0]
    row = lax.broadcasted_iota(jnp.int32, acc.shape, 0)
    d = 1
    while d < rows:
        acc = acc + jnp.where(row >= d, pltpu.roll(acc, d, axis=0), 0.0)
        d *= 2
    acc = acc + carry_ref[...]
    carry_ref[...] = acc[rows - 1:rows, :]
    neg = acc * -LOG2E
    hi = neg.astype(BF16).astype(F32)
    mid = (neg - hi).astype(BF16).astype(F32)
    lo = neg - hi - mid
    lane = lax.broadcasted_iota(jnp.int32, acc.shape, 1)
    piece = jnp.where(lane < FOX_HEADS, hi, jnp.where(lane < 2 * FOX_HEADS, mid, lo))
    dec_ref[...] = jnp.where(lane < DECAY_PARTS * FOX_HEADS, piece, 0.0).astype(BF16)

    outs = (qd_ref, kd_ref, vd_ref, qf_ref, kf_ref, vf_ref)
    for c in (2, 5, 0, 3, 1, 4):
        o_ref = outs[c]
        w = w_ref[:, c * BRANCH_WIDTH:(c + 1) * BRANCH_WIDTH]
        o = jnp.dot(h, w, preferred_element_type=F32)
        if o_ref is qd_ref or o_ref is qf_ref:
            o = (o * Q_SCALE).astype(BF16)
            low = lax.broadcasted_iota(jnp.int32, o.shape, 1) % LANES < HEAD_DIM
            o_ref[0] = jnp.where(low, o, jnp.zeros_like(o))
            o_ref[1] = jnp.where(low, jnp.zeros_like(o), o)
        elif o_ref is vd_ref or o_ref is vf_ref:
            o_t = o.T.astype(BF16)
            for t in range(o_ref.shape[1]):
                o_ref[0, t] = o_t[:, t * TK:(t + 1) * TK]
        else:
            o_ref[...] = o.astype(BF16)

    h_ref[(step + 1) % 2] = _rms(xnext_ref[...], g_ref[...]).astype(BF16)


def _in_proj(x2, g_mix, w_in, w_fl, b_f, seq):
    n = x2.shape[0]
    tm = TM_IN_PROJ
    qkv_cols = 6 * BRANCH_WIDTH
    const = lambda i: (0, 0)
    row = lambda i: (i, 0)
    out_bf = jax.ShapeDtypeStruct((n, BRANCH_WIDTH), BF16)
    out_q = jax.ShapeDtypeStruct((2, n, BRANCH_WIDTH), BF16)
    spec_bf = pl.BlockSpec((tm, BRANCH_WIDTH), row)
    spec_q = pl.BlockSpec((2, tm, BRANCH_WIDTH), lambda i: (0, i, 0))
    tiles_per_seq = seq // tm
    out_v = jax.ShapeDtypeStruct((n // seq, seq // TK, BRANCH_WIDTH, TK), BF16)
    spec_v = pl.BlockSpec((1, tm // TK, BRANCH_WIDTH, TK),
                          lambda i: (i // tiles_per_seq, i % tiles_per_seq, 0, 0))
    return pl.pallas_call(
        functools.partial(_in_proj_kernel, seq // tm),
        grid=(n // tm,),
        in_specs=[
            pl.BlockSpec((tm, D_MODEL), const),
            pl.BlockSpec((tm, D_MODEL), lambda i: (jnp.minimum(i + 1, n // tm - 1), 0)),
            pl.BlockSpec((1, D_MODEL), const),
            pl.BlockSpec((D_MODEL, qkv_cols), const, pipeline_mode=pl.Buffered(1)),
            pl.BlockSpec(w_fl.shape, const),
            pl.BlockSpec((1, LANES), const),
        ],
        out_specs=[spec_q, spec_bf, spec_v, spec_q, spec_bf, spec_v,
                   pl.BlockSpec((tm, LANES), row)],
        out_shape=[out_q, out_bf, out_v, out_q, out_bf, out_v,
                   jax.ShapeDtypeStruct((n, LANES), BF16)],
        scratch_shapes=[pltpu.VMEM((1, LANES), F32),
                        pltpu.VMEM((D_MODEL, qkv_cols), BF16),
                        pltpu.VMEM((2, tm, D_MODEL), BF16)],
        compiler_params=pltpu.CompilerParams(
            dimension_semantics=("arbitrary",), vmem_limit_bytes=VMEM_LIMIT),
        name="in_proj",
    )(x2, x2, g_mix, w_in, w_fl, b_f)


def _rel_bucket_np(rel):
    nb = REL_BUCKETS // 2
    ret = np.where(rel > 0, nb, 0)
    n = np.abs(rel)
    max_exact = nb // 2
    nf = np.maximum(n, 1).astype(np.float64)
    large = max_exact + (np.log(nf / max_exact) / math.log(REL_MAX_DIST / max_exact)
                         * (nb - max_exact)).astype(np.int32)
    large = np.minimum(large, nb - 1)
    return (ret + np.where(n < max_exact, n, large)).astype(np.int32)


def _bias_index_maps(seq):
    kk = np.arange(TK, dtype=np.int64)[:, None]
    qq = np.arange(TQ, dtype=np.int64)[None, :]
    diag = _rel_bucket_np(kk - qq)
    diag = np.where(kk // CHUNK <= qq // CHUNK, diag, MASKED_BUCKET).astype(np.int32)
    prev = _rel_bucket_np(kk - TK - qq)
    far = _rel_bucket_np(np.arange(-seq, -TK, dtype=np.int64))
    far_bucket = int(far[0])
    assert (far == far_bucket).all(), "keys two tiles back must share one bucket"
    return diag, prev, far_bucket


def _bias_kernel(far_bucket, buckets, tab_ref, idx_ref, out_ref):
    for h in range(DIFF_HEADS):
        far = tab_ref[far_bucket, h]
        for t in range(idx_ref.shape[0]):
            idx = idx_ref[t]
            acc = jnp.full(idx.shape, -jnp.inf, F32)
            for b in buckets[t]:
                acc = jnp.where(idx == b, (tab_ref[b, h] - far) * LOG2E, acc)
            out_ref[t, h] = acc


def _bias_tiles(rel_table, seq):
    diag, prev, far_bucket = _bias_index_maps(seq)
    idx = np.stack([diag, prev])
    buckets = tuple(tuple(int(b) for b in np.unique(m) if b != MASKED_BUCKET) for m in idx)
    vmem = pl.BlockSpec(memory_space=pltpu.VMEM)
    return pl.pallas_call(
        functools.partial(_bias_kernel, far_bucket, buckets),
        in_specs=[pl.BlockSpec(memory_space=pltpu.SMEM), vmem],
        out_specs=vmem,
        out_shape=jax.ShapeDtypeStruct((idx.shape[0], DIFF_HEADS, TK, TQ), F32),
        name="bias_tiles",
    )(rel_table, jnp.asarray(idx))


DIFF_CHAINS = 2 * DIFF_HEADS
CHAINS = DIFF_CHAINS + FOX_HEADS
QK_AHEAD = 6
Q_BLOCKS = 4
SUM_ROWS = 16


def _attn_kernel(lam_init, lam_ref, gsub_ref, bnear_ref, causal_ref, pick_ref,
                 qd_ref, kd_ref, vd_ref, qf_ref, kf_ref, vf_ref, dec_ref,
                 od_ref, of_ref, m_ref, accd_ref, accf_ref):
    g = pl.program_id(1)
    lam_v = lam_ref[...]
    lam = (jnp.exp(jnp.sum(lam_v[0:1] * lam_v[1:2], axis=-1, keepdims=True))
           - jnp.exp(jnp.sum(lam_v[2:3] * lam_v[3:4], axis=-1, keepdims=True))
           + lam_init)

    def is_fox(c):
        return c >= DIFF_CHAINS

    def cols(c):
        blk = (c % DIFF_CHAINS) // 2
        return slice(blk * LANES, (blk + 1) * LANES)

    def rows(qb):
        return slice(qb * TQ, (qb + 1) * TQ)

    def accumulator(qb, c):
        if is_fox(c):
            return accf_ref.at[qb * FOX_HEADS + c - DIFF_CHAINS]
        return accd_ref.at[qb * DIFF_CHAINS + c]

    def normalised(qb, c):
        acc = accumulator(qb, c)[...]
        chans = acc.shape[0] - SUM_ROWS
        return acc[:chans] * (1.0 / acc[chans:chans + 1])

    def run(seq):
        tiles = {}

        def load(kind, tile, c):
            if kind == "v":
                chan = (slice((c - DIFF_CHAINS) * HEAD_DIM, (c - DIFF_CHAINS + 1) * HEAD_DIM)
                        if is_fox(c) else cols(c))
                key = (kind, id(tile), is_fox(c), chan.start)
                if key not in tiles:
                    v_t = (vf_ref if is_fox(c) else vd_ref)[0, tile, chan, :]
                    tiles[key] = jnp.concatenate([v_t, jnp.ones((SUM_ROWS, TK), BF16)], axis=0)
                return tiles[key]
            key = (kind, id(tile), is_fox(c), cols(c).start)
            if key not in tiles:
                at = pl.ds(pl.multiple_of(tile * TK, TK), TK)
                if is_fox(c):
                    tiles[key] = jnp.concatenate([kf_ref[0, at, cols(c)], dec_ref[0, at, :]],
                                                 axis=1)
                else:
                    tiles[key] = kd_ref[0, at, cols(c)]
            return tiles[key]

        def scores(tile, qb, c, add, first):
            if is_fox(c):
                q_t = jnp.concatenate([qf_ref[c % 2, 0, rows(qb), cols(c)],
                                       pick_ref[c - DIFF_CHAINS]], axis=1)
            else:
                q_t = qd_ref[c % 2, 0, rows(qb), cols(c)]
            s = lax.dot_general(load("k", tile, c), q_t, NT_DIMS, preferred_element_type=F32)
            return s if add is None else s + add()

        pending = {j: scores(*seq[j]) for j in range(min(QK_AHEAD, len(seq)))}
        for j, (tile, qb, c, _, first) in enumerate(seq):
            s = pending.pop(j)
            state = qb * CHAINS + c
            m_new = jnp.max(s, axis=0, keepdims=True)
            if not first:
                m_old = m_ref[state]
                m_new = jnp.maximum(m_old, m_new)
                alpha = jnp.exp2(m_old - m_new)
            p = jnp.exp2(s - m_new).astype(BF16)
            pv = jnp.dot(load("v", tile, c), p, preferred_element_type=F32)
            if j + QK_AHEAD < len(seq):
                pending[j + QK_AHEAD] = scores(*seq[j + QK_AHEAD])
            acc = accumulator(qb, c)
            m_ref[state] = m_new
            acc[...] = pv if first else alpha * acc[...] + pv

    def bias(kind, c):
        if kind == "diag":
            return (lambda: causal_ref[...]) if is_fox(c) else (lambda: bnear_ref[0, c // 2])
        if kind == "prev" and not is_fox(c):
            return lambda: bnear_ref[1, c // 2]
        return None

    def steps(tile, kinds):
        order = [c0 + b for c0 in range(0, DIFF_CHAINS, 2) for b in (0, DIFF_CHAINS)]
        return [(tile, qb, c0 + e, bias(kind, c0 + e), kind == "diag")
                for c0 in order for e in range(2) for qb, kind in kinds.items()]

    base = Q_BLOCKS * g
    head = []
    for u in reversed(range(Q_BLOCKS)):
        head += steps(base + u, {j: "diag" if j == u else "prev" if j == u + 1 else "far"
                                 for j in range(u, Q_BLOCKS)})
    run(head)

    def body(r, carry):
        hi = base - 1 - 2 * r
        lo = hi - 1
        far = {j: "far" for j in range(Q_BLOCKS)}
        pl.when(r == 0)(lambda: run(steps(hi, {**far, 0: "prev"}) + steps(lo, far)))
        pl.when(r > 0)(lambda: run(steps(hi, far) + steps(lo, far)))
        return carry

    lax.fori_loop(0, base // 2, body, 0)

    for qb in range(Q_BLOCKS):
        for h in range(DIFF_HEADS):
            o = normalised(qb, 2 * h) - lam * normalised(qb, 2 * h + 1)
            o = o * lax.rsqrt(jnp.mean(o * o, axis=0, keepdims=True) + EPS) * gsub_ref[...]
            od_ref[0, h * LANES:(h + 1) * LANES, rows(qb)] = o.astype(BF16)
        for h in range(FOX_HEADS):
            o = normalised(qb, DIFF_CHAINS + h)
            of_ref[0, h * HEAD_DIM:(h + 1) * HEAD_DIM, rows(qb)] = o.astype(BF16)


def _attention(lam_vecs, g_subln, bias_near, qd, kd, vd, qf, kf, vf, decay, lam_init):
    batch, seq, _ = kd.shape
    const2 = lambda b, i: (0, 0)
    const4 = lambda b, i: (0, 0, 0, 0)
    kk = np.arange(TK)[:, None]
    qq = np.arange(TQ)[None, :]
    causal = jnp.asarray(np.where(kk <= qq, 0.0, -np.inf).astype(np.float32))
    lane = np.arange(LANES)
    pick = (lane[None] < DECAY_PARTS * FOX_HEADS) & (lane[None] % FOX_HEADS
                                                     == np.arange(FOX_HEADS)[:, None])
    pick = jnp.asarray(np.broadcast_to(pick[:, None, :], (FOX_HEADS, TQ, LANES)), BF16)
    g_rows = jnp.broadcast_to((g_subln * (1.0 - lam_init)).reshape(LANES, 1), (LANES, TQ))
    oblk = pl.BlockSpec((1, BRANCH_WIDTH, Q_BLOCKS * TQ), lambda b, i: (b, 0, i))
    qsel = pl.BlockSpec((2, 1, Q_BLOCKS * TQ, BRANCH_WIDTH), lambda b, i: (0, b, i, 0))
    full = pl.BlockSpec((1, seq, BRANCH_WIDTH), lambda b, i: (b, 0, 0))
    full_v = pl.BlockSpec((1, seq // TK, BRANCH_WIDTH, TK), lambda b, i: (b, 0, 0, 0))
    out = jax.ShapeDtypeStruct((batch, BRANCH_WIDTH, seq), BF16)
    return pl.pallas_call(
        functools.partial(_attn_kernel, lam_init),
        grid=(batch, seq // (Q_BLOCKS * TQ)),
        in_specs=[
            pl.BlockSpec(lam_vecs.shape, const2),
            pl.BlockSpec(g_rows.shape, const2),
            pl.BlockSpec(bias_near.shape, const4),
            pl.BlockSpec(causal.shape, const2),
            pl.BlockSpec(pick.shape, lambda b, i: (0, 0, 0)),
            qsel, full, full_v, qsel, full, full_v,
            pl.BlockSpec((1, seq, LANES), lambda b, i: (b, 0, 0)),
        ],
        out_specs=[oblk, oblk],
        out_shape=[out, out],
        scratch_shapes=[
            pltpu.VMEM((Q_BLOCKS * CHAINS, 1, TQ), F32),
            pltpu.VMEM((Q_BLOCKS * DIFF_CHAINS, LANES + SUM_ROWS, TQ), F32),
            pltpu.VMEM((Q_BLOCKS * FOX_HEADS, HEAD_DIM + SUM_ROWS, TQ), F32),
        ],
        compiler_params=pltpu.CompilerParams(
            dimension_semantics=("parallel", "arbitrary"), vmem_limit_bytes=VMEM_LIMIT),
        name="attention",
    )(lam_vecs, g_rows, bias_near, causal, pick, qd, kd, vd, qf, kf, vf, decay)


def _merge_kernel(x_ref, od_ref, of_ref, g_ref, wg_ref, wpa_ref, wpb_ref, wo_ref, y_ref):
    x = x_ref[...]
    h = _rms(x, g_ref[...]).astype(BF16)
    a = lax.dot_general(od_ref[0], wpa_ref[...], TN_DIMS, preferred_element_type=F32)
    b = lax.dot_general(of_ref[0], wpb_ref[...], TN_DIMS, preferred_element_type=F32)
    ga = jax.nn.sigmoid(jnp.dot(h, wg_ref[:, :D_MODEL], preferred_element_type=F32))
    merged = ga * a
    gb = jax.nn.sigmoid(jnp.dot(h, wg_ref[:, D_MODEL:], preferred_element_type=F32))
    merged = (merged + gb * b).astype(BF16)
    y_ref[...] = x + jnp.dot(merged, wo_ref[...], preferred_element_type=F32)


def _merge(x2, od, of, g_mix, w_gate, w_pa, w_pb, w_o):
    n = x2.shape[0]
    tm = TM_PROJ
    const = lambda i: (0, 0)
    row = lambda i: (i, 0)
    tiles_per_seq = od.shape[2] // tm
    col = pl.BlockSpec((1, BRANCH_WIDTH, tm), lambda i: (i // tiles_per_seq, 0, i % tiles_per_seq))
    return pl.pallas_call(
        _merge_kernel,
        grid=(n // tm,),
        in_specs=[
            pl.BlockSpec((tm, D_MODEL), row),
            col,
            col,
            pl.BlockSpec((1, D_MODEL), const),
            pl.BlockSpec(w_gate.shape, const),
            pl.BlockSpec(w_pa.shape, const),
            pl.BlockSpec(w_pb.shape, const),
            pl.BlockSpec(w_o.shape, const),
        ],
        out_specs=pl.BlockSpec((tm, D_MODEL), row),
        out_shape=jax.ShapeDtypeStruct((n, D_MODEL), F32),
        compiler_params=pltpu.CompilerParams(
            dimension_semantics=("parallel",), vmem_limit_bytes=VMEM_LIMIT),
        name="merge",
    )(x2, od, of, g_mix, w_gate, w_pa, w_pb, w_o)


FF_CHUNK = 1024


def _mlp_kernel(final_norm, x_ref, g_ref, w1_ref, w2_ref, gf_ref, y_ref):
    x = x_ref[...]
    h = _rms(x, g_ref[...]).astype(BF16)
    y = x
    for c in range(D_FF // FF_CHUNK):
        cols = slice(c * FF_CHUNK, (c + 1) * FF_CHUNK)
        u = jnp.maximum(jnp.dot(h, w1_ref[:, cols], preferred_element_type=F32), 0.0)
        y = y + jnp.dot((u * u).astype(BF16), w2_ref[cols, :], preferred_element_type=F32)
    y_ref[...] = _rms(y, gf_ref[...]) if final_norm else y


def _mlp(x2, g_mlp, w_1, w_2, g_final, final_norm):
    n = x2.shape[0]
    tm = TM_PROJ
    const = lambda i: (0, 0)
    row = lambda i: (i, 0)
    single = pl.Buffered(1)
    return pl.pallas_call(
        functools.partial(_mlp_kernel, final_norm),
        grid=(n // tm,),
        in_specs=[
            pl.BlockSpec((tm, D_MODEL), row),
            pl.BlockSpec((1, D_MODEL), const),
            pl.BlockSpec(w_1.shape, const, pipeline_mode=single),
            pl.BlockSpec(w_2.shape, const, pipeline_mode=single),
            pl.BlockSpec((1, D_MODEL), const),
        ],
        out_specs=pl.BlockSpec((tm, D_MODEL), row),
        out_shape=jax.ShapeDtypeStruct((n, D_MODEL), F32),
        compiler_params=pltpu.CompilerParams(
            dimension_semantics=("parallel",), vmem_limit_bytes=VMEM_LIMIT),
        name="mlp",
    )(x2, g_mlp, w_1, w_2, g_final)


def _layer(x, layer_idx, g_mix, w_in, b_f, lam_q1, lam_k1, lam_q2, lam_k2, g_subln,
           w_pa, w_pb, w_o, g_mlp, w_1, w_2, bias_near, g_final, final_norm):
    batch, seq, d = x.shape
    n = batch * seq
    x2 = x.reshape(n, d)
    qkv_cols = 6 * BRANCH_WIDTH
    pad = LANES - DECAY_PARTS * FOX_HEADS
    w_tail = lax.optimization_barrier(w_in[:, qkv_cols:])
    w_fl = jnp.pad(jnp.tile(w_tail[:, :FOX_HEADS], (1, DECAY_PARTS)),
                   ((0, 0), (0, pad))).astype(BF16)
    b_fl = jnp.pad(jnp.tile(b_f, DECAY_PARTS), (0, pad)).reshape(1, LANES)
    w_gate = w_tail[:, FOX_HEADS:].astype(BF16)
    g_mix2 = g_mix.reshape(1, d)

    qd, kd, vd, qf, kf, vf, decay = _in_proj(x2, g_mix2, w_in, w_fl, b_fl, seq)

    lam_vecs = jnp.stack([lam_q1, lam_k1, lam_q2, lam_k2]).astype(F32)
    shape3 = (batch, seq, BRANCH_WIDTH)
    shape4 = (2,) + shape3
    od, of = _attention(lam_vecs, g_subln, bias_near,
                        qd.reshape(shape4), kd.reshape(shape3), vd,
                        qf.reshape(shape4), kf.reshape(shape3), vf,
                        decay.reshape(batch, seq, LANES), _lambda_init(layer_idx))

    x1 = _merge(x2, od, of, g_mix2,
                w_gate, w_pa.astype(BF16), w_pb.astype(BF16), w_o.astype(BF16))
    y = _mlp(x1, g_mlp.reshape(1, d), w_1.astype(BF16), w_2.astype(BF16), g_final, final_norm)
    return y.reshape(batch, seq, d)


def kernel(x, g_mix, w_in, b_f, lam_q1, lam_k1, lam_q2, lam_k2, g_subln, w_pa, w_pb, w_o,
           g_mlp, w_1, w_2, rel_table, g_final):
    depth = g_mix.shape[0]
    batch, seq, d = x.shape
    assert d == D_MODEL and w_in.shape[1:] == (D_MODEL, 6 * BRANCH_WIDTH + FOX_HEADS + 2 * D_MODEL)
    assert w_1.shape[1:] == (D_MODEL, D_FF) and rel_table.shape == (REL_BUCKETS, DIFF_HEADS)
    assert seq % (Q_BLOCKS * TQ) == 0 and seq % TM_IN_PROJ == 0 and seq % TM_PROJ == 0
    bias_near = _bias_tiles(rel_table, seq)
    for l in range(depth):
        x = _layer(x, l, g_mix[l], w_in[l], b_f[l], lam_q1[l], lam_k1[l], lam_q2[l], lam_k2[l],
                   g_subln[l], w_pa[l], w_pb[l], w_o[l], g_mlp[l], w_1[l], w_2[l],
                   bias_near, g_final.reshape(1, -1), l == depth - 1)
    return x
```

```python
import functools
import math

import numpy as np
import jax
import jax.numpy as jnp
from jax import lax
from jax.experimental import pallas as pl
from jax.experimental.pallas import tpu as pltpu

D_MODEL = 1024
CHUNK = 64
HEAD_DIM = 64
DIFF_HEADS = 4
FOX_HEADS = 8
BRANCH_WIDTH = 512
D_FF = 4 * D_MODEL
REL_BUCKETS = 32
REL_MAX_DIST = 128
EPS = 1e-6
LANES = 128
MASKED_BUCKET = REL_BUCKETS

TQ = 256
TK = 256
TM_IN_PROJ = 512
TM_PROJ = 1024
V7X_VMEM_BYTES = 64 * 1024 * 1024
VMEM_LIMIT = V7X_VMEM_BYTES * 7 // 8

LOG2E = math.log2(math.e)
Q_SCALE = HEAD_DIM ** -0.5 * LOG2E

F32 = jnp.float32
BF16 = jnp.bfloat16
NT_DIMS = (((1,), (1,)), ((), ()))
TN_DIMS = (((0,), (0,)), ((), ()))


def _lambda_init(layer_idx):
    return 0.8 - 0.6 * math.exp(-0.3 * layer_idx)


def _rms(xf, g):
    return xf * lax.rsqrt(jnp.mean(xf * xf, axis=-1, keepdims=True) + EPS) * g


DECAY_PARTS = 3


def _in_proj_kernel(tiles_per_seq, x0_ref, xnext_ref, g_ref, w32_ref, wfl_ref, bf_ref,
                    qd_ref, kd_ref, vd_ref, qf_ref, kf_ref, vf_ref, dec_ref,
                    carry_ref, w_ref, h_ref):
    step = pl.program_id(0)

    @pl.when(step == 0)
    def _():
        w_ref[...] = w32_ref[...].astype(BF16)
        h_ref[0] = _rms(x0_ref[...], g_ref[...]).astype(BF16)

    h = h_ref[step % 2]

    @pl.when(step % tiles_per_seq == 0)
    def _():
        carry_ref[...] = jnp.zeros_like(carry_ref)

    z = jnp.dot(h, wfl_ref[...], preferred_element_type=F32) + bf_ref[...]
    acc = jnp.minimum(z, 0.0) - jnp.log1p(jnp.exp(-jnp.abs(z)))
    rows = acc.shape[0]
    row = lax.broadcasted_iota(jnp.int32, acc.shape, 0)
    d = 1
    while d < rows:
        acc = acc + jnp.where(row >= d, pltpu.roll(acc, d, axis=0), 0.0)
        d *= 2
    acc = acc + carry_ref[...]
    carry_ref[...] = acc[rows - 1:rows, :]
    neg = acc * -LOG2E
    hi = neg.astype(BF16).astype(F32)
    mid = (neg - hi).astype(BF16).astype(F32)
    lo = neg - hi - mid
    lane = lax.broadcasted_iota(jnp.int32, acc.shape, 1)
    piece = jnp.where(lane < FOX_HEADS, hi, jnp.where(lane < 2 * FOX_HEADS, mid, lo))
    dec_ref[...] = jnp.where(lane < DECAY_PARTS * FOX_HEADS, piece, 0.0).astype(BF16)

    outs = (qd_ref, kd_ref, vd_ref, qf_ref, kf_ref, vf_ref)
    for c in (2, 5, 0, 3, 1, 4):
        o_ref = outs[c]
        w = w_ref[:, c * BRANCH_WIDTH:(c + 1) * BRANCH_WIDTH]
        o = jnp.dot(h, w, preferred_element_type=F32)
        if o_ref is qd_ref or o_ref is qf_ref:
            o = (o * Q_SCALE).astype(BF16)
            low = lax.broadcasted_iota(jnp.int32, o.shape, 1) % LANES < HEAD_DIM
            o_ref[0] = jnp.where(low, o, jnp.zeros_like(o))
            o_ref[1] = jnp.where(low, jnp.zeros_like(o), o)
        elif o_ref is vd_ref or o_ref is vf_ref:
            o_t = o.T.astype(BF16)
            for t in range(o_ref.shape[1]):
                o_ref[0, t] = o_t[:, t * TK:(t + 1) * TK]
        else:
            o_ref[...] = o.astype(BF16)

    h_ref[(step + 1) % 2] = _rms(xnext_ref[...], g_ref[...]).astype(BF16)


def _in_proj(x2, g_mix, w_in, w_fl, b_f, seq):
    n = x2.shape[0]
    tm = TM_IN_PROJ
    qkv_cols = 6 * BRANCH_WIDTH
    const = lambda i: (0, 0)
    row = lambda i: (i, 0)
    out_bf = jax.ShapeDtypeStruct((n, BRANCH_WIDTH), BF16)
    out_q = jax.ShapeDtypeStruct((2, n, BRANCH_WIDTH), BF16)
    spec_bf = pl.BlockSpec((tm, BRANCH_WIDTH), row)
    spec_q = pl.BlockSpec((2, tm, BRANCH_WIDTH), lambda i: (0, i, 0))
    tiles_per_seq = seq // tm
    out_v = jax.ShapeDtypeStruct((n // seq, seq // TK, BRANCH_WIDTH, TK), BF16)
    spec_v = pl.BlockSpec((1, tm // TK, BRANCH_WIDTH, TK),
                          lambda i: (i // tiles_per_seq, i % tiles_per_seq, 0, 0))
    return pl.pallas_call(
        functools.partial(_in_proj_kernel, seq // tm),
        grid=(n // tm,),
        in_specs=[
            pl.BlockSpec((tm, D_MODEL), const),
            pl.BlockSpec((tm, D_MODEL), lambda i: (jnp.minimum(i + 1, n // tm - 1), 0)),
            pl.BlockSpec((1, D_MODEL), const),
            pl.BlockSpec((D_MODEL, qkv_cols), const, pipeline_mode=pl.Buffered(1)),
            pl.BlockSpec(w_fl.shape, const),
            pl.BlockSpec((1, LANES), const),
        ],
        out_specs=[spec_q, spec_bf, spec_v, spec_q, spec_bf, spec_v,
                   pl.BlockSpec((tm, LANES), row)],
        out_shape=[out_q, out_bf, out_v, out_q, out_bf, out_v,
                   jax.ShapeDtypeStruct((n, LANES), BF16)],
        scratch_shapes=[pltpu.VMEM((1, LANES), F32),
                        pltpu.VMEM((D_MODEL, qkv_cols), BF16),
                        pltpu.VMEM((2, tm, D_MODEL), BF16)],
        compiler_params=pltpu.CompilerParams(
            dimension_semantics=("arbitrary",), vmem_limit_bytes=VMEM_LIMIT),
        name="in_proj",
    )(x2, x2, g_mix, w_in, w_fl, b_f)


def _rel_bucket_np(rel):
    nb = REL_BUCKETS // 2
    ret = np.where(rel > 0, nb, 0)
    n = np.abs(rel)
    max_exact = nb // 2
    nf = np.maximum(n, 1).astype(np.float64)
    large = max_exact + (np.log(nf / max_exact) / math.log(REL_MAX_DIST / max_exact)
                         * (nb - max_exact)).astype(np.int32)
    large = np.minimum(large, nb - 1)
    return (ret + np.where(n < max_exact, n, large)).astype(np.int32)


def _bias_index_maps(seq):
    kk = np.arange(TK, dtype=np.int64)[:, None]
    qq = np.arange(TQ, dtype=np.int64)[None, :]
    diag = _rel_bucket_np(kk - qq)
    diag = np.where(kk // CHUNK <= qq // CHUNK, diag, MASKED_BUCKET).astype(np.int32)
    prev = _rel_bucket_np(kk - TK - qq)
    far = _rel_bucket_np(np.arange(-seq, -TK, dtype=np.int64))
    far_bucket = int(far[0])
    assert (far == far_bucket).all(), "keys two tiles back must share one bucket"
    return diag, prev, far_bucket


def _bias_kernel(far_bucket, buckets, tab_ref, idx_ref, out_ref):
    for h in range(DIFF_HEADS):
        far = tab_ref[far_bucket, h]
        for t in range(idx_ref.shape[0]):
            idx = idx_ref[t]
            acc = jnp.full(idx.shape, -jnp.inf, F32)
            for b in buckets[t]:
                acc = jnp.where(idx == b, (tab_ref[b, h] - far) * LOG2E, acc)
            out_ref[t, h] = acc


def _bias_tiles(rel_table, seq):
    diag, prev, far_bucket = _bias_index_maps(seq)
    idx = np.stack([diag, prev])
    buckets = tuple(tuple(int(b) for b in np.unique(m) if b != MASKED_BUCKET) for m in idx)
    vmem = pl.BlockSpec(memory_space=pltpu.VMEM)
    return pl.pallas_call(
        functools.partial(_bias_kernel, far_bucket, buckets),
        in_specs=[pl.BlockSpec(memory_space=pltpu.SMEM), vmem],
        out_specs=vmem,
        out_shape=jax.ShapeDtypeStruct((idx.shape[0], DIFF_HEADS, TK, TQ), F32),
        name="bias_tiles",
    )(rel_table, jnp.asarray(idx))


DIFF_CHAINS = 2 * DIFF_HEADS
CHAINS = DIFF_CHAINS + FOX_HEADS
QK_AHEAD = 6
Q_BLOCKS = 4
SUM_ROWS = 16


def _attn_kernel(lam_init, lam_ref, gsub_ref, bnear_ref, causal_ref, pick_ref,
                 qd_ref, kd_ref, vd_ref, qf_ref, kf_ref, vf_ref, dec_ref,
                 od_ref, of_ref, m_ref, accd_ref, accf_ref):
    g = pl.program_id(1)
    lam_v = lam_ref[...]
    lam = (jnp.exp(jnp.sum(lam_v[0:1] * lam_v[1:2], axis=-1, keepdims=True))
           - jnp.exp(jnp.sum(lam_v[2:3] * lam_v[3:4], axis=-1, keepdims=True))
           + lam_init)

    def is_fox(c):
        return c >= DIFF_CHAINS

    def cols(c):
        blk = (c % DIFF_CHAINS) // 2
        return slice(blk * LANES, (blk + 1) * LANES)

    def rows(qb):
        return slice(qb * TQ, (qb + 1) * TQ)

    def accumulator(qb, c):
        if is_fox(c):
            return accf_ref.at[qb * FOX_HEADS + c - DIFF_CHAINS]
        return accd_ref.at[qb * DIFF_CHAINS + c]

    def normalised(qb, c):
        acc = accumulator(qb, c)[...]
        chans = acc.shape[0] - SUM_ROWS
        return acc[:chans] * (1.0 / acc[chans:chans + 1])

    def run(seq):
        tiles = {}

        def load(kind, tile, c):
            if kind == "v":
                chan = (slice((c - DIFF_CHAINS) * HEAD_DIM, (c - DIFF_CHAINS + 1) * HEAD_DIM)
                        if is_fox(c) else cols(c))
                key = (kind, id(tile), is_fox(c), chan.start)
                if key not in tiles:
                    v_t = (vf_ref if is_fox(c) else vd_ref)[0, tile, chan, :]
                    tiles[key] = jnp.concatenate([v_t, jnp.ones((SUM_ROWS, TK), BF16)], axis=0)
                return tiles[key]
            key = (kind, id(tile), is_fox(c), cols(c).start)
            if key not in tiles:
                at = pl.ds(pl.multiple_of(tile * TK, TK), TK)
                if is_fox(c):
                    tiles[key] = jnp.concatenate([kf_ref[0, at, cols(c)], dec_ref[0, at, :]],
                                                 axis=1)
                else:
                    tiles[key] = kd_ref[0, at, cols(c)]
            return tiles[key]

        def scores(tile, qb, c, add, first):
            if is_fox(c):
                q_t = jnp.concatenate([qf_ref[c % 2, 0, rows(qb), cols(c)],
                                       pick_ref[c - DIFF_CHAINS]], axis=1)
            else:
                q_t = qd_ref[c % 2, 0, rows(qb), cols(c)]
            s = lax.dot_general(load("k", tile, c), q_t, NT_DIMS, preferred_element_type=F32)
            return s if add is None else s + add()

        pending = {j: scores(*seq[j]) for j in range(min(QK_AHEAD, len(seq)))}
        for j, (tile, qb, c, _, first) in enumerate(seq):
            s = pending.pop(j)
            state = qb * CHAINS + c
            m_new = jnp.max(s, axis=0, keepdims=True)
            if not first:
                m_old = m_ref[state]
                m_new = jnp.maximum(m_old, m_new)
                alpha = jnp.exp2(m_old - m_new)
            p = jnp.exp2(s - m_new).astype(BF16)
            pv = jnp.dot(load("v", tile, c), p, preferred_element_type=F32)
            if j + QK_AHEAD < len(seq):
                pending[j + QK_AHEAD] = scores(*seq[j + QK_AHEAD])
            acc = accumulator(qb, c)
            m_ref[state] = m_new
            acc[...] = pv if first else alpha * acc[...] + pv

    def bias(kind, c):
        if kind == "diag":
            return (lambda: causal_ref[...]) if is_fox(c) else (lambda: bnear_ref[0, c // 2])
        if kind == "prev" and not is_fox(c):
            return lambda: bnear_ref[1, c // 2]
        return None

    def steps(tile, kinds):
        order = [c0 + b for c0 in range(0, DIFF_CHAINS, 2) for b in (0, DIFF_CHAINS)]
        return [(tile, qb, c0 + e, bias(kind, c0 + e), kind == "diag")
                for c0 in order for qb, kind in kinds.items() for e in range(2)]

    base = Q_BLOCKS * g
    head = []
    for u in reversed(range(Q_BLOCKS)):
        head += steps(base + u, {j: "diag" if j == u else "prev" if j == u + 1 else "far"
                                 for j in range(u, Q_BLOCKS)})
    run(head)

    def body(r, carry):
        hi = base - 1 - 2 * r
        lo = hi - 1
        far = {j: "far" for j in range(Q_BLOCKS)}
        pl.when(r == 0)(lambda: run(steps(hi, {**far, 0: "prev"}) + steps(lo, far)))
        pl.when(r > 0)(lambda: run(steps(hi, far) + steps(lo, far)))
        return carry

    lax.fori_loop(0, base // 2, body, 0)

    for qb in range(Q_BLOCKS):
        for h in range(DIFF_HEADS):
            o = normalised(qb, 2 * h) - lam * normalised(qb, 2 * h + 1)
            o = o * lax.rsqrt(jnp.mean(o * o, axis=0, keepdims=True) + EPS) * gsub_ref[...]
            od_ref[0, h * LANES:(h + 1) * LANES, rows(qb)] = o.astype(BF16)
        for h in range(FOX_HEADS):
            o = normalised(qb, DIFF_CHAINS + h)
            of_ref[0, h * HEAD_DIM:(h + 1) * HEAD_DIM, rows(qb)] = o.astype(BF16)


def _attention(lam_vecs, g_subln, bias_near, qd, kd, vd, qf, kf, vf, decay, lam_init):
    batch, seq, _ = kd.shape
    const2 = lambda b, i: (0, 0)
    const4 = lambda b, i: (0, 0, 0, 0)
    kk = np.arange(TK)[:, None]
    qq = np.arange(TQ)[None, :]
    causal = jnp.asarray(np.where(kk <= qq, 0.0, -np.inf).astype(np.float32))
    lane = np.arange(LANES)
    pick = (lane[None] < DECAY_PARTS * FOX_HEADS) & (lane[None] % FOX_HEADS
                                                     == np.arange(FOX_HEADS)[:, None])
    pick = jnp.asarray(np.broadcast_to(pick[:, None, :], (FOX_HEADS, TQ, LANES)), BF16)
    g_rows = jnp.broadcast_to((g_subln * (1.0 - lam_init)).reshape(LANES, 1), (LANES, TQ))
    oblk = pl.BlockSpec((1, BRANCH_WIDTH, Q_BLOCKS * TQ), lambda b, i: (b, 0, i))
    qsel = pl.BlockSpec((2, 1, Q_BLOCKS * TQ, BRANCH_WIDTH), lambda b, i: (0, b, i, 0))
    full = pl.BlockSpec((1, seq, BRANCH_WIDTH), lambda b, i: (b, 0, 0))
    full_v = pl.BlockSpec((1, seq // TK, BRANCH_WIDTH, TK), lambda b, i: (b, 0, 0, 0))
    out = jax.ShapeDtypeStruct((batch, BRANCH_WIDTH, seq), BF16)
    return pl.pallas_call(
        functools.partial(_attn_kernel, lam_init),
        grid=(batch, seq // (Q_BLOCKS * TQ)),
        in_specs=[
            pl.BlockSpec(lam_vecs.shape, const2),
            pl.BlockSpec(g_rows.shape, const2),
            pl.BlockSpec(bias_near.shape, const4),
            pl.BlockSpec(causal.shape, const2),
            pl.BlockSpec(pick.shape, lambda b, i: (0, 0, 0)),
            qsel, full, full_v, qsel, full, full_v,
            pl.BlockSpec((1, seq, LANES), lambda b, i: (b, 0, 0)),
        ],
        out_specs=[oblk, oblk],
        out_shape=[out, out],
        scratch_shapes=[
            pltpu.VMEM((Q_BLOCKS * CHAINS, 1, TQ), F32),
            pltpu.VMEM((Q_BLOCKS * DIFF_CHAINS, LANES + SUM_ROWS, TQ), F32),
            pltpu.VMEM((Q_BLOCKS * FOX_HEADS, HEAD_DIM + SUM_ROWS, TQ), F32),
        ],
        compiler_params=pltpu.CompilerParams(
            dimension_semantics=("parallel", "arbitrary"), vmem_limit_bytes=VMEM_LIMIT),
        name="attention",
    )(lam_vecs, g_rows, bias_near, causal, pick, qd, kd, vd, qf, kf, vf, decay)


def _merge_kernel(x_ref, od_ref, of_ref, g_ref, wg_ref, wpa_ref, wpb_ref, wo_ref, y_ref):
    x = x_ref[...]
    h = _rms(x, g_ref[...]).astype(BF16)
    a = lax.dot_general(od_ref[0], wpa_ref[...], TN_DIMS, preferred_element_type=F32)
    b = lax.dot_general(of_ref[0], wpb_ref[...], TN_DIMS, preferred_element_type=F32)
    ga = jax.nn.sigmoid(jnp.dot(h, wg_ref[:, :D_MODEL], preferred_element_type=F32))
    merged = ga * a
    gb = jax.nn.sigmoid(jnp.dot(h, wg_ref[:, D_MODEL:], preferred_element_type=F32))
    merged = (merged + gb * b).astype(BF16)
    y_ref[...] = x + jnp.dot(merged, wo_ref[...], preferred_element_type=F32)


def _merge(x2, od, of, g_mix, w_gate, w_pa, w_pb, w_o):
    n = x2.shape[0]
    tm = TM_PROJ
    const = lambda i: (0, 0)
    row = lambda i: (i, 0)
    tiles_per_seq = od.shape[2] // tm
    col = pl.BlockSpec((1, BRANCH_WIDTH, tm), lambda i: (i // tiles_per_seq, 0, i % tiles_per_seq))
    return pl.pallas_call(
        _merge_kernel,
        grid=(n // tm,),
        in_specs=[
            pl.BlockSpec((tm, D_MODEL), row),
            col,
            col,
            pl.BlockSpec((1, D_MODEL), const),
            pl.BlockSpec(w_gate.shape, const),
            pl.BlockSpec(w_pa.shape, const),
            pl.BlockSpec(w_pb.shape, const),
            pl.BlockSpec(w_o.shape, const),
        ],
        out_specs=pl.BlockSpec((tm, D_MODEL), row),
        out_shape=jax.ShapeDtypeStruct((n, D_MODEL), F32),
        compiler_params=pltpu.CompilerParams(
            dimension_semantics=("parallel",), vmem_limit_bytes=VMEM_LIMIT),
        name="merge",
    )(x2, od, of, g_mix, w_gate, w_pa, w_pb, w_o)


FF_CHUNK = 2048


def _mlp_kernel(final_norm, x_ref, g_ref, w1_ref, w2_ref, gf_ref, y_ref):
    x = x_ref[...]
    h = _rms(x, g_ref[...]).astype(BF16)
    y = x
    for c in range(D_FF // FF_CHUNK):
        cols = slice(c * FF_CHUNK, (c + 1) * FF_CHUNK)
        u = jnp.maximum(jnp.dot(h, w1_ref[:, cols], preferred_element_type=F32), 0.0)
        y = y + jnp.dot((u * u).astype(BF16), w2_ref[cols, :], preferred_element_type=F32)
    y_ref[...] = _rms(y, gf_ref[...]) if final_norm else y


def _mlp(x2, g_mlp, w_1, w_2, g_final, final_norm):
    n = x2.shape[0]
    tm = TM_PROJ
    const = lambda i: (0, 0)
    row = lambda i: (i, 0)
    single = pl.Buffered(1)
    return pl.pallas_call(
        functools.partial(_mlp_kernel, final_norm),
        grid=(n // tm,),
        in_specs=[
            pl.BlockSpec((tm, D_MODEL), row),
            pl.BlockSpec((1, D_MODEL), const),
            pl.BlockSpec(w_1.shape, const, pipeline_mode=single),
            pl.BlockSpec(w_2.shape, const, pipeline_mode=single),
            pl.BlockSpec((1, D_MODEL), const),
        ],
        out_specs=pl.BlockSpec((tm, D_MODEL), row),
        out_shape=jax.ShapeDtypeStruct((n, D_MODEL), F32),
        compiler_params=pltpu.CompilerParams(
            dimension_semantics=("parallel",), vmem_limit_bytes=VMEM_LIMIT),
        name="mlp",
    )(x2, g_mlp, w_1, w_2, g_final)


def _layer(x, layer_idx, g_mix, w_in, b_f, lam_q1, lam_k1, lam_q2, lam_k2, g_subln,
           w_pa, w_pb, w_o, g_mlp, w_1, w_2, bias_near, g_final, final_norm):
    batch, seq, d = x.shape
    n = batch * seq
    x2 = x.reshape(n, d)
    qkv_cols = 6 * BRANCH_WIDTH
    pad = LANES - DECAY_PARTS * FOX_HEADS
    w_tail = lax.optimization_barrier(w_in[:, qkv_cols:])
    w_fl = jnp.pad(jnp.tile(w_tail[:, :FOX_HEADS], (1, DECAY_PARTS)),
                   ((0, 0), (0, pad))).astype(BF16)
    b_fl = jnp.pad(jnp.tile(b_f, DECAY_PARTS), (0, pad)).reshape(1, LANES)
    w_gate = w_tail[:, FOX_HEADS:].astype(BF16)
    g_mix2 = g_mix.reshape(1, d)

    qd, kd, vd, qf, kf, vf, decay = _in_proj(x2, g_mix2, w_in, w_fl, b_fl, seq)

    lam_vecs = jnp.stack([lam_q1, lam_k1, lam_q2, lam_k2]).astype(F32)
    shape3 = (batch, seq, BRANCH_WIDTH)
    shape4 = (2,) + shape3
    od, of = _attention(lam_vecs, g_subln, bias_near,
                        qd.reshape(shape4), kd.reshape(shape3), vd,
                        qf.reshape(shape4), kf.reshape(shape3), vf,
                        decay.reshape(batch, seq, LANES), _lambda_init(layer_idx))

    x1 = _merge(x2, od, of, g_mix2,
                w_gate, w_pa.astype(BF16), w_pb.astype(BF16), w_o.astype(BF16))
    y = _mlp(x1, g_mlp.reshape(1, d), w_1.astype(BF16), w_2.astype(BF16), g_final, final_norm)
    return y.reshape(batch, seq, d)


def kernel(x, g_mix, w_in, b_f, lam_q1, lam_k1, lam_q2, lam_k2, g_subln, w_pa, w_pb, w_o,
           g_mlp, w_1, w_2, rel_table, g_final):
    depth = g_mix.shape[0]
    batch, seq, d = x.shape
    assert d == D_MODEL and w_in.shape[1:] == (D_MODEL, 6 * BRANCH_WIDTH + FOX_HEADS + 2 * D_MODEL)
    assert w_1.shape[1:] == (D_MODEL, D_FF) and rel_table.shape == (REL_BUCKETS, DIFF_HEADS)
    assert seq % (Q_BLOCKS * TQ) == 0 and seq % TM_IN_PROJ == 0 and seq % TM_PROJ == 0
    bias_near = _bias_tiles(rel_table, seq)
    for l in range(depth):
        x = _layer(x, l, g_mix[l], w_in[l], b_f[l], lam_q1[l], lam_k1[l], lam_q2[l], lam_k2[l],
                   g_subln[l], w_pa[l], w_pb[l], w_o[l], g_mlp[l], w_1[l], w_2[l],
                   bias_near, g_final.reshape(1, -1), l == depth - 1)
    return x
```

```python
import functools
import math

import numpy as np
import jax
import jax.numpy as jnp
from jax import lax
from jax.experimental import pallas as pl
from jax.experimental.pallas import tpu as pltpu

D_MODEL = 1024
CHUNK = 64
HEAD_DIM = 64
DIFF_HEADS = 4
FOX_HEADS = 8
BRANCH_WIDTH = 512
D_FF = 4 * D_MODEL
REL_BUCKETS = 32
REL_MAX_DIST = 128
EPS = 1e-6
LANES = 128
MASKED_BUCKET = REL_BUCKETS

TQ = 256
TK = 256
TM_IN_PROJ = 512
TM_PROJ = 1024
V7X_VMEM_BYTES = 64 * 1024 * 1024
VMEM_LIMIT = V7X_VMEM_BYTES * 7 // 8

LOG2E = math.log2(math.e)
Q_SCALE = HEAD_DIM ** -0.5 * LOG2E

F32 = jnp.float32
BF16 = jnp.bfloat16
NT_DIMS = (((1,), (1,)), ((), ()))
TN_DIMS = (((0,), (0,)), ((), ()))


def _lambda_init(layer_idx):
    return 0.8 - 0.6 * math.exp(-0.3 * layer_idx)


def _rms(xf, g):
    return xf * lax.rsqrt(jnp.mean(xf * xf, axis=-1, keepdims=True) + EPS) * g


DECAY_PARTS = 3


def _in_proj_kernel(tiles_per_seq, x0_ref, xnext_ref, g_ref, w32_ref, wfl_ref, bf_ref,
                    qd_ref, kd_ref, vd_ref, qf_ref, kf_ref, vf_ref, dec_ref,
                    carry_ref, w_ref, h_ref):
    step = pl.program_id(0)

    @pl.when(step == 0)
    def _():
        w_ref[...] = w32_ref[...].astype(BF16)
        h_ref[0] = _rms(x0_ref[...], g_ref[...]).astype(BF16)

    h = h_ref[step % 2]

    @pl.when(step % tiles_per_seq == 0)
    def _():
        carry_ref[...] = jnp.zeros_like(carry_ref)

    z = jnp.dot(h, wfl_ref[...], preferred_element_type=F32) + bf_ref[...]
    acc = jnp.minimum(z, 0.0) - jnp.log1p(jnp.exp(-jnp.abs(z)))
    rows = acc.shape[0]
    row = lax.broadcasted_iota(jnp.int32, acc.shape, 0)
    d = 1
    while d < rows:
        acc = acc + jnp.where(row >= d, pltpu.roll(acc, d, axis=0), 0.0)
        d *= 2
    acc = acc + carry_ref[...]
    carry_ref[...] = acc[rows - 1:rows, :]
    neg = acc * -LOG2E
    hi = neg.astype(BF16).astype(F32)
    mid = (neg - hi).astype(BF16).astype(F32)
    lo = neg - hi - mid
    lane = lax.broadcasted_iota(jnp.int32, acc.shape, 1)
    piece = jnp.where(lane < FOX_HEADS, hi, jnp.where(lane < 2 * FOX_HEADS, mid, lo))
    dec_ref[...] = jnp.where(lane < DECAY_PARTS * FOX_HEADS, piece, 0.0).astype(BF16)

    outs = (qd_ref, kd_ref, vd_ref, qf_ref, kf_ref, vf_ref)
    for c in (2, 5, 0, 3, 1, 4):
        o_ref = outs[c]
        w = w_ref[:, c * BRANCH_WIDTH:(c + 1) * BRANCH_WIDTH]
        o = jnp.dot(h, w, preferred_element_type=F32)
        if o_ref is qd_ref or o_ref is qf_ref:
            o = (o * Q_SCALE).astype(BF16)
            low = lax.broadcasted_iota(jnp.int32, o.shape, 1) % LANES < HEAD_DIM
            o_ref[0] = jnp.where(low, o, jnp.zeros_like(o))
            o_ref[1] = jnp.where(low, jnp.zeros_like(o), o)
        elif o_ref is vd_ref or o_ref is vf_ref:
            o_t = o.T.astype(BF16)
            for t in range(o_ref.shape[1]):
                o_ref[0, t] = o_t[:, t * TK:(t + 1) * TK]
        else:
            o_ref[...] = o.astype(BF16)

    h_ref[(step + 1) % 2] = _rms(xnext_ref[...], g_ref[...]).astype(BF16)


def _in_proj(x2, g_mix, w_in, w_fl, b_f, seq):
    n = x2.shape[0]
    tm = TM_IN_PROJ
    qkv_cols = 6 * BRANCH_WIDTH
    const = lambda i: (0, 0)
    row = lambda i: (i, 0)
    out_bf = jax.ShapeDtypeStruct((n, BRANCH_WIDTH), BF16)
    out_q = jax.ShapeDtypeStruct((2, n, BRANCH_WIDTH), BF16)
    spec_bf = pl.BlockSpec((tm, BRANCH_WIDTH), row)
    spec_q = pl.BlockSpec((2, tm, BRANCH_WIDTH), lambda i: (0, i, 0))
    tiles_per_seq = seq // tm
    out_v = jax.ShapeDtypeStruct((n // seq, seq // TK, BRANCH_WIDTH, TK), BF16)
    spec_v = pl.BlockSpec((1, tm // TK, BRANCH_WIDTH, TK),
                          lambda i: (i // tiles_per_seq, i % tiles_per_seq, 0, 0))
    return pl.pallas_call(
        functools.partial(_in_proj_kernel, seq // tm),
        grid=(n // tm,),
        in_specs=[
            pl.BlockSpec((tm, D_MODEL), const),
            pl.BlockSpec((tm, D_MODEL), lambda i: (jnp.minimum(i + 1, n // tm - 1), 0)),
            pl.BlockSpec((1, D_MODEL), const),
            pl.BlockSpec((D_MODEL, qkv_cols), const, pipeline_mode=pl.Buffered(1)),
            pl.BlockSpec(w_fl.shape, const),
            pl.BlockSpec((1, LANES), const),
        ],
        out_specs=[spec_q, spec_bf, spec_v, spec_q, spec_bf, spec_v,
                   pl.BlockSpec((tm, LANES), row)],
        out_shape=[out_q, out_bf, out_v, out_q, out_bf, out_v,
                   jax.ShapeDtypeStruct((n, LANES), BF16)],
        scratch_shapes=[pltpu.VMEM((1, LANES), F32),
                        pltpu.VMEM((D_MODEL, qkv_cols), BF16),
                        pltpu.VMEM((2, tm, D_MODEL), BF16)],
        compiler_params=pltpu.CompilerParams(
            dimension_semantics=("arbitrary",), vmem_limit_bytes=VMEM_LIMIT),
        name="in_proj",
    )(x2, x2, g_mix, w_in, w_fl, b_f)


def _rel_bucket_np(rel):
    nb = REL_BUCKETS // 2
    ret = np.where(rel > 0, nb, 0)
    n = np.abs(rel)
    max_exact = nb // 2
    nf = np.maximum(n, 1).astype(np.float64)
    large = max_exact + (np.log(nf / max_exact) / math.log(REL_MAX_DIST / max_exact)
                         * (nb - max_exact)).astype(np.int32)
    large = np.minimum(large, nb - 1)
    return (ret + np.where(n < max_exact, n, large)).astype(np.int32)


def _bias_index_maps(seq):
    kk = np.arange(TK, dtype=np.int64)[:, None]
    qq = np.arange(TQ, dtype=np.int64)[None, :]
    diag = _rel_bucket_np(kk - qq)
    diag = np.where(kk // CHUNK <= qq // CHUNK, diag, MASKED_BUCKET).astype(np.int32)
    prev = _rel_bucket_np(kk - TK - qq)
    far = _rel_bucket_np(np.arange(-seq, -TK, dtype=np.int64))
    far_bucket = int(far[0])
    assert (far == far_bucket).all(), "keys two tiles back must share one bucket"
    return diag, prev, far_bucket


def _bias_kernel(far_bucket, buckets, tab_ref, idx_ref, out_ref):
    for h in range(DIFF_HEADS):
        far = tab_ref[far_bucket, h]
        for t in range(idx_ref.shape[0]):
            idx = idx_ref[t]
            acc = jnp.full(idx.shape, -jnp.inf, F32)
            for b in buckets[t]:
                acc = jnp.where(idx == b, (tab_ref[b, h] - far) * LOG2E, acc)
            out_ref[t, h] = acc


def _bias_tiles(rel_table, seq):
    diag, prev, far_bucket = _bias_index_maps(seq)
    idx = np.stack([diag, prev])
    buckets = tuple(tuple(int(b) for b in np.unique(m) if b != MASKED_BUCKET) for m in idx)
    vmem = pl.BlockSpec(memory_space=pltpu.VMEM)
    return pl.pallas_call(
        functools.partial(_bias_kernel, far_bucket, buckets),
        in_specs=[pl.BlockSpec(memory_space=pltpu.SMEM), vmem],
        out_specs=vmem,
        out_shape=jax.ShapeDtypeStruct((idx.shape[0], DIFF_HEADS, TK, TQ), F32),
        name="bias_tiles",
    )(rel_table, jnp.asarray(idx))


DIFF_CHAINS = 2 * DIFF_HEADS
CHAINS = DIFF_CHAINS + FOX_HEADS
QK_AHEAD = 6
Q_BLOCKS = 4
SUM_ROWS = 16


def _attn_kernel(lam_init, lam_ref, gsub_ref, bnear_ref, causal_ref, pick_ref,
                 qd_ref, kd_ref, vd_ref, qf_ref, kf_ref, vf_ref, dec_ref,
                 od_ref, of_ref, m_ref, accd_ref, accf_ref):
    g = pl.program_id(1)
    lam_v = lam_ref[...]
    lam = (jnp.exp(jnp.sum(lam_v[0:1] * lam_v[1:2], axis=-1, keepdims=True))
           - jnp.exp(jnp.sum(lam_v[2:3] * lam_v[3:4], axis=-1, keepdims=True))
           + lam_init)

    def is_fox(c):
        return c >= DIFF_CHAINS

    def cols(c):
        blk = (c % DIFF_CHAINS) // 2
        return slice(blk * LANES, (blk + 1) * LANES)

    def rows(qb):
        return slice(qb * TQ, (qb + 1) * TQ)

    def accumulator(qb, c):
        if is_fox(c):
            return accf_ref.at[qb * FOX_HEADS + c - DIFF_CHAINS]
        return accd_ref.at[qb * DIFF_CHAINS + c]

    def normalised(qb, c):
        acc = accumulator(qb, c)[...]
        chans = acc.shape[0] - SUM_ROWS
        return acc[:chans] * (1.0 / acc[chans:chans + 1])

    def run(seq):
        tiles = {}

        def load(kind, tile, c):
            if kind == "v":
                chan = (slice((c - DIFF_CHAINS) * HEAD_DIM, (c - DIFF_CHAINS + 1) * HEAD_DIM)
                        if is_fox(c) else cols(c))
                key = (kind, id(tile), is_fox(c), chan.start)
                if key not in tiles:
                    v_t = (vf_ref if is_fox(c) else vd_ref)[0, tile, chan, :]
                    tiles[key] = jnp.concatenate([v_t, jnp.ones((SUM_ROWS, TK), BF16)], axis=0)
                return tiles[key]
            key = (kind, id(tile), is_fox(c), cols(c).start)
            if key not in tiles:
                at = pl.ds(pl.multiple_of(tile * TK, TK), TK)
                if is_fox(c):
                    tiles[key] = jnp.concatenate([kf_ref[0, at, cols(c)], dec_ref[0, at, :]],
                                                 axis=1)
                else:
                    tiles[key] = kd_ref[0, at, cols(c)]
            return tiles[key]

        def scores(tile, qb, c, add, first):
            if is_fox(c):
                q_t = jnp.concatenate([qf_ref[c % 2, 0, rows(qb), cols(c)],
                                       pick_ref[c - DIFF_CHAINS]], axis=1)
            else:
                q_t = qd_ref[c % 2, 0, rows(qb), cols(c)]
            s = lax.dot_general(load("k", tile, c), q_t, NT_DIMS, preferred_element_type=F32)
            return s if add is None else s + add()

        pending = {j: scores(*seq[j]) for j in range(min(QK_AHEAD, len(seq)))}
        for j, (tile, qb, c, _, first) in enumerate(seq):
            s = pending.pop(j)
            state = qb * CHAINS + c
            m_new = jnp.max(s, axis=0, keepdims=True)
            if not first:
                m_old = m_ref[state]
                m_new = jnp.maximum(m_old, m_new)
                alpha = jnp.exp2(m_old - m_new)
            p = jnp.exp2(s - m_new).astype(BF16)
            pv = jnp.dot(load("v", tile, c), p, preferred_element_type=F32)
            if j + QK_AHEAD < len(seq):
                pending[j + QK_AHEAD] = scores(*seq[j + QK_AHEAD])
            acc = accumulator(qb, c)
            m_ref[state] = m_new
            acc[...] = pv if first else alpha * acc[...] + pv

    def bias(kind, c):
        if kind == "diag":
            return (lambda: causal_ref[...]) if is_fox(c) else (lambda: bnear_ref[0, c // 2])
        if kind == "prev" and not is_fox(c):
            return lambda: bnear_ref[1, c // 2]
        return None

    def steps(tile, kinds):
        order = [c0 + b for c0 in range(0, DIFF_CHAINS, 2) for b in (0, DIFF_CHAINS)]
        return [(tile, qb, c0 + e, bias(kind, c0 + e), kind == "diag")
                for c0 in order for qb, kind in kinds.items() for e in range(2)]

    base = Q_BLOCKS * g
    head = []
    for u in reversed(range(Q_BLOCKS)):
        head += steps(base + u, {j: "diag" if j == u else "prev" if j == u + 1 else "far"
                                 for j in range(u, Q_BLOCKS)})
    run(head)

    def body(r, carry):
        hi = base - 1 - 2 * r
        lo = hi - 1
        far = {j: "far" for j in range(Q_BLOCKS)}
        pl.when(r == 0)(lambda: run(steps(hi, {**far, 0: "prev"}) + steps(lo, far)))
        pl.when(r > 0)(lambda: run(steps(hi, far) + steps(lo, far)))
        return carry

    lax.fori_loop(0, base // 2, body, 0)

    for qb in range(Q_BLOCKS):
        for h in range(DIFF_HEADS):
            o = normalised(qb, 2 * h) - lam * normalised(qb, 2 * h + 1)
            o = o * lax.rsqrt(jnp.mean(o * o, axis=0, keepdims=True) + EPS) * gsub_ref[...]
            od_ref[0, h * LANES:(h + 1) * LANES, rows(qb)] = o.astype(BF16)
        for h in range(FOX_HEADS):
            o = normalised(qb, DIFF_CHAINS + h)
            of_ref[0, h * HEAD_DIM:(h + 1) * HEAD_DIM, rows(qb)] = o.astype(BF16)


def _attention(lam_vecs, g_subln, bias_near, qd, kd, vd, qf, kf, vf, decay, lam_init):
    batch, seq, _ = kd.shape
    const2 = lambda b, i: (0, 0)
    const4 = lambda b, i: (0, 0, 0, 0)
    kk = np.arange(TK)[:, None]
    qq = np.arange(TQ)[None, :]
    causal = jnp.asarray(np.where(kk <= qq, 0.0, -np.inf).astype(np.float32))
    lane = np.arange(LANES)
    pick = (lane[None] < DECAY_PARTS * FOX_HEADS) & (lane[None] % FOX_HEADS
                                                     == np.arange(FOX_HEADS)[:, None])
    pick = jnp.asarray(np.broadcast_to(pick[:, None, :], (FOX_HEADS, TQ, LANES)), BF16)
    g_rows = jnp.broadcast_to((g_subln * (1.0 - lam_init)).reshape(LANES, 1), (LANES, TQ))
    oblk = pl.BlockSpec((1, BRANCH_WIDTH, Q_BLOCKS * TQ), lambda b, i: (b, 0, i))
    qsel = pl.BlockSpec((2, 1, Q_BLOCKS * TQ, BRANCH_WIDTH), lambda b, i: (0, b, i, 0))
    full = pl.BlockSpec((1, seq, BRANCH_WIDTH), lambda b, i: (b, 0, 0))
    full_v = pl.BlockSpec((1, seq // TK, BRANCH_WIDTH, TK), lambda b, i: (b, 0, 0, 0))
    out = jax.ShapeDtypeStruct((batch, BRANCH_WIDTH, seq), BF16)
    return pl.pallas_call(
        functools.partial(_attn_kernel, lam_init),
        grid=(batch, seq // (Q_BLOCKS * TQ)),
        in_specs=[
            pl.BlockSpec(lam_vecs.shape, const2),
            pl.BlockSpec(g_rows.shape, const2),
            pl.BlockSpec(bias_near.shape, const4),
            pl.BlockSpec(causal.shape, const2),
            pl.BlockSpec(pick.shape, lambda b, i: (0, 0, 0)),
            qsel, full, full_v, qsel, full, full_v,
            pl.BlockSpec((1, seq, LANES), lambda b, i: (b, 0, 0)),
        ],
        out_specs=[oblk, oblk],
        out_shape=[out, out],
        scratch_shapes=[
            pltpu.VMEM((Q_BLOCKS * CHAINS, 1, TQ), F32),
            pltpu.VMEM((Q_BLOCKS * DIFF_CHAINS, LANES + SUM_ROWS, TQ), F32),
            pltpu.VMEM((Q_BLOCKS * FOX_HEADS, HEAD_DIM + SUM_ROWS, TQ), F32),
        ],
        compiler_params=pltpu.CompilerParams(
            dimension_semantics=("parallel", "arbitrary"), vmem_limit_bytes=VMEM_LIMIT),
        name="attention",
    )(lam_vecs, g_rows, bias_near, causal, pick, qd, kd, vd, qf, kf, vf, decay)


def _merge_kernel(x_ref, od_ref, of_ref, g_ref, wg_ref, wpa_ref, wpb_ref, wo_ref, y_ref):
    x = x_ref[...]
    h = _rms(x, g_ref[...]).astype(BF16)
    a = lax.dot_general(od_ref[0], wpa_ref[...], TN_DIMS, preferred_element_type=F32)
    b = lax.dot_general(of_ref[0], wpb_ref[...], TN_DIMS, preferred_element_type=F32)
    ga = jax.nn.sigmoid(jnp.dot(h, wg_ref[:, :D_MODEL], preferred_element_type=F32))
    merged = ga * a
    gb = jax.nn.sigmoid(jnp.dot(h, wg_ref[:, D_MODEL:], preferred_element_type=F32))
    merged = (merged + gb * b).astype(BF16)
    y_ref[...] = x + jnp.dot(merged, wo_ref[...], preferred_element_type=F32)


def _merge(x2, od, of, g_mix, w_gate, w_pa, w_pb, w_o):
    n = x2.shape[0]
    tm = TM_PROJ
    const = lambda i: (0, 0)
    row = lambda i: (i, 0)
    tiles_per_seq = od.shape[2] // tm
    col = pl.BlockSpec((1, BRANCH_WIDTH, tm), lambda i: (i // tiles_per_seq, 0, i % tiles_per_seq))
    return pl.pallas_call(
        _merge_kernel,
        grid=(n // tm,),
        in_specs=[
            pl.BlockSpec((tm, D_MODEL), row),
            col,
            col,
            pl.BlockSpec((1, D_MODEL), const),
            pl.BlockSpec(w_gate.shape, const),
            pl.BlockSpec(w_pa.shape, const),
            pl.BlockSpec(w_pb.shape, const),
            pl.BlockSpec(w_o.shape, const),
        ],
        out_specs=pl.BlockSpec((tm, D_MODEL), row),
        out_shape=jax.ShapeDtypeStruct((n, D_MODEL), F32),
        compiler_params=pltpu.CompilerParams(
            dimension_semantics=("parallel",), vmem_limit_bytes=VMEM_LIMIT),
        name="merge",
    )(x2, od, of, g_mix, w_gate, w_pa, w_pb, w_o)


FF_CHUNK = 512


def _mlp_kernel(final_norm, x_ref, g_ref, w1_ref, w2_ref, gf_ref, y_ref):
    x = x_ref[...]
    h = _rms(x, g_ref[...]).astype(BF16)
    y = x
    for c in range(D_FF // FF_CHUNK):
        cols = slice(c * FF_CHUNK, (c + 1) * FF_CHUNK)
        u = jnp.maximum(jnp.dot(h, w1_ref[:, cols], preferred_element_type=F32), 0.0)
        y = y + jnp.dot((u * u).astype(BF16), w2_ref[cols, :], preferred_element_type=F32)
    y_ref[...] = _rms(y, gf_ref[...]) if final_norm else y


def _mlp(x2, g_mlp, w_1, w_2, g_final, final_norm):
    n = x2.shape[0]
    tm = TM_PROJ
    const = lambda i: (0, 0)
    row = lambda i: (i, 0)
    single = pl.Buffered(1)
    return pl.pallas_call(
        functools.partial(_mlp_kernel, final_norm),
        grid=(n // tm,),
        in_specs=[
            pl.BlockSpec((tm, D_MODEL), row),
            pl.BlockSpec((1, D_MODEL), const),
            pl.BlockSpec(w_1.shape, const, pipeline_mode=single),
            pl.BlockSpec(w_2.shape, const, pipeline_mode=single),
            pl.BlockSpec((1, D_MODEL), const),
        ],
        out_specs=pl.BlockSpec((tm, D_MODEL), row),
        out_shape=jax.ShapeDtypeStruct((n, D_MODEL), F32),
        compiler_params=pltpu.CompilerParams(
            dimension_semantics=("parallel",), vmem_limit_bytes=VMEM_LIMIT),
        name="mlp",
    )(x2, g_mlp, w_1, w_2, g_final)


def _layer(x, layer_idx, g_mix, w_in, b_f, lam_q1, lam_k1, lam_q2, lam_k2, g_subln,
           w_pa, w_pb, w_o, g_mlp, w_1, w_2, bias_near, g_final, final_norm):
    batch, seq, d = x.shape
    n = batch * seq
    x2 = x.reshape(n, d)
    qkv_cols = 6 * BRANCH_WIDTH
    pad = LANES - DECAY_PARTS * FOX_HEADS
    w_tail = lax.optimization_barrier(w_in[:, qkv_cols:])
    w_fl = jnp.pad(jnp.tile(w_tail[:, :FOX_HEADS], (1, DECAY_PARTS)),
                   ((0, 0), (0, pad))).astype(BF16)
    b_fl = jnp.pad(jnp.tile(b_f, DECAY_PARTS), (0, pad)).reshape(1, LANES)
    w_gate = w_tail[:, FOX_HEADS:].astype(BF16)
    g_mix2 = g_mix.reshape(1, d)

    qd, kd, vd, qf, kf, vf, decay = _in_proj(x2, g_mix2, w_in, w_fl, b_fl, seq)

    lam_vecs = jnp.stack([lam_q1, lam_k1, lam_q2, lam_k2]).astype(F32)
    shape3 = (batch, seq, BRANCH_WIDTH)
    shape4 = (2,) + shape3
    od, of = _attention(lam_vecs, g_subln, bias_near,
                        qd.reshape(shape4), kd.reshape(shape3), vd,
                        qf.reshape(shape4), kf.reshape(shape3), vf,
                        decay.reshape(batch, seq, LANES), _lambda_init(layer_idx))

    x1 = _merge(x2, od, of, g_mix2,
                w_gate, w_pa.astype(BF16), w_pb.astype(BF16), w_o.astype(BF16))
    y = _mlp(x1, g_mlp.reshape(1, d), w_1.astype(BF16), w_2.astype(BF16), g_final, final_norm)
    return y.reshape(batch, seq, d)


def kernel(x, g_mix, w_in, b_f, lam_q1, lam_k1, lam_q2, lam_k2, g_subln, w_pa, w_pb, w_o,
           g_mlp, w_1, w_2, rel_table, g_final):
    depth = g_mix.shape[0]
    batch, seq, d = x.shape
    assert d == D_MODEL and w_in.shape[1:] == (D_MODEL, 6 * BRANCH_WIDTH + FOX_HEADS + 2 * D_MODEL)
    assert w_1.shape[1:] == (D_MODEL, D_FF) and rel_table.shape == (REL_BUCKETS, DIFF_HEADS)
    assert seq % (Q_BLOCKS * TQ) == 0 and seq % TM_IN_PROJ == 0 and seq % TM_PROJ == 0
    bias_near = _bias_tiles(rel_table, seq)
    for l in range(depth):
        x = _layer(x, l, g_mix[l], w_in[l], b_f[l], lam_q1[l], lam_k1[l], lam_q2[l], lam_k2[l],
                   g_subln[l], w_pa[l], w_pb[l], w_o[l], g_mlp[l], w_1[l], w_2[l],
                   bias_near, g_final.reshape(1, -1), l == depth - 1)
    return x
```

```python
import functools
import math

import numpy as np
import jax
import jax.numpy as jnp
from jax import lax
from jax.experimental import pallas as pl
from jax.experimental.pallas import tpu as pltpu

D_MODEL = 1024
CHUNK = 64
HEAD_DIM = 64
DIFF_HEADS = 4
FOX_HEADS = 8
BRANCH_WIDTH = 512
D_FF = 4 * D_MODEL
REL_BUCKETS = 32
REL_MAX_DIST = 128
EPS = 1e-6
LANES = 128
MASKED_BUCKET = REL_BUCKETS

TQ = 256
TK = 256
TM_IN_PROJ = 512
TM_PROJ = 1024
V7X_VMEM_BYTES = 64 * 1024 * 1024
VMEM_LIMIT = V7X_VMEM_BYTES * 7 // 8

LOG2E = math.log2(math.e)
Q_SCALE = HEAD_DIM ** -0.5 * LOG2E

F32 = jnp.float32
BF16 = jnp.bfloat16
NT_DIMS = (((1,), (1,)), ((), ()))
TN_DIMS = (((0,), (0,)), ((), ()))


def _lambda_init(layer_idx):
    return 0.8 - 0.6 * math.exp(-0.3 * layer_idx)


def _rms(xf, g):
    return xf * lax.rsqrt(jnp.mean(xf * xf, axis=-1, keepdims=True) + EPS) * g


DECAY_PARTS = 3


def _in_proj_kernel(tiles_per_seq, x0_ref, xnext_ref, g_ref, w32_ref, wfl_ref, bf_ref,
                    qd_ref, kd_ref, vd_ref, qf_ref, kf_ref, vf_ref, dec_ref,
                    carry_ref, w_ref, h_ref):
    step = pl.program_id(0)

    @pl.when(step == 0)
    def _():
        w_ref[...] = w32_ref[...].astype(BF16)
        h_ref[0] = _rms(x0_ref[...], g_ref[...]).astype(BF16)

    h = h_ref[step % 2]

    @pl.when(step % tiles_per_seq == 0)
    def _():
        carry_ref[...] = jnp.zeros_like(carry_ref)

    z = jnp.dot(h, wfl_ref[...], preferred_element_type=F32) + bf_ref[...]
    acc = jnp.minimum(z, 0.0) - jnp.log1p(jnp.exp(-jnp.abs(z)))
    rows = acc.shape[0]
    row = lax.broadcasted_iota(jnp.int32, acc.shape, 0)
    d = 1
    while d < rows:
        acc = acc + jnp.where(row >= d, pltpu.roll(acc, d, axis=0), 0.0)
        d *= 2
    acc = acc + carry_ref[...]
    carry_ref[...] = acc[rows - 1:rows, :]
    neg = acc * -LOG2E
    hi = neg.astype(BF16).astype(F32)
    mid = (neg - hi).astype(BF16).astype(F32)
    lo = neg - hi - mid
    lane = lax.broadcasted_iota(jnp.int32, acc.shape, 1)
    piece = jnp.where(lane < FOX_HEADS, hi, jnp.where(lane < 2 * FOX_HEADS, mid, lo))
    dec_ref[...] = jnp.where(lane < DECAY_PARTS * FOX_HEADS, piece, 0.0).astype(BF16)

    outs = (qd_ref, kd_ref, vd_ref, qf_ref, kf_ref, vf_ref)
    for c in (2, 5, 0, 3, 1, 4):
        o_ref = outs[c]
        w = w_ref[:, c * BRANCH_WIDTH:(c + 1) * BRANCH_WIDTH]
        o = jnp.dot(h, w, preferred_element_type=F32)
        if o_ref is qd_ref or o_ref is qf_ref:
            o = (o * Q_SCALE).astype(BF16)
            low = lax.broadcasted_iota(jnp.int32, o.shape, 1) % LANES < HEAD_DIM
            o_ref[0] = jnp.where(low, o, jnp.zeros_like(o))
            o_ref[1] = jnp.where(low, jnp.zeros_like(o), o)
        elif o_ref is vd_ref or o_ref is vf_ref:
            o_t = o.T.astype(BF16)
            for t in range(o_ref.shape[1]):
                o_ref[0, t] = o_t[:, t * TK:(t + 1) * TK]
        else:
            o_ref[...] = o.astype(BF16)

    h_ref[(step + 1) % 2] = _rms(xnext_ref[...], g_ref[...]).astype(BF16)


def _in_proj(x2, g_mix, w_in, w_fl, b_f, seq):
    n = x2.shape[0]
    tm = TM_IN_PROJ
    qkv_cols = 6 * BRANCH_WIDTH
    const = lambda i: (0, 0)
    row = lambda i: (i, 0)
    out_bf = jax.ShapeDtypeStruct((n, BRANCH_WIDTH), BF16)
    out_q = jax.ShapeDtypeStruct((2, n, BRANCH_WIDTH), BF16)
    spec_bf = pl.BlockSpec((tm, BRANCH_WIDTH), row)
    spec_q = pl.BlockSpec((2, tm, BRANCH_WIDTH), lambda i: (0, i, 0))
    tiles_per_seq = seq // tm
    out_v = jax.ShapeDtypeStruct((n // seq, seq // TK, BRANCH_WIDTH, TK), BF16)
    spec_v = pl.BlockSpec((1, tm // TK, BRANCH_WIDTH, TK),
                          lambda i: (i // tiles_per_seq, i % tiles_per_seq, 0, 0))
    return pl.pallas_call(
        functools.partial(_in_proj_kernel, seq // tm),
        grid=(n // tm,),
        in_specs=[
            pl.BlockSpec((tm, D_MODEL), const),
            pl.BlockSpec((tm, D_MODEL), lambda i: (jnp.minimum(i + 1, n // tm - 1), 0)),
            pl.BlockSpec((1, D_MODEL), const),
            pl.BlockSpec((D_MODEL, qkv_cols), const, pipeline_mode=pl.Buffered(1)),
            pl.BlockSpec(w_fl.shape, const),
            pl.BlockSpec((1, LANES), const),
        ],
        out_specs=[spec_q, spec_bf, spec_v, spec_q, spec_bf, spec_v,
                   pl.BlockSpec((tm, LANES), row)],
        out_shape=[out_q, out_bf, out_v, out_q, out_bf, out_v,
                   jax.ShapeDtypeStruct((n, LANES), BF16)],
        scratch_shapes=[pltpu.VMEM((1, LANES), F32),
                        pltpu.VMEM((D_MODEL, qkv_cols), BF16),
                        pltpu.VMEM((2, tm, D_MODEL), BF16)],
        compiler_params=pltpu.CompilerParams(
            dimension_semantics=("arbitrary",), vmem_limit_bytes=VMEM_LIMIT),
        name="in_proj",
    )(x2, x2, g_mix, w_in, w_fl, b_f)


def _rel_bucket_np(rel):
    nb = REL_BUCKETS // 2
    ret = np.where(rel > 0, nb, 0)
    n = np.abs(rel)
    max_exact = nb // 2
    nf = np.maximum(n, 1).astype(np.float64)
    large = max_exact + (np.log(nf / max_exact) / math.log(REL_MAX_DIST / max_exact)
                         * (nb - max_exact)).astype(np.int32)
    large = np.minimum(large, nb - 1)
    return (ret + np.where(n < max_exact, n, large)).astype(np.int32)


def _bias_index_maps(seq):
    kk = np.arange(TK, dtype=np.int64)[:, None]
    qq = np.arange(TQ, dtype=np.int64)[None, :]
    diag = _rel_bucket_np(kk - qq)
    diag = np.where(kk // CHUNK <= qq // CHUNK, diag, MASKED_BUCKET).astype(np.int32)
    prev = _rel_bucket_np(kk - TK - qq)
    far = _rel_bucket_np(np.arange(-seq, -TK, dtype=np.int64))
    far_bucket = int(far[0])
    assert (far == far_bucket).all(), "keys two tiles back must share one bucket"
    return diag, prev, far_bucket


def _bias_kernel(far_bucket, buckets, tab_ref, idx_ref, out_ref):
    for h in range(DIFF_HEADS):
        far = tab_ref[far_bucket, h]
        for t in range(idx_ref.shape[0]):
            idx = idx_ref[t]
            acc = jnp.full(idx.shape, -jnp.inf, F32)
            for b in buckets[t]:
                acc = jnp.where(idx == b, (tab_ref[b, h] - far) * LOG2E, acc)
            out_ref[t, h] = acc


def _bias_tiles(rel_table, seq):
    diag, prev, far_bucket = _bias_index_maps(seq)
    idx = np.stack([diag, prev])
    buckets = tuple(tuple(int(b) for b in np.unique(m) if b != MASKED_BUCKET) for m in idx)
    vmem = pl.BlockSpec(memory_space=pltpu.VMEM)
    return pl.pallas_call(
        functools.partial(_bias_kernel, far_bucket, buckets),
        in_specs=[pl.BlockSpec(memory_space=pltpu.SMEM), vmem],
        out_specs=vmem,
        out_shape=jax.ShapeDtypeStruct((idx.shape[0], DIFF_HEADS, TK, TQ), F32),
        name="bias_tiles",
    )(rel_table, jnp.asarray(idx))


DIFF_CHAINS = 2 * DIFF_HEADS
CHAINS = DIFF_CHAINS + FOX_HEADS
QK_AHEAD = 6
Q_BLOCKS = 4
SUM_ROWS = 16


def _attn_kernel(lam_init, lam_ref, gsub_ref, bnear_ref, causal_ref, pick_ref,
                 qd_ref, kd_ref, vd_ref, qf_ref, kf_ref, vf_ref, dec_ref,
                 od_ref, of_ref, m_ref, accd_ref, accf_ref):
    g = pl.program_id(1)
    lam_v = lam_ref[...]
    lam = (jnp.exp(jnp.sum(lam_v[0:1] * lam_v[1:2], axis=-1, keepdims=True))
           - jnp.exp(jnp.sum(lam_v[2:3] * lam_v[3:4], axis=-1, keepdims=True))
           + lam_init)

    def is_fox(c):
        return c >= DIFF_CHAINS

    def cols(c):
        blk = (c % DIFF_CHAINS) // 2
        return slice(blk * LANES, (blk + 1) * LANES)

    def rows(qb):
        return slice(qb * TQ, (qb + 1) * TQ)

    def accumulator(qb, c):
        if is_fox(c):
            return accf_ref.at[qb * FOX_HEADS + c - DIFF_CHAINS]
        return accd_ref.at[qb * DIFF_CHAINS + c]

    def normalised(qb, c):
        acc = accumulator(qb, c)[...]
        chans = acc.shape[0] - SUM_ROWS
        return acc[:chans] * (1.0 / acc[chans:chans + 1])

    def run(seq):
        tiles = {}

        def load(kind, tile, c):
            if kind == "v":
                chan = (slice((c - DIFF_CHAINS) * HEAD_DIM, (c - DIFF_CHAINS + 1) * HEAD_DIM)
                        if is_fox(c) else cols(c))
                key = (kind, id(tile), is_fox(c), chan.start)
                if key not in tiles:
                    v_t = (vf_ref if is_fox(c) else vd_ref)[0, tile, chan, :]
                    tiles[key] = jnp.concatenate([v_t, jnp.ones((SUM_ROWS, TK), BF16)], axis=0)
                return tiles[key]
            key = (kind, id(tile), is_fox(c), cols(c).start)
            if key not in tiles:
                at = pl.ds(pl.multiple_of(tile * TK, TK), TK)
                if is_fox(c):
                    tiles[key] = jnp.concatenate([kf_ref[0, at, cols(c)], dec_ref[0, at, :]],
                                                 axis=1)
                else:
                    tiles[key] = kd_ref[0, at, cols(c)]
            return tiles[key]

        def scores(tile, qb, c, add, first):
            if is_fox(c):
                q_t = jnp.concatenate([qf_ref[c % 2, 0, rows(qb), cols(c)],
                                       pick_ref[c - DIFF_CHAINS]], axis=1)
            else:
                q_t = qd_ref[c % 2, 0, rows(qb), cols(c)]
            s = lax.dot_general(load("k", tile, c), q_t, NT_DIMS, preferred_element_type=F32)
            return s if add is None else s + add()

        pending = {j: scores(*seq[j]) for j in range(min(QK_AHEAD, len(seq)))}
        for j, (tile, qb, c, _, first) in enumerate(seq):
            s = pending.pop(j)
            state = qb * CHAINS + c
            m_old = None if first else m_ref[state]
            halves = []
            for lo in range(0, TQ, LANES):
                s_h = s[:, lo:lo + LANES]
                m_h = jnp.max(s_h, axis=0, keepdims=True)
                if not first:
                    m_h = jnp.maximum(m_old[:, lo:lo + LANES], m_h)
                halves.append((m_h, jnp.exp2(s_h - m_h).astype(BF16)))
            m_new = jnp.concatenate([m_h for m_h, _ in halves], axis=1)
            p = jnp.concatenate([p_h for _, p_h in halves], axis=1)
            if not first:
                alpha = jnp.exp2(m_old - m_new)
            pv = jnp.dot(load("v", tile, c), p, preferred_element_type=F32)
            if j + QK_AHEAD < len(seq):
                pending[j + QK_AHEAD] = scores(*seq[j + QK_AHEAD])
            acc = accumulator(qb, c)
            m_ref[state] = m_new
            acc[...] = pv if first else alpha * acc[...] + pv

    def bias(kind, c):
        if kind == "diag":
            return (lambda: causal_ref[...]) if is_fox(c) else (lambda: bnear_ref[0, c // 2])
        if kind == "prev" and not is_fox(c):
            return lambda: bnear_ref[1, c // 2]
        return None

    def steps(tile, kinds):
        order = [c0 + b for c0 in range(0, DIFF_CHAINS, 2) for b in (0, DIFF_CHAINS)]
        return [(tile, qb, c0 + e, bias(kind, c0 + e), kind == "diag")
                for c0 in order for qb, kind in kinds.items() for e in range(2)]

    base = Q_BLOCKS * g
    head = []
    for u in reversed(range(Q_BLOCKS)):
        head += steps(base + u, {j: "diag" if j == u else "prev" if j == u + 1 else "far"
                                 for j in range(u, Q_BLOCKS)})
    run(head)

    def body(r, carry):
        hi = base - 1 - 2 * r
        lo = hi - 1
        far = {j: "far" for j in range(Q_BLOCKS)}
        pl.when(r == 0)(lambda: run(steps(hi, {**far, 0: "prev"}) + steps(lo, far)))
        pl.when(r > 0)(lambda: run(steps(hi, far) + steps(lo, far)))
        return carry

    lax.fori_loop(0, base // 2, body, 0)

    for qb in range(Q_BLOCKS):
        for h in range(DIFF_HEADS):
            o = normalised(qb, 2 * h) - lam * normalised(qb, 2 * h + 1)
            o = o * lax.rsqrt(jnp.mean(o * o, axis=0, keepdims=True) + EPS) * gsub_ref[...]
            od_ref[0, h * LANES:(h + 1) * LANES, rows(qb)] = o.astype(BF16)
        for h in range(FOX_HEADS):
            o = normalised(qb, DIFF_CHAINS + h)
            of_ref[0, h * HEAD_DIM:(h + 1) * HEAD_DIM, rows(qb)] = o.astype(BF16)


def _attention(lam_vecs, g_subln, bias_near, qd, kd, vd, qf, kf, vf, decay, lam_init):
    batch, seq, _ = kd.shape
    const2 = lambda b, i: (0, 0)
    const4 = lambda b, i: (0, 0, 0, 0)
    kk = np.arange(TK)[:, None]
    qq = np.arange(TQ)[None, :]
    causal = jnp.asarray(np.where(kk <= qq, 0.0, -np.inf).astype(np.float32))
    lane = np.arange(LANES)
    pick = (lane[None] < DECAY_PARTS * FOX_HEADS) & (lane[None] % FOX_HEADS
                                                     == np.arange(FOX_HEADS)[:, None])
    pick = jnp.asarray(np.broadcast_to(pick[:, None, :], (FOX_HEADS, TQ, LANES)), BF16)
    g_rows = jnp.broadcast_to((g_subln * (1.0 - lam_init)).reshape(LANES, 1), (LANES, TQ))
    oblk = pl.BlockSpec((1, BRANCH_WIDTH, Q_BLOCKS * TQ), lambda b, i: (b, 0, i))
    qsel = pl.BlockSpec((2, 1, Q_BLOCKS * TQ, BRANCH_WIDTH), lambda b, i: (0, b, i, 0))
    full = pl.BlockSpec((1, seq, BRANCH_WIDTH), lambda b, i: (b, 0, 0))
    full_v = pl.BlockSpec((1, seq // TK, BRANCH_WIDTH, TK), lambda b, i: (b, 0, 0, 0))
    out = jax.ShapeDtypeStruct((batch, BRANCH_WIDTH, seq), BF16)
    return pl.pallas_call(
        functools.partial(_attn_kernel, lam_init),
        grid=(batch, seq // (Q_BLOCKS * TQ)),
        in_specs=[
            pl.BlockSpec(lam_vecs.shape, const2),
            pl.BlockSpec(g_rows.shape, const2),
            pl.BlockSpec(bias_near.shape, const4),
            pl.BlockSpec(causal.shape, const2),
            pl.BlockSpec(pick.shape, lambda b, i: (0, 0, 0)),
            qsel, full, full_v, qsel, full, full_v,
            pl.BlockSpec((1, seq, LANES), lambda b, i: (b, 0, 0)),
        ],
        out_specs=[oblk, oblk],
        out_shape=[out, out],
        scratch_shapes=[
            pltpu.VMEM((Q_BLOCKS * CHAINS, 1, TQ), F32),
            pltpu.VMEM((Q_BLOCKS * DIFF_CHAINS, LANES + SUM_ROWS, TQ), F32),
            pltpu.VMEM((Q_BLOCKS * FOX_HEADS, HEAD_DIM + SUM_ROWS, TQ), F32),
        ],
        compiler_params=pltpu.CompilerParams(
            dimension_semantics=("parallel", "arbitrary"), vmem_limit_bytes=VMEM_LIMIT),
        name="attention",
    )(lam_vecs, g_rows, bias_near, causal, pick, qd, kd, vd, qf, kf, vf, decay)


def _merge_kernel(x_ref, od_ref, of_ref, g_ref, wg_ref, wpa_ref, wpb_ref, wo_ref, y_ref):
    x = x_ref[...]
    h = _rms(x, g_ref[...]).astype(BF16)
    a = lax.dot_general(od_ref[0], wpa_ref[...], TN_DIMS, preferred_element_type=F32)
    b = lax.dot_general(of_ref[0], wpb_ref[...], TN_DIMS, preferred_element_type=F32)
    ga = jax.nn.sigmoid(jnp.dot(h, wg_ref[:, :D_MODEL], preferred_element_type=F32))
    merged = ga * a
    gb = jax.nn.sigmoid(jnp.dot(h, wg_ref[:, D_MODEL:], preferred_element_type=F32))
    merged = (merged + gb * b).astype(BF16)
    y_ref[...] = x + jnp.dot(merged, wo_ref[...], preferred_element_type=F32)


def _merge(x2, od, of, g_mix, w_gate, w_pa, w_pb, w_o):
    n = x2.shape[0]
    tm = TM_PROJ
    const = lambda i: (0, 0)
    row = lambda i: (i, 0)
    tiles_per_seq = od.shape[2] // tm
    col = pl.BlockSpec((1, BRANCH_WIDTH, tm), lambda i: (i // tiles_per_seq, 0, i % tiles_per_seq))
    return pl.pallas_call(
        _merge_kernel,
        grid=(n // tm,),
        in_specs=[
            pl.BlockSpec((tm, D_MODEL), row),
            col,
            col,
            pl.BlockSpec((1, D_MODEL), const),
            pl.BlockSpec(w_gate.shape, const),
            pl.BlockSpec(w_pa.shape, const),
            pl.BlockSpec(w_pb.shape, const),
            pl.BlockSpec(w_o.shape, const),
        ],
        out_specs=pl.BlockSpec((tm, D_MODEL), row),
        out_shape=jax.ShapeDtypeStruct((n, D_MODEL), F32),
        compiler_params=pltpu.CompilerParams(
            dimension_semantics=("parallel",), vmem_limit_bytes=VMEM_LIMIT),
        name="merge",
    )(x2, od, of, g_mix, w_gate, w_pa, w_pb, w_o)


FF_CHUNK = 1024


def _mlp_kernel(final_norm, x_ref, g_ref, w1_ref, w2_ref, gf_ref, y_ref):
    x = x_ref[...]
    h = _rms(x, g_ref[...]).astype(BF16)
    y = x
    for c in range(D_FF // FF_CHUNK):
        cols = slice(c * FF_CHUNK, (c + 1) * FF_CHUNK)
        u = jnp.maximum(jnp.dot(h, w1_ref[:, cols], preferred_element_type=F32), 0.0)
        y = y + jnp.dot((u * u).astype(BF16), w2_ref[cols, :], preferred_element_type=F32)
    y_ref[...] = _rms(y, gf_ref[...]) if final_norm else y


def _mlp(x2, g_mlp, w_1, w_2, g_final, final_norm):
    n = x2.shape[0]
    tm = TM_PROJ
    const = lambda i: (0, 0)
    row = lambda i: (i, 0)
    single = pl.Buffered(1)
    return pl.pallas_call(
        functools.partial(_mlp_kernel, final_norm),
        grid=(n // tm,),
        in_specs=[
            pl.BlockSpec((tm, D_MODEL), row),
            pl.BlockSpec((1, D_MODEL), const),
            pl.BlockSpec(w_1.shape, const, pipeline_mode=single),
            pl.BlockSpec(w_2.shape, const, pipeline_mode=single),
            pl.BlockSpec((1, D_MODEL), const),
        ],
        out_specs=pl.BlockSpec((tm, D_MODEL), row),
        out_shape=jax.ShapeDtypeStruct((n, D_MODEL), F32),
        compiler_params=pltpu.CompilerParams(
            dimension_semantics=("parallel",), vmem_limit_bytes=VMEM_LIMIT),
        name="mlp",
    )(x2, g_mlp, w_1, w_2, g_final)


def _layer(x, layer_idx, g_mix, w_in, b_f, lam_q1, lam_k1, lam_q2, lam_k2, g_subln,
           w_pa, w_pb, w_o, g_mlp, w_1, w_2, bias_near, g_final, final_norm):
    batch, seq, d = x.shape
    n = batch * seq
    x2 = x.reshape(n, d)
    qkv_cols = 6 * BRANCH_WIDTH
    pad = LANES - DECAY_PARTS * FOX_HEADS
    w_tail = lax.optimization_barrier(w_in[:, qkv_cols:])
    w_fl = jnp.pad(jnp.tile(w_tail[:, :FOX_HEADS], (1, DECAY_PARTS)),
                   ((0, 0), (0, pad))).astype(BF16)
    b_fl = jnp.pad(jnp.tile(b_f, DECAY_PARTS), (0, pad)).reshape(1, LANES)
    w_gate = w_tail[:, FOX_HEADS:].astype(BF16)
    g_mix2 = g_mix.reshape(1, d)

    qd, kd, vd, qf, kf, vf, decay = _in_proj(x2, g_mix2, w_in, w_fl, b_fl, seq)

    lam_vecs = jnp.stack([lam_q1, lam_k1, lam_q2, lam_k2]).astype(F32)
    shape3 = (batch, seq, BRANCH_WIDTH)
    shape4 = (2,) + shape3
    od, of = _attention(lam_vecs, g_subln, bias_near,
                        qd.reshape(shape4), kd.reshape(shape3), vd,
                        qf.reshape(shape4), kf.reshape(shape3), vf,
                        decay.reshape(batch, seq, LANES), _lambda_init(layer_idx))

    x1 = _merge(x2, od, of, g_mix2,
                w_gate, w_pa.astype(BF16), w_pb.astype(BF16), w_o.astype(BF16))
    y = _mlp(x1, g_mlp.reshape(1, d), w_1.astype(BF16), w_2.astype(BF16), g_final, final_norm)
    return y.reshape(batch, seq, d)


def kernel(x, g_mix, w_in, b_f, lam_q1, lam_k1, lam_q2, lam_k2, g_subln, w_pa, w_pb, w_o,
           g_mlp, w_1, w_2, rel_table, g_final):
    depth = g_mix.shape[0]
    batch, seq, d = x.shape
    assert d == D_MODEL and w_in.shape[1:] == (D_MODEL, 6 * BRANCH_WIDTH + FOX_HEADS + 2 * D_MODEL)
    assert w_1.shape[1:] == (D_MODEL, D_FF) and rel_table.shape == (REL_BUCKETS, DIFF_HEADS)
    assert seq % (Q_BLOCKS * TQ) == 0 and seq % TM_IN_PROJ == 0 and seq % TM_PROJ == 0
    bias_near = _bias_tiles(rel_table, seq)
    for l in range(depth):
        x = _layer(x, l, g_mix[l], w_in[l], b_f[l], lam_q1[l], lam_k1[l], lam_q2[l], lam_k2[l],
                   g_subln[l], w_pa[l], w_pb[l], w_o[l], g_mlp[l], w_1[l], w_2[l],
                   bias_near, g_final.reshape(1, -1), l == depth - 1)
    return x
```
